```python
import math
import jax, jax.numpy as jnp
from jax import lax
import numpy as np

D_MODEL = 1024
BATCH = 2
SEQ = 8192
DEPTH = 4

HEAD_DIM = 64
H_FOX = 8
H_SB = 8
N_HEADS = H_FOX + H_SB
MIX_WIDTH = N_HEADS * HEAD_DIM
ATTN_IN = 3 * MIX_WIDTH + H_FOX
CONV_WIDTH = D_MODEL
CONV_K = 3
FFN_CONV_K = 3
D_FF = 2816
Q_BLOCK = 128
N_ATTN = (DEPTH + 1) // 2
N_CONV = DEPTH // 2
EPS = 1e-6

kernel_name = 'fox_stickbreak_shortconv_hybrid'


def rms_norm(x, g):
    xf = x.astype(jnp.float32)
    y = xf * lax.rsqrt(jnp.mean(xf * xf, axis=-1, keepdims=True) + EPS)
    return (y * g.astype(jnp.float32)).astype(x.dtype)


def causal_dwconv(x, w):
    k = w.shape[0]
    s = x.shape[1]
    xp = jnp.pad(x, ((0, 0), (k - 1, 0), (0, 0)))
    return sum(xp[:, j:j + s] * w[j] for j in range(k))


def split_query_blocks(t):
    b, h, s = t.shape[:3]
    rest = t.shape[3:]
    t = t.reshape((b, h, s // Q_BLOCK, Q_BLOCK) + rest)
    return jnp.moveaxis(t, 2, 0)


def merge_query_blocks(t):
    nb, b, h, qb, d = t.shape
    return jnp.moveaxis(t, 0, 2).reshape(b, h, nb * qb, d)


def forgetting_attention(q, k, v, log_f):
    s, dh = q.shape[2], q.shape[3]
    cum_f = jnp.cumsum(log_f, axis=-1)
    key_pos = jnp.arange(s)
    scale = dh ** -0.5
    nb = s // Q_BLOCK

    def one_block(args):
        i, qi, fi = args
        q_pos = i * Q_BLOCK + jnp.arange(Q_BLOCK)
        logits = jnp.einsum('bhqd,bhkd->bhqk', qi, k,
                            preferred_element_type=jnp.float32) * scale
        logits = logits + fi[..., :, None] - cum_f[:, :, None, :]
        causal = key_pos[None, :] <= q_pos[:, None]
        logits = jnp.where(causal, logits, -jnp.inf)
        p = jax.nn.softmax(logits, axis=-1)
        return jnp.einsum('bhqk,bhkd->bhqd', p.astype(v.dtype), v)

    out = lax.map(one_block, (jnp.arange(nb), split_query_blocks(q), split_query_blocks(cum_f)))
    return merge_query_blocks(out)


def stick_breaking_attention(q, k, v):
    s, dh = q.shape[2], q.shape[3]
    key_pos = jnp.arange(s)
    scale = dh ** -0.5
    nb = s // Q_BLOCK

    def one_block(args):
        i, qi = args
        q_pos = i * Q_BLOCK + jnp.arange(Q_BLOCK)
        z = jnp.einsum('bhqd,bhkd->bhqk', qi, k,
                       preferred_element_type=jnp.float32) * scale
        strict = key_pos[None, :] < q_pos[:, None]
        log_beta = jax.nn.log_sigmoid(z)
        log_one_minus = jnp.where(strict, jax.nn.log_sigmoid(-z), 0.0)
        key_axis = log_one_minus.ndim - 1
        later = lax.cumsum(log_one_minus, axis=key_axis, reverse=True) - log_one_minus
        w = jnp.where(strict, jnp.exp(log_beta + later), 0.0)
        return jnp.einsum('bhqk,bhkd->bhqd', w.astype(v.dtype), v)

    out = lax.map(one_block, (jnp.arange(nb), split_query_blocks(q)))
    return merge_query_blocks(out)


def attention_mixer(h, norm_g, w_in, f_bias, fox_q_g, fox_k_g, sb_q_g, sb_k_g, w_out):
    b, s, _ = h.shape
    xn = rms_norm(h, norm_g)
    proj = xn @ w_in

    def heads(t):
        return t.reshape(b, s, N_HEADS, HEAD_DIM).transpose(0, 2, 1, 3)

    q = heads(proj[..., :MIX_WIDTH])
    k = heads(proj[..., MIX_WIDTH:2 * MIX_WIDTH])
    v = heads(proj[..., 2 * MIX_WIDTH:3 * MIX_WIDTH])
    f_logit = proj[..., 3 * MIX_WIDTH:].astype(jnp.float32) + f_bias.astype(jnp.float32)
    log_f = jax.nn.log_sigmoid(f_logit).transpose(0, 2, 1)

    q_fox = rms_norm(q[:, :H_FOX], fox_q_g)
    k_fox = rms_norm(k[:, :H_FOX], fox_k_g)
    q_sb = rms_norm(q[:, H_FOX:], sb_q_g)
    k_sb = rms_norm(k[:, H_FOX:], sb_k_g)

    o_fox = forgetting_attention(q_fox, k_fox, v[:, :H_FOX], log_f)
    o_sb = stick_breaking_attention(q_sb, k_sb, v[:, H_FOX:])
    o = jnp.concatenate([o_fox, o_sb], axis=1)
    o = o.transpose(0, 2, 1, 3).reshape(b, s, MIX_WIDTH)
    return o @ w_out


def short_conv_mixer(h, norm_g, w_in, conv_w, w_out):
    xn = rms_norm(h, norm_g)
    proj = xn @ w_in
    gate_b = proj[..., :CONV_WIDTH]
    gate_c = proj[..., CONV_WIDTH:2 * CONV_WIDTH]
    u = proj[..., 2 * CONV_WIDTH:]
    y = gate_b * causal_dwconv(gate_c * u, conv_w)
    return y @ w_out


def conv_ffn(h, norm_g, w_up, conv_w, w_down):
    xn = rms_norm(h, norm_g)
    u = causal_dwconv(xn @ w_up, conv_w)
    g, val = u[..., :D_FF], u[..., D_FF:]
    return (jax.nn.silu(g) * val) @ w_down


def setup_inputs(seed: int = 0) -> dict:
    key = jax.random.key(seed)
    ks = jax.random.split(key, 20)
    f32 = jnp.float32
    out_scale = (2 * DEPTH) ** -0.5

    def normal(k, shape, scale):
        return scale * jax.random.normal(k, shape, f32)

    def gain(k, shape):
        return 1.0 + 0.05 * jax.random.normal(k, shape, f32)

    x = jax.random.normal(ks[0], (BATCH, SEQ, D_MODEL), f32)
    attn_norm = gain(ks[1], (N_ATTN, D_MODEL))
    attn_w_in = normal(ks[2], (N_ATTN, D_MODEL, ATTN_IN), D_MODEL ** -0.5)
    attn_f_bias = jnp.linspace(1.0, 6.0, H_FOX, dtype=f32)[None, :] + normal(ks[3], (N_ATTN, H_FOX), 0.1)
    fox_q_gain = gain(ks[4], (N_ATTN, HEAD_DIM))
    fox_k_gain = gain(ks[5], (N_ATTN, HEAD_DIM))
    sb_q_gain = gain(ks[6], (N_ATTN, HEAD_DIM))
    sb_k_gain = gain(ks[7], (N_ATTN, HEAD_DIM))
    attn_w_out = normal(ks[8], (N_ATTN, MIX_WIDTH, D_MODEL), out_scale * MIX_WIDTH ** -0.5)
    conv_norm = gain(ks[9], (N_CONV, D_MODEL))
    conv_w_in = normal(ks[10], (N_CONV, D_MODEL, 3 * CONV_WIDTH), D_MODEL ** -0.5)
    conv_kernel = normal(ks[11], (N_CONV, CONV_K, CONV_WIDTH), CONV_K ** -0.5)
    conv_w_out = normal(ks[12], (N_CONV, CONV_WIDTH, D_MODEL), out_scale * CONV_WIDTH ** -0.5)
    ffn_norm = gain(ks[13], (DEPTH, D_MODEL))
    ffn_w_up = normal(ks[14], (DEPTH, D_MODEL, 2 * D_FF), D_MODEL ** -0.5)
    ffn_conv = normal(ks[15], (DEPTH, FFN_CONV_K, 2 * D_FF), FFN_CONV_K ** -0.5)
    ffn_w_down = normal(ks[16], (DEPTH, D_FF, D_MODEL), out_scale * D_FF ** -0.5)
    return {
        'x': x,
        'attn_norm': attn_norm, 'attn_w_in': attn_w_in, 'attn_f_bias': attn_f_bias,
        'fox_q_gain': fox_q_gain, 'fox_k_gain': fox_k_gain,
        'sb_q_gain': sb_q_gain, 'sb_k_gain': sb_k_gain, 'attn_w_out': attn_w_out,
        'conv_norm': conv_norm, 'conv_w_in': conv_w_in, 'conv_kernel': conv_kernel,
        'conv_w_out': conv_w_out,
        'ffn_norm': ffn_norm, 'ffn_w_up': ffn_w_up, 'ffn_conv': ffn_conv, 'ffn_w_down': ffn_w_down,
    }


def reference(x, attn_norm, attn_w_in, attn_f_bias, fox_q_gain, fox_k_gain, sb_q_gain,
              sb_k_gain, attn_w_out, conv_norm, conv_w_in, conv_kernel, conv_w_out,
              ffn_norm, ffn_w_up, ffn_conv, ffn_w_down):
    h = x
    for layer in range(DEPTH):
        i = layer // 2
        if layer % 2 == 0:
            h = h + attention_mixer(h, attn_norm[i], attn_w_in[i], attn_f_bias[i],
                                    fox_q_gain[i], fox_k_gain[i], sb_q_gain[i], sb_k_gain[i],
                                    attn_w_out[i])
        else:
            h = h + short_conv_mixer(h, conv_norm[i], conv_w_in[i], conv_kernel[i], conv_w_out[i])
        h = h + conv_ffn(h, ffn_norm[layer], ffn_w_up[layer], ffn_conv[layer], ffn_w_down[layer])
    return h
```

```python
import functools

import jax
import jax.numpy as jnp
from jax import lax
from jax.experimental import pallas as pl
from jax.experimental.pallas import tpu as pltpu

F32 = jnp.float32
BF16 = jnp.bfloat16

D_MODEL = 1024
HEAD_DIM = 64
H_FOX = 8
H_SB = 8
MIX_WIDTH = (H_FOX + H_SB) * HEAD_DIM
D_FF = 2816
EPS = 1e-6
QK_SCALE = HEAD_DIM ** -0.5

LANES = 128
SUBLANES = 8
HEADS_PER_BLOCK = LANES // HEAD_DIM
MXU_DIM = 256
VMEM_LIMIT = 56 * 1024 * 1024

TM_PROJ = 256
TM_FFN = 512
FF_CHUNK = 256
T_ATTN = 256
CONV_HALO = SUBLANES


def _params(sem):
    return pltpu.CompilerParams(dimension_semantics=sem, vmem_limit_bytes=VMEM_LIMIT)


def _resident(shape):
    zeros = (0,) * len(shape)
    return pl.BlockSpec(shape, lambda *_: zeros, pipeline_mode=pl.Buffered(1))


def _rms_norm(x, g):
    ms = jnp.mean(x * x, axis=-1, keepdims=True)
    return x * lax.rsqrt(ms + EPS) * g


def _split3(x):
    hi = x.astype(BF16)
    r = x - hi.astype(F32)
    mid = r.astype(BF16)
    lo = (r - mid.astype(F32)).astype(BF16)
    return hi, mid, lo


def _split2(x):
    hi = x.astype(BF16)
    lo = (x - hi.astype(F32)).astype(BF16)
    return hi, lo


def _causal_conv3(ext_ref, tail, cur, w, tm):
    ext_ref[0:CONV_HALO, :] = tail
    ext_ref[CONV_HALO:CONV_HALO + tm, :] = cur
    return (w[2:3, :] * cur
            + w[1:2, :] * ext_ref[CONV_HALO - 1:CONV_HALO - 1 + tm, :]
            + w[0:1, :] * ext_ref[CONV_HALO - 2:CONV_HALO - 2 + tm, :])


def _attn_inproj_kernel(h_ref, g_ref, w_ref, wft_ref, fb_ref, qkg_ref, gmat_ref, tri_ref,
                        q_ref, k_ref, v_ref, cf_ref, carry_ref, *, tm):
    i = pl.program_id(1)

    @pl.when(i == 0)
    def _():
        carry_ref[...] = jnp.zeros_like(carry_ref)

    xn = _rms_norm(h_ref[...], g_ref[...]).astype(BF16)
    gmat = gmat_ref[...]

    def head_norm(col0, gain_col0, out_ref, scale):
        for c in range(MIX_WIDTH // MXU_DIM):
            lo = c * MXU_DIM
            t = jnp.dot(xn, w_ref[:, col0 + lo:col0 + lo + MXU_DIM],
                        preferred_element_type=F32)
            ssq = jnp.dot((t * t).astype(BF16), gmat, preferred_element_type=F32)
            gain = qkg_ref[:, gain_col0 + lo:gain_col0 + lo + MXU_DIM]
            tn = t * lax.rsqrt(ssq * (1.0 / HEAD_DIM) + EPS) * gain
            if scale != 1.0:
                tn = tn * scale
            out_ref[:, lo:lo + MXU_DIM] = tn.astype(BF16)

    head_norm(0, 0, q_ref, QK_SCALE)
    head_norm(MIX_WIDTH, MIX_WIDTH, k_ref, 1.0)
    v_ref[...] = jnp.dot(xn, w_ref[:, 2 * MIX_WIDTH:3 * MIX_WIDTH],
                         preferred_element_type=F32).astype(BF16)

    fl = lax.dot_general(wft_ref[...], xn, (((1,), (1,)), ((), ())),
                         preferred_element_type=F32) + fb_ref[...]
    log_f = jnp.minimum(fl, 0.0) - jnp.log1p(jnp.exp(-jnp.abs(fl)))
    tri = tri_ref[...]
    hi, mid, lo = _split3(log_f)
    cs = (jnp.dot(hi, tri, preferred_element_type=F32)
          + jnp.dot(mid, tri, preferred_element_type=F32)
          + jnp.dot(lo, tri, preferred_element_type=F32))
    cf = cs + carry_ref[:, 0:1]
    carry_ref[...] = jnp.broadcast_to(cf[:, tm - 1:tm], carry_ref.shape)
    cf_ref[...] = cf[0:H_FOX, :]


def _attn_inproj(h, g, w_qkv, wft, fb, qk_gain):
    b, s, d = h.shape
    tm = TM_PROJ
    r = lax.broadcasted_iota(jnp.int32, (MXU_DIM, MXU_DIM), 0) // HEAD_DIM
    c = lax.broadcasted_iota(jnp.int32, (MXU_DIM, MXU_DIM), 1) // HEAD_DIM
    gmat = (r == c).astype(BF16)
    rr = lax.broadcasted_iota(jnp.int32, (tm, tm), 0)
    cc = lax.broadcasted_iota(jnp.int32, (tm, tm), 1)
    tri = (rr <= cc).astype(BF16)
    row_spec = pl.BlockSpec((None, tm, MIX_WIDTH), lambda bi, i: (bi, i, 0))
    return pl.pallas_call(
        functools.partial(_attn_inproj_kernel, tm=tm),
        grid=(b, s // tm),
        in_specs=[
            pl.BlockSpec((None, tm, d), lambda bi, i: (bi, i, 0)),
            _resident(g.shape), _resident(w_qkv.shape), _resident(wft.shape),
            _resident(fb.shape), _resident(qk_gain.shape), _resident(gmat.shape),
            _resident(tri.shape),
        ],
        out_specs=[row_spec, row_spec, row_spec,
                   pl.BlockSpec((None, H_FOX, tm), lambda bi, i: (bi, 0, i))],
        out_shape=[jax.ShapeDtypeStruct((b, s, MIX_WIDTH), BF16)] * 3
        + [jax.ShapeDtypeStruct((b, H_FOX, s), F32)],
        scratch_shapes=[pltpu.VMEM((2 * SUBLANES, LANES), F32)],
        compiler_params=_params(("arbitrary", "arbitrary")),
        name="attn_inproj",
    )(h, g, w_qkv, wft, fb, qk_gain, gmat, tri)


def _head_masks(t):
    lane = lax.broadcasted_iota(jnp.int32, (t, LANES), 1)
    return [(lane >= HEAD_DIM * hh) & (lane < HEAD_DIM * (hh + 1))
            for hh in range(HEADS_PER_BLOCK)]


def _fox_kernel(q_ref, k_ref, v_ref, cf_ref, o_ref, m_ref, l_ref, acc_ref, *, t):
    i = pl.program_id(2)
    q = q_ref[...]
    masks = _head_masks(t)
    qh = [jnp.where(mk, q, jnp.zeros_like(q)) for mk in masks]
    m_ref[...] = jnp.full_like(m_ref, -jnp.inf)
    l_ref[...] = jnp.zeros_like(l_ref)
    acc_ref[...] = jnp.zeros_like(acc_ref)
    row = lax.broadcasted_iota(jnp.int32, (t, t), 0)
    col = lax.broadcasted_iota(jnp.int32, (t, t), 1)

    def step(j, masked):
        off = pl.multiple_of(j * t, t)
        kt = k_ref[pl.ds(off, t), :]
        vt = v_ref[pl.ds(off, t), :]
        for hh in range(HEADS_PER_BLOCK):
            sc = lax.dot_general(qh[hh], kt, (((1,), (1,)), ((), ())),
                                 preferred_element_type=F32)
            sc = sc - cf_ref[hh:hh + 1, pl.ds(off, t)]
            if masked:
                sc = jnp.where(col <= row, sc, -jnp.inf)
            m_old = m_ref[hh]
            m_new = jnp.maximum(m_old, jnp.max(sc, axis=1, keepdims=True))
            alpha = jnp.exp(m_old - m_new)
            p = jnp.exp(sc - m_new)
            l_ref[hh] = alpha * l_ref[hh] + jnp.sum(p, axis=1, keepdims=True)
            acc_ref[hh] = alpha * acc_ref[hh] + jnp.dot(p.astype(BF16), vt,
                                                        preferred_element_type=F32)
            m_ref[hh] = m_new

    def body(j, carry):
        step(j, False)
        return carry

    lax.fori_loop(0, i, body, 0)
    step(i, True)
    out = [acc_ref[hh] / l_ref[hh] for hh in range(HEADS_PER_BLOCK)]
    o_ref[...] = jnp.where(masks[0], out[0], out[1]).astype(o_ref.dtype)


def _fox_attention(q, k, v, cf):
    b, s, _ = q.shape
    t = T_ATTN
    nblk = H_FOX // HEADS_PER_BLOCK
    cf4 = cf.reshape(b, nblk, HEADS_PER_BLOCK, s)
    return pl.pallas_call(
        functools.partial(_fox_kernel, t=t),
        grid=(b, nblk, s // t),
        in_specs=[
            pl.BlockSpec((None, t, LANES), lambda bi, p, i: (bi, i, p)),
            pl.BlockSpec((None, s, LANES), lambda bi, p, i: (bi, 0, p)),
            pl.BlockSpec((None, s, LANES), lambda bi, p, i: (bi, 0, p)),
            pl.BlockSpec((None, None, HEADS_PER_BLOCK, s), lambda bi, p, i: (bi, p, 0, 0)),
        ],
        out_specs=pl.BlockSpec((None, t, LANES), lambda bi, p, i: (bi, i, p)),
        out_shape=jax.ShapeDtypeStruct((b, s, H_FOX * HEAD_DIM), BF16),
        scratch_shapes=[pltpu.VMEM((HEADS_PER_BLOCK, t, 1), F32),
                        pltpu.VMEM((HEADS_PER_BLOCK, t, 1), F32),
                        pltpu.VMEM((HEADS_PER_BLOCK, t, LANES), F32)],
        compiler_params=_params(("arbitrary", "arbitrary", "arbitrary")),
        name="fox_attention",
    )(q, k, v, cf4)


def _sb_kernel(q_ref, k_ref, v_ref, tri_ref, o_ref, r_ref, acc_ref, *, t):
    i = pl.program_id(2)
    q = q_ref[...]
    masks = _head_masks(t)
    qh = [jnp.where(mk, q, jnp.zeros_like(q)) for mk in masks]
    r_ref[...] = jnp.zeros_like(r_ref)
    acc_ref[...] = jnp.zeros_like(acc_ref)
    row = lax.broadcasted_iota(jnp.int32, (t, t), 0)
    col = lax.broadcasted_iota(jnp.int32, (t, t), 1)
    strict = col < row

    def step(j, masked):
        off = pl.multiple_of(j * t, t)
        kt = k_ref[pl.ds(off, t), :]
        vt = v_ref[pl.ds(off, t), :]
        tri = tri_ref[...]
        for hh in range(HEADS_PER_BLOCK):
            z = lax.dot_general(qh[hh], kt, (((1,), (1,)), ((), ())),
                                preferred_element_type=F32)
            a = jnp.maximum(z, 0.0) + jnp.log1p(jnp.exp(-jnp.abs(z)))
            if masked:
                a = jnp.where(strict, a, 0.0)
            a_hi, a_lo = _split2(a)
            later = (jnp.dot(a_hi, tri, preferred_element_type=F32)
                     + jnp.dot(a_lo, tri, preferred_element_type=F32))
            r_old = r_ref[hh]
            arg = z - a - later - r_old
            if masked:
                arg = jnp.where(strict, arg, -jnp.inf)
            w = jnp.exp(arg)
            acc_ref[hh] = acc_ref[hh] + jnp.dot(w.astype(BF16), vt,
                                                preferred_element_type=F32)
            r_ref[hh] = r_old + later[:, 0:1] + a[:, 0:1]

    step(i, True)

    def body(kk, carry):
        step(i - 1 - kk, False)
        return carry

    lax.fori_loop(0, i, body, 0)
    o_ref[...] = jnp.where(masks[0], acc_ref[0], acc_ref[1]).astype(o_ref.dtype)


def _sb_attention(q, k, v):
    b, s, _ = q.shape
    t = T_ATTN
    nblk = H_SB // HEADS_PER_BLOCK
    first = H_FOX // HEADS_PER_BLOCK
    rr = lax.broadcasted_iota(jnp.int32, (t, t), 0)
    cc = lax.broadcasted_iota(jnp.int32, (t, t), 1)
    tri = (rr > cc).astype(BF16)
    return pl.pallas_call(
        functools.partial(_sb_kernel, t=t),
        grid=(b, nblk, s // t),
        in_specs=[
            pl.BlockSpec((None, t, LANES), lambda bi, p, i: (bi, i, p + first)),
            pl.BlockSpec((None, s, LANES), lambda bi, p, i: (bi, 0, p + first)),
            pl.BlockSpec((None, s, LANES), lambda bi, p, i: (bi, 0, p + first)),
            _resident(tri.shape),
        ],
        out_specs=pl.BlockSpec((None, t, LANES), lambda bi, p, i: (bi, i, p)),
        out_shape=jax.ShapeDtypeStruct((b, s, H_SB * HEAD_DIM), BF16),
        scratch_shapes=[pltpu.VMEM((HEADS_PER_BLOCK, t, 1), F32),
                        pltpu.VMEM((HEADS_PER_BLOCK, t, LANES), F32)],
        compiler_params=_params(("arbitrary", "arbitrary", "arbitrary")),
        name="sb_attention",
    )(q, k, v, tri)


def _attn_out_kernel(h_ref, of_ref, os_ref, wf_ref, ws_ref, o_ref):
    o_ref[...] = (h_ref[...]
                  + jnp.dot(of_ref[...], wf_ref[...], preferred_element_type=F32)
                  + jnp.dot(os_ref[...], ws_ref[...], preferred_element_type=F32))


def _attn_out(h, o_fox, o_sb, w_fox, w_sb):
    b, s, d = h.shape
    tm = TM_PROJ
    return pl.pallas_call(
        _attn_out_kernel,
        grid=(b, s // tm),
        in_specs=[
            pl.BlockSpec((None, tm, d), lambda bi, i: (bi, i, 0)),
            pl.BlockSpec((None, tm, o_fox.shape[-1]), lambda bi, i: (bi, i, 0)),
            pl.BlockSpec((None, tm, o_sb.shape[-1]), lambda bi, i: (bi, i, 0)),
            _resident(w_fox.shape), _resident(w_sb.shape),
        ],
        out_specs=pl.BlockSpec((None, tm, d), lambda bi, i: (bi, i, 0)),
        out_shape=jax.ShapeDtypeStruct(h.shape, F32),
        compiler_params=_params(("arbitrary", "arbitrary")),
        name="attn_out",
    )(h, o_fox, o_sb, w_fox, w_sb)


def _conv_mixer_kernel(h_ref, g_ref, win_ref, cw_ref, wout_ref, o_ref, ext_ref, tail_ref,
                       *, tm):
    i = pl.program_id(1)

    @pl.when(i == 0)
    def _():
        tail_ref[...] = jnp.zeros_like(tail_ref)

    x = h_ref[...]
    xn = _rms_norm(x, g_ref[...]).astype(BF16)
    d = x.shape[-1]
    gate_c = jnp.dot(xn, win_ref[:, d:2 * d], preferred_element_type=F32)
    u = jnp.dot(xn, win_ref[:, 2 * d:3 * d], preferred_element_type=F32)
    cu = gate_c * u
    y = _causal_conv3(ext_ref, tail_ref[...], cu, cw_ref[...], tm)
    tail_ref[...] = cu[tm - CONV_HALO:tm, :]
    gate_b = jnp.dot(xn, win_ref[:, 0:d], preferred_element_type=F32)
    y = (gate_b * y).astype(BF16)
    o_ref[...] = x + jnp.dot(y, wout_ref[...], preferred_element_type=F32)


def _conv_mixer(h, g, w_in, conv_w, w_out):
    b, s, d = h.shape
    tm = TM_PROJ
    return pl.pallas_call(
        functools.partial(_conv_mixer_kernel, tm=tm),
        grid=(b, s // tm),
        in_specs=[
            pl.BlockSpec((None, tm, d), lambda bi, i: (bi, i, 0)),
            _resident(g.shape), _resident(w_in.shape), _resident(conv_w.shape),
            _resident(w_out.shape),
        ],
        out_specs=pl.BlockSpec((None, tm, d), lambda bi, i: (bi, i, 0)),
        out_shape=jax.ShapeDtypeStruct(h.shape, F32),
        scratch_shapes=[pltpu.VMEM((tm + CONV_HALO, d), F32),
                        pltpu.VMEM((CONV_HALO, d), F32)],
        compiler_params=_params(("arbitrary", "arbitrary")),
        name="conv_mixer",
    )(h, g, w_in, conv_w, w_out)


def _ffn_kernel(h_ref, g_ref, wg_ref, wv_ref, cg_ref, cv_ref, wd_ref, o_ref,
                xn_ref, extg_ref, extv_ref, tailg_ref, tailv_ref, *, tm, n_chunks):
    i = pl.program_id(1)

    @pl.when(i == 0)
    def _():
        tailg_ref[...] = jnp.zeros_like(tailg_ref)
        tailv_ref[...] = jnp.zeros_like(tailv_ref)

    x = h_ref[...]
    xn_ref[...] = _rms_norm(x, g_ref[...]).astype(BF16)
    o_ref[...] = x

    def chunk(c, carry):
        xn = xn_ref[...]
        ug = jnp.dot(xn, wg_ref[c], preferred_element_type=F32)
        uv = jnp.dot(xn, wv_ref[c], preferred_element_type=F32)
        yg = _causal_conv3(extg_ref, tailg_ref[c], ug, cg_ref[c], tm)
        yv = _causal_conv3(extv_ref, tailv_ref[c], uv, cv_ref[c], tm)
        tailg_ref[c] = ug[tm - CONV_HALO:tm, :]
        tailv_ref[c] = uv[tm - CONV_HALO:tm, :]
        act = (yg / (1.0 + jnp.exp(-yg)) * yv).astype(BF16)
        o_ref[...] += jnp.dot(act, wd_ref[c], preferred_element_type=F32)
        return carry

    lax.fori_loop(0, n_chunks, chunk, 0)


def _ffn(h, g, wg, wv, cg, cv, wd):
    b, s, d = h.shape
    tm = TM_FFN
    n_chunks, _, fc = wg.shape
    return pl.pallas_call(
        functools.partial(_ffn_kernel, tm=tm, n_chunks=n_chunks),
        grid=(b, s // tm),
        in_specs=[
            pl.BlockSpec((None, tm, d), lambda bi, i: (bi, i, 0)),
            _resident(g.shape), _resident(wg.shape), _resident(wv.shape),
            _resident(cg.shape), _resident(cv.shape), _resident(wd.shape),
        ],
        out_specs=pl.BlockSpec((None, tm, d), lambda bi, i: (bi, i, 0)),
        out_shape=jax.ShapeDtypeStruct(h.shape, F32),
        scratch_shapes=[pltpu.VMEM((tm, d), BF16),
                        pltpu.VMEM((tm + CONV_HALO, fc), F32),
                        pltpu.VMEM((tm + CONV_HALO, fc), F32),
                        pltpu.VMEM((n_chunks, CONV_HALO, fc), F32),
                        pltpu.VMEM((n_chunks, CONV_HALO, fc), F32)],
        compiler_params=_params(("arbitrary", "arbitrary")),
        name="conv_ffn",
    )(h, g, wg, wv, cg, cv, wd)


def _ffn_weights(w_up, conv_w, w_down):
    fc = FF_CHUNK
    n_chunks = D_FF // fc
    d = w_up.shape[0]

    def up(w):
        return w.reshape(d, n_chunks, fc).transpose(1, 0, 2).astype(BF16)

    def cw(w):
        return w.reshape(w.shape[0], n_chunks, fc).transpose(1, 0, 2)

    return (up(w_up[:, :D_FF]), up(w_up[:, D_FF:]), cw(conv_w[:, :D_FF]), cw(conv_w[:, D_FF:]),
            w_down.reshape(n_chunks, fc, d).astype(BF16))


def kernel(x, attn_norm, attn_w_in, attn_f_bias, fox_q_gain, fox_k_gain, sb_q_gain, sb_k_gain,
           attn_w_out, conv_norm, conv_w_in, conv_kernel, conv_w_out, ffn_norm, ffn_w_up,
           ffn_conv, ffn_w_down):
    depth = ffn_norm.shape[0]
    fox_w = H_FOX * HEAD_DIM
    h = x
    for layer in range(depth):
        i = layer // 2
        if layer % 2 == 0:
            w_in = attn_w_in[i]
            w_qkv = w_in[:, :3 * MIX_WIDTH].astype(BF16)
            wft = jnp.zeros((2 * SUBLANES, D_MODEL), BF16).at[:H_FOX].set(
                w_in[:, 3 * MIX_WIDTH:].T.astype(BF16))
            fb = jnp.zeros((2 * SUBLANES, 1), F32).at[:H_FOX, 0].set(attn_f_bias[i])
            qk_gain = jnp.concatenate(
                [jnp.tile(fox_q_gain[i], H_FOX), jnp.tile(sb_q_gain[i], H_SB),
                 jnp.tile(fox_k_gain[i], H_FOX), jnp.tile(sb_k_gain[i], H_SB)])[None, :]
            q, k, v, cf = _attn_inproj(h, attn_norm[i][None, :], w_qkv, wft, fb, qk_gain)
            o_fox = _fox_attention(q, k, v, cf)
            o_sb = _sb_attention(q, k, v)
            w_out = attn_w_out[i].astype(BF16)
            h = _attn_out(h, o_fox, o_sb, w_out[:fox_w], w_out[fox_w:])
        else:
            h = _conv_mixer(h, conv_norm[i][None, :], conv_w_in[i].astype(BF16),
                            conv_kernel[i], conv_w_out[i].astype(BF16))
        h = _ffn(h, ffn_norm[layer][None, :],
                 *_ffn_weights(ffn_w_up[layer], ffn_conv[layer], ffn_w_down[layer]))
    return h
```

```python
import functools

import jax
import jax.numpy as jnp
from jax import lax
from jax.experimental import pallas as pl
from jax.experimental.pallas import tpu as pltpu

F32 = jnp.float32
BF16 = jnp.bfloat16

D_MODEL = 1024
HEAD_DIM = 64
H_FOX = 8
H_SB = 8
MIX_WIDTH = (H_FOX + H_SB) * HEAD_DIM
D_FF = 2816
EPS = 1e-6
QK_SCALE = HEAD_DIM ** -0.5
LOG2E = 1.4426950408889634

LANES = 128
SUBLANES = 8
HEADS_PER_BLOCK = LANES // HEAD_DIM
MXU_DIM = 256
VMEM_LIMIT = 56 * 1024 * 1024

TM_PROJ = 256
TM_FFN = 512
FF_CHUNK = 256
T_ATTN = 512
CONV_HALO = SUBLANES


def _params(sem):
    return pltpu.CompilerParams(dimension_semantics=sem, vmem_limit_bytes=VMEM_LIMIT)


def _resident(shape):
    zeros = (0,) * len(shape)
    return pl.BlockSpec(shape, lambda *_: zeros, pipeline_mode=pl.Buffered(1))


def _rms_norm(x, g):
    ms = jnp.mean(x * x, axis=-1, keepdims=True)
    return x * lax.rsqrt(ms + EPS) * g


def _split3(x):
    hi = x.astype(BF16)
    r = x - hi.astype(F32)
    mid = r.astype(BF16)
    lo = (r - mid.astype(F32)).astype(BF16)
    return hi, mid, lo


def _split2(x):
    hi = x.astype(BF16)
    lo = (x - hi.astype(F32)).astype(BF16)
    return hi, lo


def _neg_abs(x):
    bits = lax.bitcast_convert_type(x, jnp.int32) | jnp.int32(-2 ** 31)
    return lax.bitcast_convert_type(bits, F32)


def _dot_nt(a, b):
    return lax.dot_general(a, b, (((1,), (1,)), ((), ())), preferred_element_type=F32)


def _causal_conv3(ext_ref, tail, cur, w, tm):
    ext_ref[0:CONV_HALO, :] = tail
    ext_ref[CONV_HALO:CONV_HALO + tm, :] = cur
    return (w[2:3, :] * cur
            + w[1:2, :] * ext_ref[CONV_HALO - 1:CONV_HALO - 1 + tm, :]
            + w[0:1, :] * ext_ref[CONV_HALO - 2:CONV_HALO - 2 + tm, :])


def _attn_inproj_kernel(h_ref, g_ref, w_ref, wvt_ref, wf_ref, fb_ref, qkg_ref, gmat_ref,
                        tri_ref, q_ref, k_ref, vt_ref, cf_ref, carry_ref, *, tm):
    i = pl.program_id(1)

    @pl.when(i == 0)
    def _():
        carry_ref[...] = jnp.zeros_like(carry_ref)

    xn = _rms_norm(h_ref[...], g_ref[...]).astype(BF16)
    gmat = gmat_ref[...]

    def head_norm(col0, out_ref, scale):
        for c in range(MIX_WIDTH // MXU_DIM):
            lo = c * MXU_DIM
            t = jnp.dot(xn, w_ref[:, col0 + lo:col0 + lo + MXU_DIM],
                        preferred_element_type=F32)
            ssq = jnp.dot((t * t).astype(BF16), gmat, preferred_element_type=F32)
            gain = qkg_ref[:, col0 + lo:col0 + lo + MXU_DIM]
            tn = t * lax.rsqrt(ssq * (1.0 / HEAD_DIM) + EPS) * gain
            if scale != 1.0:
                tn = tn * scale
            out_ref[:, lo:lo + MXU_DIM] = tn.astype(BF16)

    head_norm(0, q_ref, QK_SCALE * LOG2E)
    head_norm(MIX_WIDTH, k_ref, 1.0)
    vt_ref[...] = _dot_nt(wvt_ref[...], xn).astype(BF16)

    fl = jnp.dot(xn, wf_ref[...], preferred_element_type=F32) + fb_ref[...]
    log_f = jnp.minimum(fl, 0.0) - jnp.log1p(jnp.exp(-jnp.abs(fl)))
    tri = tri_ref[...]
    hi, mid, lo = _split3(log_f)
    cs = (jnp.dot(tri, hi, preferred_element_type=F32)
          + jnp.dot(tri, mid, preferred_element_type=F32)
          + jnp.dot(tri, lo, preferred_element_type=F32))
    cf = cs + carry_ref[0:1, :]
    carry_ref[...] = jnp.broadcast_to(cf[tm - 1:tm, :], carry_ref.shape)
    cf_ref[...] = cf


def _attn_inproj(h, g, w_qk, wvt, wf, fb, qk_gain):
    b, s, d = h.shape
    tm = TM_PROJ
    r = lax.broadcasted_iota(jnp.int32, (MXU_DIM, MXU_DIM), 0) // HEAD_DIM
    c = lax.broadcasted_iota(jnp.int32, (MXU_DIM, MXU_DIM), 1) // HEAD_DIM
    gmat = (r == c).astype(BF16)
    rr = lax.broadcasted_iota(jnp.int32, (tm, tm), 0)
    cc = lax.broadcasted_iota(jnp.int32, (tm, tm), 1)
    tri = (cc <= rr).astype(BF16)
    row_spec = pl.BlockSpec((None, tm, MIX_WIDTH), lambda bi, i: (bi, i, 0))
    return pl.pallas_call(
        functools.partial(_attn_inproj_kernel, tm=tm),
        grid=(b, s // tm),
        in_specs=[
            pl.BlockSpec((None, tm, d), lambda bi, i: (bi, i, 0)),
            _resident(g.shape), _resident(w_qk.shape), _resident(wvt.shape),
            _resident(wf.shape), _resident(fb.shape), _resident(qk_gain.shape),
            _resident(gmat.shape), _resident(tri.shape),
        ],
        out_specs=[row_spec, row_spec,
                   pl.BlockSpec((None, MIX_WIDTH, tm), lambda bi, i: (bi, 0, i)),
                   pl.BlockSpec((None, tm, LANES), lambda bi, i: (bi, i, 0))],
        out_shape=[jax.ShapeDtypeStruct((b, s, MIX_WIDTH), BF16),
                   jax.ShapeDtypeStruct((b, s, MIX_WIDTH), BF16),
                   jax.ShapeDtypeStruct((b, MIX_WIDTH, s), BF16),
                   jax.ShapeDtypeStruct((b, s, LANES), F32)],
        scratch_shapes=[pltpu.VMEM((SUBLANES, LANES), F32)],
        compiler_params=_params(("arbitrary", "arbitrary")),
        name="attn_inproj",
    )(h, g, w_qk, wvt, wf, fb, qk_gain, gmat, tri)


def _head_masks(t):
    lane = lax.broadcasted_iota(jnp.int32, (t, LANES), 1)
    return [(lane >= HEAD_DIM * hh) & (lane < HEAD_DIM * (hh + 1))
            for hh in range(HEADS_PER_BLOCK)]


def _lane_pieces(t):
    return [slice(c * LANES, (c + 1) * LANES) for c in range(t // LANES)]


def _sweep_key_tiles(i, scores, consume):
    scores(i, 0)
    scores(jnp.maximum(i - 1, 0), 1)
    consume(i, 0, True)

    def pair(r, carry):
        j = i - 1 - 2 * r
        scores(jnp.maximum(j - 1, 0), 0)
        consume(j, 1, False)
        scores(jnp.maximum(j - 2, 0), 1)
        consume(j - 1, 0, False)
        return carry

    lax.fori_loop(0, i // 2, pair, 0)

    @pl.when(i % 2 == 1)
    def _():
        consume(0, 1, False)


def _write_heads(o_ref, outs_t):
    o_ref[...] = jnp.concatenate(outs_t, axis=0).T.astype(o_ref.dtype)


def _fox_kernel(q_ref, k_ref, vt_ref, cf_ref, o_ref, cfb_ref, st_ref, m_ref, l_ref, acc_ref,
                *, t, fill_rows):
    p = pl.program_id(1)
    i = pl.program_id(2)

    @pl.when(i == 0)
    def _():
        sel_row = lax.broadcasted_iota(jnp.int32, (LANES, LANES), 0)
        for hh in range(HEADS_PER_BLOCK):
            sel = (sel_row == p * HEADS_PER_BLOCK + hh).astype(BF16)

            def fill(r, carry):
                off = pl.multiple_of(r * fill_rows, fill_rows)
                hi, mid, lo = _split3(cf_ref[pl.ds(off, fill_rows), :])
                cfb_ref[hh, pl.ds(off, fill_rows), :] = LOG2E * (
                    jnp.dot(hi, sel, preferred_element_type=F32)
                    + jnp.dot(mid, sel, preferred_element_type=F32)
                    + jnp.dot(lo, sel, preferred_element_type=F32))
                return carry

            lax.fori_loop(0, cf_ref.shape[0] // fill_rows, fill, 0)

    q = q_ref[...]
    qh = [jnp.where(mk, q, jnp.zeros_like(q)) for mk in _head_masks(t)]
    m_ref[...] = jnp.full_like(m_ref, -jnp.inf)
    l_ref[...] = jnp.zeros_like(l_ref)
    acc_ref[...] = jnp.zeros_like(acc_ref)
    key = lax.broadcasted_iota(jnp.int32, (t, LANES), 0)
    qry = lax.broadcasted_iota(jnp.int32, (t, LANES), 1)
    pieces = _lane_pieces(t)

    def scores(j, slot):
        off = pl.multiple_of(j * t, t)
        kt = k_ref[pl.ds(off, t), :]
        for hh in range(HEADS_PER_BLOCK):
            st = _dot_nt(kt, qh[hh])
            fcol = cfb_ref[hh, pl.ds(off, t), :]
            for cols in pieces:
                st_ref[slot, hh, :, cols] = st[:, cols] - fcol

    def consume(j, slot, masked):
        off = pl.multiple_of(j * t, t)
        for hh in range(HEADS_PER_BLOCK):
            m_old = m_ref[hh]
            l_old = l_ref[hh]
            m_new, l_new, alpha, prob = [], [], [], []
            for c, cols in enumerate(pieces):
                x = st_ref[slot, hh, :, cols]
                if masked:
                    x = jnp.where(key <= qry + c * LANES, x, -jnp.inf)
                m_o = m_old[:, cols]
                m_n = jnp.maximum(m_o, jnp.max(x, axis=0, keepdims=True))
                a_c = jnp.exp2(m_o - m_n)
                p_c = jnp.exp2(x - m_n)
                m_new.append(m_n)
                alpha.append(a_c)
                l_new.append(a_c * l_old[:, cols] + jnp.sum(p_c, axis=0, keepdims=True))
                prob.append(p_c.astype(BF16))
            vt_h = vt_ref[hh * HEAD_DIM:(hh + 1) * HEAD_DIM, pl.ds(off, t)]
            pv = jnp.dot(vt_h, jnp.concatenate(prob, axis=1), preferred_element_type=F32)
            acc_ref[hh] = jnp.concatenate(alpha, axis=1) * acc_ref[hh] + pv
            m_ref[hh] = jnp.concatenate(m_new, axis=1)
            l_ref[hh] = jnp.concatenate(l_new, axis=1)

    _sweep_key_tiles(i, scores, consume)
    _write_heads(o_ref, [acc_ref[hh] / l_ref[hh] for hh in range(HEADS_PER_BLOCK)])


def _fox_attention(q, k, vt, cf):
    b, s, _ = q.shape
    t = T_ATTN
    nblk = H_FOX // HEADS_PER_BLOCK
    return pl.pallas_call(
        functools.partial(_fox_kernel, t=t, fill_rows=t),
        grid=(b, nblk, s // t),
        in_specs=[
            pl.BlockSpec((None, t, LANES), lambda bi, p, i: (bi, i, p)),
            pl.BlockSpec((None, s, LANES), lambda bi, p, i: (bi, 0, p)),
            pl.BlockSpec((None, LANES, s), lambda bi, p, i: (bi, p, 0)),
            pl.BlockSpec((None, s, LANES), lambda bi, p, i: (bi, 0, 0)),
        ],
        out_specs=pl.BlockSpec((None, t, LANES), lambda bi, p, i: (bi, i, p)),
        out_shape=jax.ShapeDtypeStruct((b, s, H_FOX * HEAD_DIM), BF16),
        scratch_shapes=[pltpu.VMEM((HEADS_PER_BLOCK, s, LANES), F32),
                        pltpu.VMEM((2, HEADS_PER_BLOCK, t, t), F32),
                        pltpu.VMEM((HEADS_PER_BLOCK, 1, t), F32),
                        pltpu.VMEM((HEADS_PER_BLOCK, 1, t), F32),
                        pltpu.VMEM((HEADS_PER_BLOCK, HEAD_DIM, t), F32)],
        compiler_params=_params(("arbitrary", "arbitrary", "arbitrary")),
        name="fox_attention",
    )(q, k, vt, cf)


def _sb_kernel(q_ref, k_ref, vt_ref, tri_ref, o_ref, z_ref, r_ref, acc_ref, *, t):
    i = pl.program_id(2)
    q = q_ref[...]
    qh = [jnp.where(mk, q, jnp.zeros_like(q)) for mk in _head_masks(t)]
    r_ref[...] = jnp.zeros_like(r_ref)
    acc_ref[...] = jnp.zeros_like(acc_ref)
    key = lax.broadcasted_iota(jnp.int32, (t, t), 0)
    qry = lax.broadcasted_iota(jnp.int32, (t, t), 1)
    strict = key < qry
    n_sub = t // MXU_DIM

    sub_rows = [slice(c * MXU_DIM, (c + 1) * MXU_DIM) for c in range(n_sub)]

    def scores(j, slot):
        off = pl.multiple_of(j * t, t)
        kt = k_ref[pl.ds(off, t), :]
        for hh in range(HEADS_PER_BLOCK):
            z_ref[slot, hh] = _dot_nt(kt, qh[hh])

    def consume(j, slot, masked):
        off = pl.multiple_of(j * t, t)
        tri = tri_ref[...]
        staged = []
        for hh in range(HEADS_PER_BLOCK):
            z = z_ref[slot, hh]
            a = jnp.maximum(z, 0.0) + LOG2E * jnp.log(1.0 + jnp.exp2(_neg_abs(z)))
            base = z - a
            if masked:
                a = jnp.where(strict, a, 0.0)
                base = jnp.where(strict, base, -jnp.inf)
            later = []
            for rows in sub_rows:
                a_hi, a_lo = _split2(a[rows, :])
                later.append(jnp.dot(tri, a_hi, preferred_element_type=F32)
                             + jnp.dot(tri, a_lo, preferred_element_type=F32))
            staged.append((a, base, later))
        for hh in range(HEADS_PER_BLOCK):
            a, base, later = staged[hh]
            r_run = r_ref[hh]
            w = [None] * n_sub
            for c in reversed(range(n_sub)):
                rows = sub_rows[c]
                w[c] = jnp.exp2(base[rows, :] - later[c] - r_run).astype(BF16)
                r_run = r_run + jnp.sum(a[rows, :], axis=0, keepdims=True)
            vt_h = vt_ref[hh * HEAD_DIM:(hh + 1) * HEAD_DIM, pl.ds(off, t)]
            acc_ref[hh] += jnp.dot(vt_h, jnp.concatenate(w, axis=0),
                                   preferred_element_type=F32)
            r_ref[hh] = r_run

    _sweep_key_tiles(i, scores, consume)
    _write_heads(o_ref, [acc_ref[hh] for hh in range(HEADS_PER_BLOCK)])


def _sb_attention(q, k, vt):
    b, s, _ = q.shape
    t = T_ATTN
    nblk = H_SB // HEADS_PER_BLOCK
    first = H_FOX // HEADS_PER_BLOCK
    rr = lax.broadcasted_iota(jnp.int32, (MXU_DIM, MXU_DIM), 0)
    cc = lax.broadcasted_iota(jnp.int32, (MXU_DIM, MXU_DIM), 1)
    tri = (cc > rr).astype(BF16)
    return pl.pallas_call(
        functools.partial(_sb_kernel, t=t),
        grid=(b, nblk, s // t),
        in_specs=[
            pl.BlockSpec((None, t, LANES), lambda bi, p, i: (bi, i, p + first)),
            pl.BlockSpec((None, s, LANES), lambda bi, p, i: (bi, 0, p + first)),
            pl.BlockSpec((None, LANES, s), lambda bi, p, i: (bi, p + first, 0)),
            _resident(tri.shape),
        ],
        out_specs=pl.BlockSpec((None, t, LANES), lambda bi, p, i: (bi, i, p)),
        out_shape=jax.ShapeDtypeStruct((b, s, H_SB * HEAD_DIM), BF16),
        scratch_shapes=[pltpu.VMEM((2, HEADS_PER_BLOCK, t, t), F32),
                        pltpu.VMEM((HEADS_PER_BLOCK, 1, t), F32),
                        pltpu.VMEM((HEADS_PER_BLOCK, HEAD_DIM, t), F32)],
        compiler_params=_params(("arbitrary", "arbitrary", "arbitrary")),
        name="sb_attention",
    )(q, k, vt, tri)


def _attn_out_kernel(h_ref, of_ref, os_ref, wf_ref, ws_ref, o_ref):
    o_ref[...] = (h_ref[...]
                  + jnp.dot(of_ref[...], wf_ref[...], preferred_element_type=F32)
                  + jnp.dot(os_ref[...], ws_ref[...], preferred_element_type=F32))


def _attn_out(h, o_fox, o_sb, w_fox, w_sb):
    b, s, d = h.shape
    tm = TM_PROJ
    return pl.pallas_call(
        _attn_out_kernel,
        grid=(b, s // tm),
        in_specs=[
            pl.BlockSpec((None, tm, d), lambda bi, i: (bi, i, 0)),
            pl.BlockSpec((None, tm, o_fox.shape[-1]), lambda bi, i: (bi, i, 0)),
            pl.BlockSpec((None, tm, o_sb.shape[-1]), lambda bi, i: (bi, i, 0)),
            _resident(w_fox.shape), _resident(w_sb.shape),
        ],
        out_specs=pl.BlockSpec((None, tm, d), lambda bi, i: (bi, i, 0)),
        out_shape=jax.ShapeDtypeStruct(h.shape, F32),
        compiler_params=_params(("arbitrary", "arbitrary")),
        name="attn_out",
    )(h, o_fox, o_sb, w_fox, w_sb)


def _conv_mixer_kernel(h_ref, g_ref, win_ref, cw_ref, wout_ref, o_ref, ext_ref, tail_ref,
                       *, tm):
    i = pl.program_id(1)

    @pl.when(i == 0)
    def _():
        tail_ref[...] = jnp.zeros_like(tail_ref)

    x = h_ref[...]
    xn = _rms_norm(x, g_ref[...]).astype(BF16)
    d = x.shape[-1]
    gate_c = jnp.dot(xn, win_ref[:, d:2 * d], preferred_element_type=F32)
    u = jnp.dot(xn, win_ref[:, 2 * d:3 * d], preferred_element_type=F32)
    cu = gate_c * u
    y = _causal_conv3(ext_ref, tail_ref[...], cu, cw_ref[...], tm)
    tail_ref[...] = cu[tm - CONV_HALO:tm, :]
    gate_b = jnp.dot(xn, win_ref[:, 0:d], preferred_element_type=F32)
    y = (gate_b * y).astype(BF16)
    o_ref[...] = x + jnp.dot(y, wout_ref[...], preferred_element_type=F32)


def _conv_mixer(h, g, w_in, conv_w, w_out):
    b, s, d = h.shape
    tm = TM_PROJ
    return pl.pallas_call(
        functools.partial(_conv_mixer_kernel, tm=tm),
        grid=(b, s // tm),
        in_specs=[
            pl.BlockSpec((None, tm, d), lambda bi, i: (bi, i, 0)),
            _resident(g.shape), _resident(w_in.shape), _resident(conv_w.shape),
            _resident(w_out.shape),
        ],
        out_specs=pl.BlockSpec((None, tm, d), lambda bi, i: (bi, i, 0)),
        out_shape=jax.ShapeDtypeStruct(h.shape, F32),
        scratch_shapes=[pltpu.VMEM((tm + CONV_HALO, d), F32),
                        pltpu.VMEM((CONV_HALO, d), F32)],
        compiler_params=_params(("arbitrary", "arbitrary")),
        name="conv_mixer",
    )(h, g, w_in, conv_w, w_out)


def _ffn_kernel(h_ref, g_ref, wg_ref, wv_ref, cg_ref, cv_ref, wd_ref, o_ref,
                xn_ref, extg_ref, extv_ref, tailg_ref, tailv_ref, *, tm, n_chunks):
    i = pl.program_id(1)

    @pl.when(i == 0)
    def _():
        tailg_ref[...] = jnp.zeros_like(tailg_ref)
        tailv_ref[...] = jnp.zeros_like(tailv_ref)

    x = h_ref[...]
    xn_ref[...] = _rms_norm(x, g_ref[...]).astype(BF16)
    o_ref[...] = x

    def chunk(c, carry):
        xn = xn_ref[...]
        ug = jnp.dot(xn, wg_ref[c], preferred_element_type=F32)
        uv = jnp.dot(xn, wv_ref[c], preferred_element_type=F32)
        yg = _causal_conv3(extg_ref, tailg_ref[c], ug, cg_ref[c], tm)
        yv = _causal_conv3(extv_ref, tailv_ref[c], uv, cv_ref[c], tm)
        tailg_ref[c] = ug[tm - CONV_HALO:tm, :]
        tailv_ref[c] = uv[tm - CONV_HALO:tm, :]
        act = (yg / (1.0 + jnp.exp(-yg)) * yv).astype(BF16)
        o_ref[...] += jnp.dot(act, wd_ref[c], preferred_element_type=F32)
        return carry

    lax.fori_loop(0, n_chunks, chunk, 0)


def _ffn(h, g, wg, wv, cg, cv, wd):
    b, s, d = h.shape
    tm = TM_FFN
    n_chunks, _, fc = wg.shape
    return pl.pallas_call(
        functools.partial(_ffn_kernel, tm=tm, n_chunks=n_chunks),
        grid=(b, s // tm),
        in_specs=[
            pl.BlockSpec((None, tm, d), lambda bi, i: (bi, i, 0)),
            _resident(g.shape), _resident(wg.shape), _resident(wv.shape),
            _resident(cg.shape), _resident(cv.shape), _resident(wd.shape),
        ],
        out_specs=pl.BlockSpec((None, tm, d), lambda bi, i: (bi, i, 0)),
        out_shape=jax.ShapeDtypeStruct(h.shape, F32),
        scratch_shapes=[pltpu.VMEM((tm, d), BF16),
                        pltpu.VMEM((tm + CONV_HALO, fc), F32),
                        pltpu.VMEM((tm + CONV_HALO, fc), F32),
                        pltpu.VMEM((n_chunks, CONV_HALO, fc), F32),
                        pltpu.VMEM((n_chunks, CONV_HALO, fc), F32)],
        compiler_params=_params(("arbitrary", "arbitrary")),
        name="conv_ffn",
    )(h, g, wg, wv, cg, cv, wd)


def _ffn_weights(w_up, conv_w, w_down):
    fc = FF_CHUNK
    n_chunks = D_FF // fc
    d = w_up.shape[0]

    def up(w):
        return w.reshape(d, n_chunks, fc).transpose(1, 0, 2).astype(BF16)

    def cw(w):
        return w.reshape(w.shape[0], n_chunks, fc).transpose(1, 0, 2)

    return (up(w_up[:, :D_FF]), up(w_up[:, D_FF:]), cw(conv_w[:, :D_FF]), cw(conv_w[:, D_FF:]),
            w_down.reshape(n_chunks, fc, d).astype(BF16))


def kernel(x, attn_norm, attn_w_in, attn_f_bias, fox_q_gain, fox_k_gain, sb_q_gain, sb_k_gain,
           attn_w_out, conv_norm, conv_w_in, conv_kernel, conv_w_out, ffn_norm, ffn_w_up,
           ffn_conv, ffn_w_down):
    depth = ffn_norm.shape[0]
    fox_w = H_FOX * HEAD_DIM
    h = x
    for layer in range(depth):
        i = layer // 2
        if layer % 2 == 0:
            w_in = attn_w_in[i]
            w_qk = w_in[:, :2 * MIX_WIDTH].astype(BF16)
            wvt = w_in[:, 2 * MIX_WIDTH:3 * MIX_WIDTH].T.astype(BF16)
            wf = jnp.zeros((D_MODEL, LANES), BF16).at[:, :H_FOX].set(
                w_in[:, 3 * MIX_WIDTH:].astype(BF16))
            fb = jnp.zeros((1, LANES), F32).at[0, :H_FOX].set(attn_f_bias[i])
            qk_gain = jnp.concatenate(
                [jnp.tile(fox_q_gain[i], H_FOX), jnp.tile(sb_q_gain[i], H_SB),
                 jnp.tile(fox_k_gain[i], H_FOX), jnp.tile(sb_k_gain[i], H_SB)])[None, :]
            q, k, vt, cf = _attn_inproj(h, attn_norm[i][None, :], w_qk, wvt, wf, fb, qk_gain)
            o_fox = _fox_attention(q, k, vt, cf)
            o_sb = _sb_attention(q, k, vt)
            w_out = attn_w_out[i].astype(BF16)
            h = _attn_out(h, o_fox, o_sb, w_out[:fox_w], w_out[fox_w:])
        else:
            h = _conv_mixer(h, conv_norm[i][None, :], conv_w_in[i].astype(BF16),
                            conv_kernel[i], conv_w_out[i].astype(BF16))
        h = _ffn(h, ffn_norm[layer][None, :],
                 *_ffn_weights(ffn_w_up[layer], ffn_conv[layer], ffn_w_down[layer]))
    return h
```

```python
import functools

import jax
import jax.numpy as jnp
from jax import lax
from jax.experimental import pallas as pl
from jax.experimental.pallas import tpu as pltpu

F32 = jnp.float32
BF16 = jnp.bfloat16

D_MODEL = 1024
HEAD_DIM = 64
H_FOX = 8
H_SB = 8
MIX_WIDTH = (H_FOX + H_SB) * HEAD_DIM
D_FF = 2816
EPS = 1e-6
QK_SCALE = HEAD_DIM ** -0.5
LOG2E = 1.4426950408889634

LANES = 128
SUBLANES = 8
HEADS_PER_BLOCK = LANES // HEAD_DIM
BF16_ROWS = 2 * SUBLANES
MXU_DIM = 256
EXP2_CLAMP = 126.0
VMEM_LIMIT = 56 * 1024 * 1024

TM_PROJ = 256
TM_FFN = 512
FF_CHUNK = 256
T_ATTN = 512
CONV_HALO = SUBLANES


def _params(sem):
    return pltpu.CompilerParams(dimension_semantics=sem, vmem_limit_bytes=VMEM_LIMIT)


def _resident(shape):
    zeros = (0,) * len(shape)
    return pl.BlockSpec(shape, lambda *_: zeros, pipeline_mode=pl.Buffered(1))


def _rms_norm(x, g):
    ms = jnp.mean(x * x, axis=-1, keepdims=True)
    return x * lax.rsqrt(ms + EPS) * g


def _split3(x):
    hi = x.astype(BF16)
    r = x - hi.astype(F32)
    mid = r.astype(BF16)
    lo = (r - mid.astype(F32)).astype(BF16)
    return hi, mid, lo


def _dot_nt(a, b):
    return lax.dot_general(a, b, (((1,), (1,)), ((), ())), preferred_element_type=F32)


def _causal_conv3(ext_ref, tail, cur, w, tm):
    ext_ref[0:CONV_HALO, :] = tail
    ext_ref[CONV_HALO:CONV_HALO + tm, :] = cur
    return (w[2:3, :] * cur
            + w[1:2, :] * ext_ref[CONV_HALO - 1:CONV_HALO - 1 + tm, :]
            + w[0:1, :] * ext_ref[CONV_HALO - 2:CONV_HALO - 2 + tm, :])


def _attn_inproj_kernel(h_ref, g_ref, w_ref, wvt_ref, wf_ref, fb_ref, qkg_ref, gmat_ref,
                        tri_ref, q_ref, k_ref, vt_ref, cf_ref, carry_ref, *, tm):
    i = pl.program_id(1)

    @pl.when(i == 0)
    def _():
        carry_ref[...] = jnp.zeros_like(carry_ref)

    xn = _rms_norm(h_ref[...], g_ref[...]).astype(BF16)
    gmat = gmat_ref[...]

    n_blk = MIX_WIDTH // MXU_DIM
    proj = [jnp.dot(xn, w_ref[:, c * MXU_DIM:(c + 1) * MXU_DIM], preferred_element_type=F32)
            for c in range(2 * n_blk)]
    vt_ref[...] = _dot_nt(wvt_ref[...], xn).astype(BF16)
    fl = jnp.dot(xn, wf_ref[...], preferred_element_type=F32) + fb_ref[...]

    for c in range(2 * n_blk):
        out_ref, scale = (q_ref, QK_SCALE * LOG2E) if c < n_blk else (k_ref, 1.0)
        t = proj[c]
        ssq = jnp.dot((t * t).astype(BF16), gmat, preferred_element_type=F32)
        gain = qkg_ref[:, c * MXU_DIM:(c + 1) * MXU_DIM]
        tn = t * lax.rsqrt(ssq * (1.0 / HEAD_DIM) + EPS) * gain
        if scale != 1.0:
            tn = tn * scale
        lo = (c % n_blk) * MXU_DIM
        out_ref[:, lo:lo + MXU_DIM] = tn.astype(BF16)

    log_f = jnp.minimum(fl, 0.0) - jnp.log1p(jnp.exp(-jnp.abs(fl)))
    tri = tri_ref[...]
    hi, mid, lo = _split3(log_f)
    cs = (jnp.dot(tri, hi, preferred_element_type=F32)
          + jnp.dot(tri, mid, preferred_element_type=F32)
          + jnp.dot(tri, lo, preferred_element_type=F32))
    cf = cs + carry_ref[0:1, :]
    carry_ref[...] = jnp.broadcast_to(cf[tm - 1:tm, :], carry_ref.shape)
    cf_ref[...] = cf


def _attn_inproj(h, g, w_qk, wvt, wf, fb, qk_gain):
    b, s, d = h.shape
    tm = TM_PROJ
    r = lax.broadcasted_iota(jnp.int32, (MXU_DIM, MXU_DIM), 0) // HEAD_DIM
    c = lax.broadcasted_iota(jnp.int32, (MXU_DIM, MXU_DIM), 1) // HEAD_DIM
    gmat = (r == c).astype(BF16)
    rr = lax.broadcasted_iota(jnp.int32, (tm, tm), 0)
    cc = lax.broadcasted_iota(jnp.int32, (tm, tm), 1)
    tri = (cc <= rr).astype(BF16)
    row_spec = pl.BlockSpec((None, tm, MIX_WIDTH), lambda bi, i: (bi, i, 0))
    return pl.pallas_call(
        functools.partial(_attn_inproj_kernel, tm=tm),
        grid=(b, s // tm),
        in_specs=[
            pl.BlockSpec((None, tm, d), lambda bi, i: (bi, i, 0)),
            _resident(g.shape), _resident(w_qk.shape), _resident(wvt.shape),
            _resident(wf.shape), _resident(fb.shape), _resident(qk_gain.shape),
            _resident(gmat.shape), _resident(tri.shape),
        ],
        out_specs=[row_spec, row_spec,
                   pl.BlockSpec((None, MIX_WIDTH, tm), lambda bi, i: (bi, 0, i)),
                   pl.BlockSpec((None, tm, LANES), lambda bi, i: (bi, i, 0))],
        out_shape=[jax.ShapeDtypeStruct((b, s, MIX_WIDTH), BF16),
                   jax.ShapeDtypeStruct((b, s, MIX_WIDTH), BF16),
                   jax.ShapeDtypeStruct((b, MIX_WIDTH, s), BF16),
                   jax.ShapeDtypeStruct((b, s, LANES), F32)],
        scratch_shapes=[pltpu.VMEM((SUBLANES, LANES), F32)],
        compiler_params=_params(("arbitrary", "arbitrary")),
        name="attn_inproj",
    )(h, g, w_qk, wvt, wf, fb, qk_gain, gmat, tri)


def _head_masks(t):
    lane = lax.broadcasted_iota(jnp.int32, (t, LANES), 1)
    return [(lane >= HEAD_DIM * hh) & (lane < HEAD_DIM * (hh + 1))
            for hh in range(HEADS_PER_BLOCK)]


def _lane_pieces(t):
    return [slice(c * LANES, (c + 1) * LANES) for c in range(t // LANES)]


def _sweep_key_tiles(i, scores, consume):
    scores(i, 0)
    scores(jnp.maximum(i - 1, 0), 1)
    consume(i, 0, True)

    def pair(r, carry):
        j = i - 1 - 2 * r
        scores(jnp.maximum(j - 1, 0), 0)
        consume(j, 1, False)
        scores(jnp.maximum(j - 2, 0), 1)
        consume(j - 1, 0, False)
        return carry

    lax.fori_loop(0, i // 2, pair, 0)

    @pl.when(i % 2 == 1)
    def _():
        consume(0, 1, False)


def _write_heads(o_ref, outs_t):
    o_ref[...] = jnp.concatenate(outs_t, axis=0).T.astype(o_ref.dtype)


def _fox_kernel(q_ref, k_ref, vt_ref, cf_ref, o_ref, cfb_ref, st_ref, m_ref, acc_ref,
                *, t, fill_rows):
    p = pl.program_id(1)
    i = pl.program_id(2)

    @pl.when(i == 0)
    def _():
        sel_row = lax.broadcasted_iota(jnp.int32, (LANES, LANES), 0)
        for hh in range(HEADS_PER_BLOCK):
            sel = (sel_row == p * HEADS_PER_BLOCK + hh).astype(BF16)

            def fill(r, carry):
                off = pl.multiple_of(r * fill_rows, fill_rows)
                hi, mid, lo = _split3(cf_ref[pl.ds(off, fill_rows), :])
                cfb_ref[hh, pl.ds(off, fill_rows), :] = LOG2E * (
                    jnp.dot(hi, sel, preferred_element_type=F32)
                    + jnp.dot(mid, sel, preferred_element_type=F32)
                    + jnp.dot(lo, sel, preferred_element_type=F32))
                return carry

            lax.fori_loop(0, cf_ref.shape[0] // fill_rows, fill, 0)

    q = q_ref[...]
    qh = [jnp.where(mk, q, jnp.zeros_like(q)) for mk in _head_masks(t)]
    m_ref[...] = jnp.full_like(m_ref, -jnp.inf)
    acc_ref[...] = jnp.zeros_like(acc_ref)
    key = lax.broadcasted_iota(jnp.int32, (t, LANES), 0)
    qry = lax.broadcasted_iota(jnp.int32, (t, LANES), 1)
    pieces = _lane_pieces(t)
    ones_rows = jnp.ones((BF16_ROWS, t), BF16)

    def scores(j, slot):
        off = pl.multiple_of(j * t, t)
        kt = k_ref[pl.ds(off, t), :]
        for hh in range(HEADS_PER_BLOCK):
            st = _dot_nt(kt, qh[hh])
            fcol = cfb_ref[hh, pl.ds(off, t), :]
            for cols in pieces:
                st_ref[slot, hh, :, cols] = st[:, cols] - fcol

    def consume(j, slot, masked):
        off = pl.multiple_of(j * t, t)
        for hh in range(HEADS_PER_BLOCK):
            m_old = m_ref[hh]
            m_new, alpha, prob = [], [], []
            for c, cols in enumerate(pieces):
                x = st_ref[slot, hh, :, cols]
                if masked:
                    x = jnp.where(key <= qry + c * LANES, x, -jnp.inf)
                m_o = m_old[:, cols]
                m_n = jnp.maximum(m_o, jnp.max(x, axis=0, keepdims=True))
                m_new.append(m_n)
                alpha.append(jnp.exp2(m_o - m_n))
                prob.append(jnp.exp2(x - m_n).astype(BF16))
            vt_h = jnp.concatenate(
                [vt_ref[hh * HEAD_DIM:(hh + 1) * HEAD_DIM, pl.ds(off, t)], ones_rows], axis=0)
            pv = jnp.dot(vt_h, jnp.concatenate(prob, axis=1), preferred_element_type=F32)
            acc_ref[hh] = jnp.concatenate(alpha, axis=1) * acc_ref[hh] + pv
            m_ref[hh] = jnp.concatenate(m_new, axis=1)

    _sweep_key_tiles(i, scores, consume)
    _write_heads(o_ref, [acc_ref[hh, 0:HEAD_DIM, :] / acc_ref[hh, HEAD_DIM:HEAD_DIM + 1, :]
                         for hh in range(HEADS_PER_BLOCK)])


def _fox_attention(q, k, vt, cf):
    b, s, _ = q.shape
    t = T_ATTN
    nblk = H_FOX // HEADS_PER_BLOCK
    return pl.pallas_call(
        functools.partial(_fox_kernel, t=t, fill_rows=t),
        grid=(b, nblk, s // t),
        in_specs=[
            pl.BlockSpec((None, t, LANES), lambda bi, p, i: (bi, i, p)),
            pl.BlockSpec((None, s, LANES), lambda bi, p, i: (bi, 0, p)),
            pl.BlockSpec((None, LANES, s), lambda bi, p, i: (bi, p, 0)),
            pl.BlockSpec((None, s, LANES), lambda bi, p, i: (bi, 0, 0)),
        ],
        out_specs=pl.BlockSpec((None, t, LANES), lambda bi, p, i: (bi, i, p)),
        out_shape=jax.ShapeDtypeStruct((b, s, H_FOX * HEAD_DIM), BF16),
        scratch_shapes=[pltpu.VMEM((HEADS_PER_BLOCK, s, LANES), F32),
                        pltpu.VMEM((2, HEADS_PER_BLOCK, t, t), F32),
                        pltpu.VMEM((HEADS_PER_BLOCK, 1, t), F32),
                        pltpu.VMEM((HEADS_PER_BLOCK, HEAD_DIM + BF16_ROWS, t), F32)],
        compiler_params=_params(("arbitrary", "arbitrary", "arbitrary")),
        name="fox_attention",
    )(q, k, vt, cf)


def _sb_kernel(q_ref, k_ref, vt_ref, tri_ref, o_ref, z_ref, r_ref, acc_ref, *, t):
    i = pl.program_id(2)
    q = q_ref[...]
    qh = [jnp.where(mk, q, jnp.zeros_like(q)) for mk in _head_masks(t)]
    r_ref[...] = jnp.zeros_like(r_ref)
    acc_ref[...] = jnp.zeros_like(acc_ref)
    key = lax.broadcasted_iota(jnp.int32, (t, t), 0)
    qry = lax.broadcasted_iota(jnp.int32, (t, t), 1)
    strict = key < qry
    n_sub = t // MXU_DIM

    sub_rows = [slice(c * MXU_DIM, (c + 1) * MXU_DIM) for c in range(n_sub)]

    def scores(j, slot):
        off = pl.multiple_of(j * t, t)
        kt = k_ref[pl.ds(off, t), :]
        for hh in range(HEADS_PER_BLOCK):
            z_ref[slot, hh] = _dot_nt(kt, qh[hh])

    def consume(j, slot, masked):
        off = pl.multiple_of(j * t, t)
        tri = tri_ref[...]
        staged = []
        for hh in range(HEADS_PER_BLOCK):
            z = z_ref[slot, hh]
            a = jnp.maximum(
                LOG2E * jnp.log(1.0 + jnp.exp2(jnp.minimum(z, EXP2_CLAMP))), z)
            base = z - a
            if masked:
                a = jnp.where(strict, a, 0.0)
                base = jnp.where(strict, base, -jnp.inf)
            sums = [jnp.dot(tri, a[rows, :].astype(BF16), preferred_element_type=F32)
                    for rows in sub_rows]
            staged.append((base, sums))
        for hh in range(HEADS_PER_BLOCK):
            base, sums = staged[hh]
            r_run = r_ref[hh]
            w = [None] * n_sub
            for c in reversed(range(n_sub)):
                later = sums[c][0:MXU_DIM, :]
                w[c] = jnp.exp2(base[sub_rows[c], :] - later - r_run).astype(BF16)
                r_run = r_run + sums[c][MXU_DIM:MXU_DIM + 1, :]
            vt_h = vt_ref[hh * HEAD_DIM:(hh + 1) * HEAD_DIM, pl.ds(off, t)]
            acc_ref[hh] += jnp.dot(vt_h, jnp.concatenate(w, axis=0),
                                   preferred_element_type=F32)
            r_ref[hh] = r_run

    _sweep_key_tiles(i, scores, consume)
    _write_heads(o_ref, [acc_ref[hh] for hh in range(HEADS_PER_BLOCK)])


def _sb_attention(q, k, vt):
    b, s, _ = q.shape
    t = T_ATTN
    nblk = H_SB // HEADS_PER_BLOCK
    first = H_FOX // HEADS_PER_BLOCK
    rr = lax.broadcasted_iota(jnp.int32, (MXU_DIM + BF16_ROWS, MXU_DIM), 0)
    cc = lax.broadcasted_iota(jnp.int32, (MXU_DIM + BF16_ROWS, MXU_DIM), 1)
    tri = ((cc > rr) | (rr >= MXU_DIM)).astype(BF16)
    return pl.pallas_call(
        functools.partial(_sb_kernel, t=t),
        grid=(b, nblk, s // t),
        in_specs=[
            pl.BlockSpec((None, t, LANES), lambda bi, p, i: (bi, i, p + first)),
            pl.BlockSpec((None, s, LANES), lambda bi, p, i: (bi, 0, p + first)),
            pl.BlockSpec((None, LANES, s), lambda bi, p, i: (bi, p + first, 0)),
            _resident(tri.shape),
        ],
        out_specs=pl.BlockSpec((None, t, LANES), lambda bi, p, i: (bi, i, p)),
        out_shape=jax.ShapeDtypeStruct((b, s, H_SB * HEAD_DIM), BF16),
        scratch_shapes=[pltpu.VMEM((2, HEADS_PER_BLOCK, t, t), F32),
                        pltpu.VMEM((HEADS_PER_BLOCK, 1, t), F32),
                        pltpu.VMEM((HEADS_PER_BLOCK, HEAD_DIM, t), F32)],
        compiler_params=_params(("arbitrary", "arbitrary", "arbitrary")),
        name="sb_attention",
    )(q, k, vt, tri)


def _attn_out_kernel(h_ref, of_ref, os_ref, wf_ref, ws_ref, o_ref):
    o_ref[...] = (h_ref[...]
                  + jnp.dot(of_ref[...], wf_ref[...], preferred_element_type=F32)
                  + jnp.dot(os_ref[...], ws_ref[...], preferred_element_type=F32))


def _attn_out(h, o_fox, o_sb, w_fox, w_sb):
    b, s, d = h.shape
    tm = TM_PROJ
    return pl.pallas_call(
        _attn_out_kernel,
        grid=(b, s // tm),
        in_specs=[
            pl.BlockSpec((None, tm, d), lambda bi, i: (bi, i, 0)),
            pl.BlockSpec((None, tm, o_fox.shape[-1]), lambda bi, i: (bi, i, 0)),
            pl.BlockSpec((None, tm, o_sb.shape[-1]), lambda bi, i: (bi, i, 0)),
            _resident(w_fox.shape), _resident(w_sb.shape),
        ],
        out_specs=pl.BlockSpec((None, tm, d), lambda bi, i: (bi, i, 0)),
        out_shape=jax.ShapeDtypeStruct(h.shape, F32),
        compiler_params=_params(("arbitrary", "arbitrary")),
        name="attn_out",
    )(h, o_fox, o_sb, w_fox, w_sb)


def _conv_mixer_kernel(h_ref, g_ref, win_ref, cw_ref, wout_ref, o_ref, ext_ref, tail_ref,
                       *, tm):
    i = pl.program_id(1)

    @pl.when(i == 0)
    def _():
        tail_ref[...] = jnp.zeros_like(tail_ref)

    x = h_ref[...]
    xn = _rms_norm(x, g_ref[...]).astype(BF16)
    d = x.shape[-1]
    gate_c = jnp.dot(xn, win_ref[:, d:2 * d], preferred_element_type=F32)
    u = jnp.dot(xn, win_ref[:, 2 * d:3 * d], preferred_element_type=F32)
    cu = gate_c * u
    y = _causal_conv3(ext_ref, tail_ref[...], cu, cw_ref[...], tm)
    tail_ref[...] = cu[tm - CONV_HALO:tm, :]
    gate_b = jnp.dot(xn, win_ref[:, 0:d], preferred_element_type=F32)
    y = (gate_b * y).astype(BF16)
    o_ref[...] = x + jnp.dot(y, wout_ref[...], preferred_element_type=F32)


def _conv_mixer(h, g, w_in, conv_w, w_out):
    b, s, d = h.shape
    tm = TM_PROJ
    return pl.pallas_call(
        functools.partial(_conv_mixer_kernel, tm=tm),
        grid=(b, s // tm),
        in_specs=[
            pl.BlockSpec((None, tm, d), lambda bi, i: (bi, i, 0)),
            _resident(g.shape), _resident(w_in.shape), _resident(conv_w.shape),
            _resident(w_out.shape),
        ],
        out_specs=pl.BlockSpec((None, tm, d), lambda bi, i: (bi, i, 0)),
        out_shape=jax.ShapeDtypeStruct(h.shape, F32),
        scratch_shapes=[pltpu.VMEM((tm + CONV_HALO, d), F32),
                        pltpu.VMEM((CONV_HALO, d), F32)],
        compiler_params=_params(("arbitrary", "arbitrary")),
        name="conv_mixer",
    )(h, g, w_in, conv_w, w_out)


def _ffn_kernel(h_ref, g_ref, wup_ref, cw_ref, wd_ref, o_ref,
                xn_ref, ext_ref, tail_ref, act_ref, *, tm, fc):
    i = pl.program_id(1)
    n_chunks = D_FF // fc

    @pl.when(i == 0)
    def _():
        tail_ref[...] = jnp.zeros_like(tail_ref)

    x = h_ref[...]
    xn_ref[...] = _rms_norm(x, g_ref[...]).astype(BF16)

    def up_project(c):
        slot = c % 2
        for part in range(2):
            cols = slice(part * D_FF + c * fc, part * D_FF + (c + 1) * fc)
            ext_ref[slot, part, 0:CONV_HALO, :] = tail_ref[part, :, c * fc:(c + 1) * fc]
            ext_ref[slot, part, CONV_HALO:CONV_HALO + tm, :] = jnp.dot(
                xn_ref[...], wup_ref[:, cols], preferred_element_type=F32)

    def conv_gate(c):
        slot = c % 2
        y = []
        for part in range(2):
            cols = slice(part * D_FF + c * fc, part * D_FF + (c + 1) * fc)
            w = cw_ref[:, cols]
            taps = [w[k:k + 1, :] * ext_ref[slot, part, CONV_HALO - 2 + k:CONV_HALO - 2 + k + tm, :]
                    for k in range(3)]
            y.append(taps[2] + taps[1] + taps[0])
            tail_ref[part, :, c * fc:(c + 1) * fc] = ext_ref[slot, part, tm:tm + CONV_HALO, :]
        act_ref[:, c * fc:(c + 1) * fc] = (y[0] / (1.0 + jnp.exp(-y[0])) * y[1]).astype(BF16)

    up_project(0)
    for c in range(n_chunks):
        if c + 1 < n_chunks:
            up_project(c + 1)
        conv_gate(c)
    o_ref[...] = x + jnp.dot(act_ref[...], wd_ref[...], preferred_element_type=F32)


def _ffn(h, g, w_up, conv_w, w_down):
    b, s, d = h.shape
    tm = TM_FFN
    fc = FF_CHUNK
    return pl.pallas_call(
        functools.partial(_ffn_kernel, tm=tm, fc=fc),
        grid=(b, s // tm),
        in_specs=[
            pl.BlockSpec((None, tm, d), lambda bi, i: (bi, i, 0)),
            _resident(g.shape), _resident(w_up.shape), _resident(conv_w.shape),
            _resident(w_down.shape),
        ],
        out_specs=pl.BlockSpec((None, tm, d), lambda bi, i: (bi, i, 0)),
        out_shape=jax.ShapeDtypeStruct(h.shape, F32),
        scratch_shapes=[pltpu.VMEM((tm, d), BF16),
                        pltpu.VMEM((2, 2, tm + CONV_HALO, fc), F32),
                        pltpu.VMEM((2, CONV_HALO, D_FF), F32),
                        pltpu.VMEM((tm, D_FF), BF16)],
        compiler_params=_params(("arbitrary", "arbitrary")),
        name="conv_ffn",
    )(h, g, w_up, conv_w, w_down)


def kernel(x, attn_norm, attn_w_in, attn_f_bias, fox_q_gain, fox_k_gain, sb_q_gain, sb_k_gain,
           attn_w_out, conv_norm, conv_w_in, conv_kernel, conv_w_out, ffn_norm, ffn_w_up,
           ffn_conv, ffn_w_down):
    depth = ffn_norm.shape[0]
    fox_w = H_FOX * HEAD_DIM
    h = x
    for layer in range(depth):
        i = layer // 2
        if layer % 2 == 0:
            w_in = attn_w_in[i]
            w_qk = w_in[:, :2 * MIX_WIDTH].astype(BF16)
            wvt = w_in[:, 2 * MIX_WIDTH:3 * MIX_WIDTH].T.astype(BF16)
            wf = jnp.zeros((D_MODEL, LANES), BF16).at[:, :H_FOX].set(
                w_in[:, 3 * MIX_WIDTH:].astype(BF16))
            fb = jnp.zeros((1, LANES), F32).at[0, :H_FOX].set(attn_f_bias[i])
            qk_gain = jnp.concatenate(
                [jnp.tile(fox_q_gain[i], H_FOX), jnp.tile(sb_q_gain[i], H_SB),
                 jnp.tile(fox_k_gain[i], H_FOX), jnp.tile(sb_k_gain[i], H_SB)])[None, :]
            q, k, vt, cf = _attn_inproj(h, attn_norm[i][None, :], w_qk, wvt, wf, fb, qk_gain)
            o_fox = _fox_attention(q, k, vt, cf)
            o_sb = _sb_attention(q, k, vt)
            w_out = attn_w_out[i].astype(BF16)
            h = _attn_out(h, o_fox, o_sb, w_out[:fox_w], w_out[fox_w:])
        else:
            h = _conv_mixer(h, conv_norm[i][None, :], conv_w_in[i].astype(BF16),
                            conv_kernel[i], conv_w_out[i].astype(BF16))
        h = _ffn(h, ffn_norm[layer][None, :], ffn_w_up[layer].astype(BF16), ffn_conv[layer],
                 ffn_w_down[layer].astype(BF16))
    return h
```

```python
import functools

import jax
import jax.numpy as jnp
from jax import lax
from jax.experimental import pallas as pl
from jax.experimental.pallas import tpu as pltpu

F32 = jnp.float32
BF16 = jnp.bfloat16

D_MODEL = 1024
HEAD_DIM = 64
H_FOX = 8
H_SB = 8
MIX_WIDTH = (H_FOX + H_SB) * HEAD_DIM
D_FF = 2816
EPS = 1e-6
QK_SCALE = HEAD_DIM ** -0.5
LOG2E = 1.4426950408889634

LANES = 128
SUBLANES = 8
HEADS_PER_BLOCK = LANES // HEAD_DIM
BF16_ROWS = 2 * SUBLANES
MXU_DIM = 256
EXP2_CLAMP = 126.0
VMEM_LIMIT = 56 * 1024 * 1024

TM_PROJ = 256
TM_FFN = 512
FF_CHUNK = 256
T_ATTN = 512
CONV_HALO = SUBLANES


def _params(sem):
    return pltpu.CompilerParams(dimension_semantics=sem, vmem_limit_bytes=VMEM_LIMIT)


def _resident(shape):
    zeros = (0,) * len(shape)
    return pl.BlockSpec(shape, lambda *_: zeros, pipeline_mode=pl.Buffered(1))


def _rms_norm(x, g):
    ms = jnp.mean(x * x, axis=-1, keepdims=True)
    return x * lax.rsqrt(ms + EPS) * g


def _split3(x):
    hi = x.astype(BF16)
    r = x - hi.astype(F32)
    mid = r.astype(BF16)
    lo = (r - mid.astype(F32)).astype(BF16)
    return hi, mid, lo


def _dot_nt(a, b):
    return lax.dot_general(a, b, (((1,), (1,)), ((), ())), preferred_element_type=F32)


def _causal_conv3(ext_ref, tail, cur, w, tm):
    ext_ref[0:CONV_HALO, :] = tail
    ext_ref[CONV_HALO:CONV_HALO + tm, :] = cur
    return (w[2:3, :] * cur
            + w[1:2, :] * ext_ref[CONV_HALO - 1:CONV_HALO - 1 + tm, :]
            + w[0:1, :] * ext_ref[CONV_HALO - 2:CONV_HALO - 2 + tm, :])


def _attn_inproj_kernel(h_ref, g_ref, w_ref, wvt_ref, wf_ref, fb_ref, qkg_ref, gmat_ref,
                        tri_ref, q_ref, k_ref, vt_ref, cf_ref, carry_ref, *, tm):
    i = pl.program_id(1)

    @pl.when(i == 0)
    def _():
        carry_ref[...] = jnp.zeros_like(carry_ref)

    xn = _rms_norm(h_ref[...], g_ref[...]).astype(BF16)
    gmat = gmat_ref[...]

    n_blk = MIX_WIDTH // MXU_DIM
    proj = [jnp.dot(xn, w_ref[:, c * MXU_DIM:(c + 1) * MXU_DIM], preferred_element_type=F32)
            for c in range(2 * n_blk)]
    vt_ref[...] = _dot_nt(wvt_ref[...], xn).astype(BF16)
    fl = jnp.dot(xn, wf_ref[...], preferred_element_type=F32) + fb_ref[...]

    for c in range(2 * n_blk):
        out_ref, scale = (q_ref, QK_SCALE * LOG2E) if c < n_blk else (k_ref, 1.0)
        t = proj[c]
        ssq = jnp.dot((t * t).astype(BF16), gmat, preferred_element_type=F32)
        gain = qkg_ref[:, c * MXU_DIM:(c + 1) * MXU_DIM]
        tn = t * lax.rsqrt(ssq * (1.0 / HEAD_DIM) + EPS) * gain
        if scale != 1.0:
            tn = tn * scale
        lo = (c % n_blk) * MXU_DIM
        out_ref[:, lo:lo + MXU_DIM] = tn.astype(BF16)

    log_f = jnp.minimum(fl, 0.0) - jnp.log1p(jnp.exp(-jnp.abs(fl)))
    tri = tri_ref[...]
    hi, mid, lo = _split3(log_f)
    cs = (jnp.dot(tri, hi, preferred_element_type=F32)
          + jnp.dot(tri, mid, preferred_element_type=F32)
          + jnp.dot(tri, lo, preferred_element_type=F32))
    cf = cs + carry_ref[0:1, :]
    carry_ref[...] = jnp.broadcast_to(cf[tm - 1:tm, :], carry_ref.shape)
    cf_ref[...] = cf


def _attn_inproj(h, g, w_qk, wvt, wf, fb, qk_gain):
    b, s, d = h.shape
    tm = TM_PROJ
    r = lax.broadcasted_iota(jnp.int32, (MXU_DIM, MXU_DIM), 0) // HEAD_DIM
    c = lax.broadcasted_iota(jnp.int32, (MXU_DIM, MXU_DIM), 1) // HEAD_DIM
    gmat = (r == c).astype(BF16)
    rr = lax.broadcasted_iota(jnp.int32, (tm, tm), 0)
    cc = lax.broadcasted_iota(jnp.int32, (tm, tm), 1)
    tri = (cc <= rr).astype(BF16)
    row_spec = pl.BlockSpec((None, tm, MIX_WIDTH), lambda bi, i: (bi, i, 0))
    return pl.pallas_call(
        functools.partial(_attn_inproj_kernel, tm=tm),
        grid=(b, s // tm),
        in_specs=[
            pl.BlockSpec((None, tm, d), lambda bi, i: (bi, i, 0)),
            _resident(g.shape), _resident(w_qk.shape), _resident(wvt.shape),
            _resident(wf.shape), _resident(fb.shape), _resident(qk_gain.shape),
            _resident(gmat.shape), _resident(tri.shape),
        ],
        out_specs=[row_spec, row_spec,
                   pl.BlockSpec((None, MIX_WIDTH, tm), lambda bi, i: (bi, 0, i)),
                   pl.BlockSpec((None, tm, LANES), lambda bi, i: (bi, i, 0))],
        out_shape=[jax.ShapeDtypeStruct((b, s, MIX_WIDTH), BF16),
                   jax.ShapeDtypeStruct((b, s, MIX_WIDTH), BF16),
                   jax.ShapeDtypeStruct((b, MIX_WIDTH, s), BF16),
                   jax.ShapeDtypeStruct((b, s, LANES), F32)],
        scratch_shapes=[pltpu.VMEM((SUBLANES, LANES), F32)],
        compiler_params=_params(("arbitrary", "arbitrary")),
        name="attn_inproj",
    )(h, g, w_qk, wvt, wf, fb, qk_gain, gmat, tri)


def _head_masks(t):
    lane = lax.broadcasted_iota(jnp.int32, (t, LANES), 1)
    return [(lane >= HEAD_DIM * hh) & (lane < HEAD_DIM * (hh + 1))
            for hh in range(HEADS_PER_BLOCK)]


def _lane_pieces(t):
    return [slice(c * LANES, (c + 1) * LANES) for c in range(t // LANES)]


def _sweep_key_tiles(i, scores, step):
    scores(i, 0)
    step(i, 0, True, (jnp.maximum(i - 1, 0), 1))

    def pair(r, carry):
        j = i - 1 - 2 * r
        step(j, 1, False, (jnp.maximum(j - 1, 0), 0))
        step(j - 1, 0, False, (jnp.maximum(j - 2, 0), 1))
        return carry

    lax.fori_loop(0, i // 2, pair, 0)

    @pl.when(i % 2 == 1)
    def _():
        step(0, 1, False, None)


def _write_heads(o_ref, outs_t):
    o_ref[...] = jnp.concatenate(outs_t, axis=0).T.astype(o_ref.dtype)


def _fox_parts(q_ref, k_ref, vt_ref, cfb_ref, st_ref, m_ref, acc_ref, t):
    q = q_ref[...]
    qh = [jnp.where(mk, q, jnp.zeros_like(q)) for mk in _head_masks(t)]
    m_ref[...] = jnp.full_like(m_ref, -jnp.inf)
    acc_ref[...] = jnp.zeros_like(acc_ref)
    key = lax.broadcasted_iota(jnp.int32, (t, LANES), 0)
    qry = lax.broadcasted_iota(jnp.int32, (t, LANES), 1)
    pieces = _lane_pieces(t)
    ones_rows = jnp.ones((BF16_ROWS, t), BF16)

    def scores(j, slot, hh):
        off = pl.multiple_of(j * t, t)
        st = _dot_nt(k_ref[pl.ds(off, t), :], qh[hh])
        fcol = cfb_ref[hh, pl.ds(off, t), :]
        for cols in pieces:
            st_ref[slot, hh, :, cols] = st[:, cols] - fcol

    def consume(j, slot, masked, hh):
        off = pl.multiple_of(j * t, t)
        m_old = m_ref[hh]
        m_new, alpha, prob = [], [], []
        for c, cols in enumerate(pieces):
            x = st_ref[slot, hh, :, cols]
            if masked:
                x = jnp.where(key <= qry + c * LANES, x, -jnp.inf)
            m_o = m_old[:, cols]
            m_n = jnp.maximum(m_o, jnp.max(x, axis=0, keepdims=True))
            m_new.append(m_n)
            alpha.append(jnp.exp2(m_o - m_n))
            prob.append(jnp.exp2(x - m_n).astype(BF16))
        vt_h = jnp.concatenate(
            [vt_ref[hh * HEAD_DIM:(hh + 1) * HEAD_DIM, pl.ds(off, t)], ones_rows], axis=0)
        pv = jnp.dot(vt_h, jnp.concatenate(prob, axis=1), preferred_element_type=F32)
        acc_ref[hh] = jnp.concatenate(alpha, axis=1) * acc_ref[hh] + pv
        m_ref[hh] = jnp.concatenate(m_new, axis=1)

    def result():
        return [acc_ref[hh, 0:HEAD_DIM, :] / acc_ref[hh, HEAD_DIM:HEAD_DIM + 1, :]
                for hh in range(HEADS_PER_BLOCK)]

    return scores, consume, result


def _sb_parts(q_ref, k_ref, vt_ref, tri_ref, z_ref, r_ref, acc_ref, t):
    q = q_ref[...]
    qh = [jnp.where(mk, q, jnp.zeros_like(q)) for mk in _head_masks(t)]
    r_ref[...] = jnp.zeros_like(r_ref)
    acc_ref[...] = jnp.zeros_like(acc_ref)
    key = lax.broadcasted_iota(jnp.int32, (t, t), 0)
    qry = lax.broadcasted_iota(jnp.int32, (t, t), 1)
    strict = key < qry
    n_sub = t // MXU_DIM

    sub_rows = [slice(c * MXU_DIM, (c + 1) * MXU_DIM) for c in range(n_sub)]

    def scores(j, slot, hh):
        off = pl.multiple_of(j * t, t)
        z_ref[slot, hh] = _dot_nt(k_ref[pl.ds(off, t), :], qh[hh])

    def gate_sums(slot, masked, hh):
        z = z_ref[slot, hh]
        a = jnp.maximum(LOG2E * jnp.log(1.0 + jnp.exp2(jnp.minimum(z, EXP2_CLAMP))), z)
        base = z - a
        if masked:
            a = jnp.where(strict, a, 0.0)
            base = jnp.where(strict, base, -jnp.inf)
        tri = tri_ref[...]
        sums = [jnp.dot(tri, a[rows, :].astype(BF16), preferred_element_type=F32)
                for rows in sub_rows]
        return base, sums

    def accumulate(j, staged, hh):
        off = pl.multiple_of(j * t, t)
        base, sums = staged
        r_run = r_ref[hh]
        w = [None] * n_sub
        for c in reversed(range(n_sub)):
            later = sums[c][0:MXU_DIM, :]
            w[c] = jnp.exp2(base[sub_rows[c], :] - later - r_run).astype(BF16)
            r_run = r_run + sums[c][MXU_DIM:MXU_DIM + 1, :]
        vt_h = vt_ref[hh * HEAD_DIM:(hh + 1) * HEAD_DIM, pl.ds(off, t)]
        acc_ref[hh] += jnp.dot(vt_h, jnp.concatenate(w, axis=0), preferred_element_type=F32)
        r_ref[hh] = r_run

    def result():
        return [acc_ref[hh] for hh in range(HEADS_PER_BLOCK)]

    return scores, gate_sums, accumulate, result


def _attention_kernel(qf_ref, kf_ref, vtf_ref, cf_ref, qs_ref, ks_ref, vts_ref, tri_ref,
                      of_ref, os_ref, cfb_ref, st_ref, m_ref, facc_ref, z_ref, r_ref,
                      sacc_ref, *, t):
    p = pl.program_id(1)
    i = pl.program_id(2)

    @pl.when(i == 0)
    def _():
        sel_row = lax.broadcasted_iota(jnp.int32, (LANES, LANES), 0)
        for hh in range(HEADS_PER_BLOCK):
            sel = (sel_row == p * HEADS_PER_BLOCK + hh).astype(BF16)

            def fill(r, carry):
                off = pl.multiple_of(r * t, t)
                hi, mid, lo = _split3(cf_ref[pl.ds(off, t), :])
                cfb_ref[hh, pl.ds(off, t), :] = LOG2E * (
                    jnp.dot(hi, sel, preferred_element_type=F32)
                    + jnp.dot(mid, sel, preferred_element_type=F32)
                    + jnp.dot(lo, sel, preferred_element_type=F32))
                return carry

            lax.fori_loop(0, cf_ref.shape[0] // t, fill, 0)

    fox_scores, fox_consume, fox_result = _fox_parts(
        qf_ref, kf_ref, vtf_ref, cfb_ref, st_ref, m_ref, facc_ref, t)
    sb_scores, sb_gate_sums, sb_accumulate, sb_result = _sb_parts(
        qs_ref, ks_ref, vts_ref, tri_ref, z_ref, r_ref, sacc_ref, t)

    heads = range(HEADS_PER_BLOCK)

    def scores(j, slot):
        for hh in heads:
            sb_scores(j, slot, hh)
            fox_scores(j, slot, hh)

    def step(j, slot, masked, nxt):
        staged = []
        for hh in heads:
            if nxt is not None:
                sb_scores(*nxt, hh)
            staged.append(sb_gate_sums(slot, masked, hh))
        for hh in heads:
            if nxt is not None:
                fox_scores(*nxt, hh)
            fox_consume(j, slot, masked, hh)
        for hh in heads:
            sb_accumulate(j, staged[hh], hh)

    _sweep_key_tiles(i, scores, step)
    _write_heads(of_ref, fox_result())
    _write_heads(os_ref, sb_result())


def _attention(q, k, vt, cf):
    b, s, _ = q.shape
    t = T_ATTN
    nblk = H_FOX // HEADS_PER_BLOCK
    rr = lax.broadcasted_iota(jnp.int32, (MXU_DIM + BF16_ROWS, MXU_DIM), 0)
    cc = lax.broadcasted_iota(jnp.int32, (MXU_DIM + BF16_ROWS, MXU_DIM), 1)
    tri = ((cc > rr) | (rr >= MXU_DIM)).astype(BF16)

    def q_spec(first):
        return pl.BlockSpec((None, t, LANES), lambda bi, p, i: (bi, i, p + first))

    def k_spec(first):
        return pl.BlockSpec((None, s, LANES), lambda bi, p, i: (bi, 0, p + first))

    def vt_spec(first):
        return pl.BlockSpec((None, LANES, s), lambda bi, p, i: (bi, p + first, 0))

    out_spec = pl.BlockSpec((None, t, LANES), lambda bi, p, i: (bi, i, p))
    return pl.pallas_call(
        functools.partial(_attention_kernel, t=t),
        grid=(b, nblk, s // t),
        in_specs=[
            q_spec(0), k_spec(0), vt_spec(0),
            pl.BlockSpec((None, s, LANES), lambda bi, p, i: (bi, 0, 0),
                         pipeline_mode=pl.Buffered(1)),
            q_spec(nblk), k_spec(nblk), vt_spec(nblk), _resident(tri.shape),
        ],
        out_specs=[out_spec, out_spec],
        out_shape=[jax.ShapeDtypeStruct((b, s, H_FOX * HEAD_DIM), BF16),
                   jax.ShapeDtypeStruct((b, s, H_SB * HEAD_DIM), BF16)],
        scratch_shapes=[pltpu.VMEM((HEADS_PER_BLOCK, s, LANES), F32),
                        pltpu.VMEM((2, HEADS_PER_BLOCK, t, t), F32),
                        pltpu.VMEM((HEADS_PER_BLOCK, 1, t), F32),
                        pltpu.VMEM((HEADS_PER_BLOCK, HEAD_DIM + BF16_ROWS, t), F32),
                        pltpu.VMEM((2, HEADS_PER_BLOCK, t, t), F32),
                        pltpu.VMEM((HEADS_PER_BLOCK, 1, t), F32),
                        pltpu.VMEM((HEADS_PER_BLOCK, HEAD_DIM, t), F32)],
        compiler_params=_params(("arbitrary", "arbitrary", "arbitrary")),
        name="attention",
    )(q, k, vt, cf, q, k, vt, tri)


def _attn_out_kernel(h_ref, of_ref, os_ref, wf_ref, ws_ref, o_ref):
    o_ref[...] = (h_ref[...]
                  + jnp.dot(of_ref[...], wf_ref[...], preferred_element_type=F32)
                  + jnp.dot(os_ref[...], ws_ref[...], preferred_element_type=F32))


def _attn_out(h, o_fox, o_sb, w_fox, w_sb):
    b, s, d = h.shape
    tm = TM_PROJ
    return pl.pallas_call(
        _attn_out_kernel,
        grid=(b, s // tm),
        in_specs=[
            pl.BlockSpec((None, tm, d), lambda bi, i: (bi, i, 0)),
            pl.BlockSpec((None, tm, o_fox.shape[-1]), lambda bi, i: (bi, i, 0)),
            pl.BlockSpec((None, tm, o_sb.shape[-1]), lambda bi, i: (bi, i, 0)),
            _resident(w_fox.shape), _resident(w_sb.shape),
        ],
        out_specs=pl.BlockSpec((None, tm, d), lambda bi, i: (bi, i, 0)),
        out_shape=jax.ShapeDtypeStruct(h.shape, F32),
        compiler_params=_params(("arbitrary", "arbitrary")),
        name="attn_out",
    )(h, o_fox, o_sb, w_fox, w_sb)


def _conv_mixer_kernel(h_ref, g_ref, win_ref, cw_ref, wout_ref, o_ref, ext_ref, tail_ref,
                       *, tm):
    i = pl.program_id(1)

    @pl.when(i == 0)
    def _():
        tail_ref[...] = jnp.zeros_like(tail_ref)

    x = h_ref[...]
    xn = _rms_norm(x, g_ref[...]).astype(BF16)
    d = x.shape[-1]
    gate_c = jnp.dot(xn, win_ref[:, d:2 * d], preferred_element_type=F32)
    u = jnp.dot(xn, win_ref[:, 2 * d:3 * d], preferred_element_type=F32)
    cu = gate_c * u
    y = _causal_conv3(ext_ref, tail_ref[...], cu, cw_ref[...], tm)
    tail_ref[...] = cu[tm - CONV_HALO:tm, :]
    gate_b = jnp.dot(xn, win_ref[:, 0:d], preferred_element_type=F32)
    y = (gate_b * y).astype(BF16)
    o_ref[...] = x + jnp.dot(y, wout_ref[...], preferred_element_type=F32)


def _conv_mixer(h, g, w_in, conv_w, w_out):
    b, s, d = h.shape
    tm = TM_PROJ
    return pl.pallas_call(
        functools.partial(_conv_mixer_kernel, tm=tm),
        grid=(b, s // tm),
        in_specs=[
            pl.BlockSpec((None, tm, d), lambda bi, i: (bi, i, 0)),
            _resident(g.shape), _resident(w_in.shape), _resident(conv_w.shape),
            _resident(w_out.shape),
        ],
        out_specs=pl.BlockSpec((None, tm, d), lambda bi, i: (bi, i, 0)),
        out_shape=jax.ShapeDtypeStruct(h.shape, F32),
        scratch_shapes=[pltpu.VMEM((tm + CONV_HALO, d), F32),
                        pltpu.VMEM((CONV_HALO, d), F32)],
        compiler_params=_params(("arbitrary", "arbitrary")),
        name="conv_mixer",
    )(h, g, w_in, conv_w, w_out)


def _ffn_kernel(h_ref, g_ref, wup_ref, cw_ref, wd_ref, o_ref,
                xn_ref, ext_ref, tail_ref, act_ref, *, tm, fc):
    i = pl.program_id(1)
    n_chunks = D_FF // fc

    @pl.when(i == 0)
    def _():
        tail_ref[...] = jnp.zeros_like(tail_ref)

    x = h_ref[...]
    xn_ref[...] = _rms_norm(x, g_ref[...]).astype(BF16)

    def up_project(c):
        slot = c % 2
        for part in range(2):
            cols = slice(part * D_FF + c * fc, part * D_FF + (c + 1) * fc)
            ext_ref[slot, part, 0:CONV_HALO, :] = tail_ref[part, :, c * fc:(c + 1) * fc]
            ext_ref[slot, part, CONV_HALO:CONV_HALO + tm, :] = jnp.dot(
                xn_ref[...], wup_ref[:, cols], preferred_element_type=F32)

    def conv_gate(c):
        slot = c % 2
        y = []
        for part in range(2):
            cols = slice(part * D_FF + c * fc, part * D_FF + (c + 1) * fc)
            w = cw_ref[:, cols]
            taps = [w[k:k + 1, :] * ext_ref[slot, part, CONV_HALO - 2 + k:CONV_HALO - 2 + k + tm, :]
                    for k in range(3)]
            y.append(taps[2] + taps[1] + taps[0])
            tail_ref[part, :, c * fc:(c + 1) * fc] = ext_ref[slot, part, tm:tm + CONV_HALO, :]
        act_ref[:, c * fc:(c + 1) * fc] = (y[0] / (1.0 + jnp.exp(-y[0])) * y[1]).astype(BF16)

    up_project(0)
    for c in range(n_chunks):
        if c + 1 < n_chunks:
            up_project(c + 1)
        conv_gate(c)
    o_ref[...] = x + jnp.dot(act_ref[...], wd_ref[...], preferred_element_type=F32)


def _ffn(h, g, w_up, conv_w, w_down):
    b, s, d = h.shape
    tm = TM_FFN
    fc = FF_CHUNK
    return pl.pallas_call(
        functools.partial(_ffn_kernel, tm=tm, fc=fc),
        grid=(b, s // tm),
        in_specs=[
            pl.BlockSpec((None, tm, d), lambda bi, i: (bi, i, 0)),
            _resident(g.shape), _resident(w_up.shape), _resident(conv_w.shape),
            _resident(w_down.shape),
        ],
        out_specs=pl.BlockSpec((None, tm, d), lambda bi, i: (bi, i, 0)),
        out_shape=jax.ShapeDtypeStruct(h.shape, F32),
        scratch_shapes=[pltpu.VMEM((tm, d), BF16),
                        pltpu.VMEM((2, 2, tm + CONV_HALO, fc), F32),
                        pltpu.VMEM((2, CONV_HALO, D_FF), F32),
                        pltpu.VMEM((tm, D_FF), BF16)],
        compiler_params=_params(("arbitrary", "arbitrary")),
        name="conv_ffn",
    )(h, g, w_up, conv_w, w_down)


def kernel(x, attn_norm, attn_w_in, attn_f_bias, fox_q_gain, fox_k_gain, sb_q_gain, sb_k_gain,
           attn_w_out, conv_norm, conv_w_in, conv_kernel, conv_w_out, ffn_norm, ffn_w_up,
           ffn_conv, ffn_w_down):
    depth = ffn_norm.shape[0]
    fox_w = H_FOX * HEAD_DIM
    h = x
    for layer in range(depth):
        i = layer // 2
        if layer % 2 == 0:
            w_in = attn_w_in[i]
            w_qk = w_in[:, :2 * MIX_WIDTH].astype(BF16)
            wvt = w_in[:, 2 * MIX_WIDTH:3 * MIX_WIDTH].T.astype(BF16)
            wf = jnp.zeros((D_MODEL, LANES), BF16).at[:, :H_FOX].set(
                w_in[:, 3 * MIX_WIDTH:].astype(BF16))
            fb = jnp.zeros((1, LANES), F32).at[0, :H_FOX].set(attn_f_bias[i])
            qk_gain = jnp.concatenate(
                [jnp.tile(fox_q_gain[i], H_FOX), jnp.tile(sb_q_gain[i], H_SB),
                 jnp.tile(fox_k_gain[i], H_FOX), jnp.tile(sb_k_gain[i], H_SB)])[None, :]
            q, k, vt, cf = _attn_inproj(h, attn_norm[i][None, :], w_qk, wvt, wf, fb, qk_gain)
            o_fox, o_sb = _attention(q, k, vt, cf)
            w_out = attn_w_out[i].astype(BF16)
            h = _attn_out(h, o_fox, o_sb, w_out[:fox_w], w_out[fox_w:])
        else:
            h = _conv_mixer(h, conv_norm[i][None, :], conv_w_in[i].astype(BF16),
                            conv_kernel[i], conv_w_out[i].astype(BF16))
        h = _ffn(h, ffn_norm[layer][None, :], ffn_w_up[layer].astype(BF16), ffn_conv[layer],
                 ffn_w_down[layer].astype(BF16))
    return h
```

```python
import functools

import jax
import jax.numpy as jnp
from jax import lax
from jax.experimental import pallas as pl
from jax.experimental.pallas import tpu as pltpu

F32 = jnp.float32
BF16 = jnp.bfloat16

D_MODEL = 1024
HEAD_DIM = 64
H_FOX = 8
H_SB = 8
MIX_WIDTH = (H_FOX + H_SB) * HEAD_DIM
D_FF = 2816
EPS = 1e-6
QK_SCALE = HEAD_DIM ** -0.5
LOG2E = 1.4426950408889634

LANES = 128
SUBLANES = 8
HEADS_PER_BLOCK = LANES // HEAD_DIM
BF16_ROWS = 2 * SUBLANES
MXU_DIM = 256
EXP2_CLAMP = 126.0
VMEM_LIMIT = 56 * 1024 * 1024

TM_PROJ = 256
TM_FFN = 512
FF_CHUNK = 256
T_FOX = 512
T_SB = 256
SB_EXIT = 160.0
CONV_HALO = SUBLANES


def _params(sem):
    return pltpu.CompilerParams(dimension_semantics=sem, vmem_limit_bytes=VMEM_LIMIT)


def _resident(shape):
    zeros = (0,) * len(shape)
    return pl.BlockSpec(shape, lambda *_: zeros, pipeline_mode=pl.Buffered(1))


def _rms_norm(x, g):
    ms = jnp.mean(x * x, axis=-1, keepdims=True)
    return x * lax.rsqrt(ms + EPS) * g


def _split3(x):
    hi = x.astype(BF16)
    r = x - hi.astype(F32)
    mid = r.astype(BF16)
    lo = (r - mid.astype(F32)).astype(BF16)
    return hi, mid, lo


def _dot_nt(a, b):
    return lax.dot_general(a, b, (((1,), (1,)), ((), ())), preferred_element_type=F32)


def _causal_conv3(ext_ref, tail, cur, w, tm):
    ext_ref[0:CONV_HALO, :] = tail
    ext_ref[CONV_HALO:CONV_HALO + tm, :] = cur
    return (w[2:3, :] * cur
            + w[1:2, :] * ext_ref[CONV_HALO - 1:CONV_HALO - 1 + tm, :]
            + w[0:1, :] * ext_ref[CONV_HALO - 2:CONV_HALO - 2 + tm, :])


def _attn_inproj_kernel(h_ref, g_ref, w_ref, wvt_ref, wf_ref, fb_ref, qkg_ref, gmat_ref,
                        tri_ref, q_ref, k_ref, vt_ref, cf_ref, carry_ref, *, tm):
    i = pl.program_id(1)

    @pl.when(i == 0)
    def _():
        carry_ref[...] = jnp.zeros_like(carry_ref)

    xn = _rms_norm(h_ref[...], g_ref[...]).astype(BF16)
    gmat = gmat_ref[...]

    n_blk = MIX_WIDTH // MXU_DIM
    proj = [jnp.dot(xn, w_ref[:, c * MXU_DIM:(c + 1) * MXU_DIM], preferred_element_type=F32)
            for c in range(2 * n_blk)]
    vt_ref[...] = _dot_nt(wvt_ref[...], xn).astype(BF16)
    fl = jnp.dot(xn, wf_ref[...], preferred_element_type=F32) + fb_ref[...]

    for c in range(2 * n_blk):
        out_ref, scale = (q_ref, QK_SCALE * LOG2E) if c < n_blk else (k_ref, 1.0)
        t = proj[c]
        ssq = jnp.dot((t * t).astype(BF16), gmat, preferred_element_type=F32)
        gain = qkg_ref[:, c * MXU_DIM:(c + 1) * MXU_DIM]
        tn = t * lax.rsqrt(ssq * (1.0 / HEAD_DIM) + EPS) * gain
        if scale != 1.0:
            tn = tn * scale
        lo = (c % n_blk) * MXU_DIM
        out_ref[:, lo:lo + MXU_DIM] = tn.astype(BF16)

    log_f = jnp.minimum(fl, 0.0) - jnp.log1p(jnp.exp(-jnp.abs(fl)))
    tri = tri_ref[...]
    hi, mid, lo = _split3(log_f)
    cs = (jnp.dot(tri, hi, preferred_element_type=F32)
          + jnp.dot(tri, mid, preferred_element_type=F32)
          + jnp.dot(tri, lo, preferred_element_type=F32))
    cf = cs + carry_ref[0:1, :]
    carry_ref[...] = jnp.broadcast_to(cf[tm - 1:tm, :], carry_ref.shape)
    cf_ref[...] = cf


def _attn_inproj(h, g, w_qk, wvt, wf, fb, qk_gain):
    b, s, d = h.shape
    tm = TM_PROJ
    r = lax.broadcasted_iota(jnp.int32, (MXU_DIM, MXU_DIM), 0) // HEAD_DIM
    c = lax.broadcasted_iota(jnp.int32, (MXU_DIM, MXU_DIM), 1) // HEAD_DIM
    gmat = (r == c).astype(BF16)
    rr = lax.broadcasted_iota(jnp.int32, (tm, tm), 0)
    cc = lax.broadcasted_iota(jnp.int32, (tm, tm), 1)
    tri = (cc <= rr).astype(BF16)
    row_spec = pl.BlockSpec((None, tm, MIX_WIDTH), lambda bi, i: (bi, i, 0))
    return pl.pallas_call(
        functools.partial(_attn_inproj_kernel, tm=tm),
        grid=(b, s // tm),
        in_specs=[
            pl.BlockSpec((None, tm, d), lambda bi, i: (bi, i, 0)),
            _resident(g.shape), _resident(w_qk.shape), _resident(wvt.shape),
            _resident(wf.shape), _resident(fb.shape), _resident(qk_gain.shape),
            _resident(gmat.shape), _resident(tri.shape),
        ],
        out_specs=[row_spec, row_spec,
                   pl.BlockSpec((None, MIX_WIDTH, tm), lambda bi, i: (bi, 0, i)),
                   pl.BlockSpec((None, tm, LANES), lambda bi, i: (bi, i, 0))],
        out_shape=[jax.ShapeDtypeStruct((b, s, MIX_WIDTH), BF16),
                   jax.ShapeDtypeStruct((b, s, MIX_WIDTH), BF16),
                   jax.ShapeDtypeStruct((b, MIX_WIDTH, s), BF16),
                   jax.ShapeDtypeStruct((b, s, LANES), F32)],
        scratch_shapes=[pltpu.VMEM((SUBLANES, LANES), F32)],
        compiler_params=_params(("arbitrary", "arbitrary")),
        name="attn_inproj",
    )(h, g, w_qk, wvt, wf, fb, qk_gain, gmat, tri)


def _head_masks(t):
    lane = lax.broadcasted_iota(jnp.int32, (t, LANES), 1)
    return [(lane >= HEAD_DIM * hh) & (lane < HEAD_DIM * (hh + 1))
            for hh in range(HEADS_PER_BLOCK)]


def _lane_pieces(t):
    return [slice(c * LANES, (c + 1) * LANES) for c in range(t // LANES)]


def _sweep_key_tiles(i, scores, step, more_needed):
    scores(i, 0)
    step(i, 0, True, (jnp.maximum(i - 1, 0), 1))

    def unfinished(carry):
        r, go = carry
        return jnp.logical_and(r < i // 2, go > 0)

    def pair(carry):
        r, _ = carry
        j = i - 1 - 2 * r
        step(j, 1, False, (jnp.maximum(j - 1, 0), 0))
        step(j - 1, 0, False, (jnp.maximum(j - 2, 0), 1))
        return r + 1, more_needed()

    _, go = lax.while_loop(unfinished, pair, (jnp.int32(0), more_needed()))

    @pl.when(jnp.logical_and(i % 2 == 1, go > 0))
    def _():
        step(0, 1, False, None)


def _write_heads(o_ref, outs_t):
    o_ref[...] = jnp.concatenate(outs_t, axis=0).T.astype(o_ref.dtype)


def _fox_parts(q_ref, k_ref, vt_ref, cfb_ref, st_ref, m_ref, acc_ref, t):
    q = q_ref[...]
    qh = [jnp.where(mk, q, jnp.zeros_like(q)) for mk in _head_masks(t)]
    m_ref[...] = jnp.full_like(m_ref, -jnp.inf)
    acc_ref[...] = jnp.zeros_like(acc_ref)
    key = lax.broadcasted_iota(jnp.int32, (t, LANES), 0)
    qry = lax.broadcasted_iota(jnp.int32, (t, LANES), 1)
    pieces = _lane_pieces(t)
    ones_rows = jnp.ones((BF16_ROWS, t), BF16)

    def scores(j, slot, hh):
        off = pl.multiple_of(j * t, t)
        st = _dot_nt(k_ref[pl.ds(off, t), :], qh[hh])
        fcol = cfb_ref[hh, pl.ds(off, t), :]
        for cols in pieces:
            st_ref[slot, hh, :, cols] = st[:, cols] - fcol

    def consume(j, slot, masked, hh):
        off = pl.multiple_of(j * t, t)
        m_old = m_ref[hh]
        m_new, alpha, prob = [], [], []
        for c, cols in enumerate(pieces):
            x = st_ref[slot, hh, :, cols]
            if masked:
                x = jnp.where(key <= qry + c * LANES, x, -jnp.inf)
            m_o = m_old[:, cols]
            m_n = jnp.maximum(m_o, jnp.max(x, axis=0, keepdims=True))
            m_new.append(m_n)
            alpha.append(jnp.exp2(m_o - m_n))
            prob.append(jnp.exp2(x - m_n).astype(BF16))
        vt_h = jnp.concatenate(
            [vt_ref[hh * HEAD_DIM:(hh + 1) * HEAD_DIM, pl.ds(off, t)], ones_rows], axis=0)
        pv = jnp.dot(vt_h, jnp.concatenate(prob, axis=1), preferred_element_type=F32)
        acc_ref[hh] = jnp.concatenate(alpha, axis=1) * acc_ref[hh] + pv
        m_ref[hh] = jnp.concatenate(m_new, axis=1)

    def result():
        return [acc_ref[hh, 0:HEAD_DIM, :] / acc_ref[hh, HEAD_DIM:HEAD_DIM + 1, :]
                for hh in range(HEADS_PER_BLOCK)]

    return scores, consume, result


def _sb_parts(q_ref, k_ref, vt_ref, tri_ref, z_ref, r_ref, acc_ref, t):
    q = q_ref[...]
    qh = [jnp.where(mk, q, jnp.zeros_like(q)) for mk in _head_masks(t)]
    r_ref[...] = jnp.zeros_like(r_ref)
    acc_ref[...] = jnp.zeros_like(acc_ref)
    key = lax.broadcasted_iota(jnp.int32, (t, t), 0)
    qry = lax.broadcasted_iota(jnp.int32, (t, t), 1)
    strict = key < qry
    n_sub = t // MXU_DIM

    sub_rows = [slice(c * MXU_DIM, (c + 1) * MXU_DIM) for c in range(n_sub)]

    def scores(j, slot, hh):
        off = pl.multiple_of(j * t, t)
        z_ref[slot, hh] = _dot_nt(k_ref[pl.ds(off, t), :], qh[hh])

    def gate_sums(slot, masked, hh):
        z = z_ref[slot, hh]
        a = jnp.maximum(LOG2E * jnp.log(1.0 + jnp.exp2(jnp.minimum(z, EXP2_CLAMP))), z)
        base = z - a
        if masked:
            a = jnp.where(strict, a, 0.0)
            base = jnp.where(strict, base, -jnp.inf)
        tri = tri_ref[...]
        sums = [jnp.dot(tri, a[rows, :].astype(BF16), preferred_element_type=F32)
                for rows in sub_rows]
        return base, sums

    def accumulate(j, staged, hh):
        off = pl.multiple_of(j * t, t)
        base, sums = staged
        r_run = r_ref[hh]
        w = [None] * n_sub
        for c in reversed(range(n_sub)):
            later = sums[c][0:MXU_DIM, :]
            w[c] = jnp.exp2(base[sub_rows[c], :] - later - r_run).astype(BF16)
            r_run = r_run + sums[c][MXU_DIM:MXU_DIM + 1, :]
        vt_h = vt_ref[hh * HEAD_DIM:(hh + 1) * HEAD_DIM, pl.ds(off, t)]
        acc_ref[hh] += jnp.dot(vt_h, jnp.concatenate(w, axis=0), preferred_element_type=F32)
        r_ref[hh] = r_run

    def result():
        return [acc_ref[hh] for hh in range(HEADS_PER_BLOCK)]

    return scores, gate_sums, accumulate, result


def _fox_kernel(q_ref, k_ref, vt_ref, cf_ref, o_ref, cfb_ref, st_ref, m_ref, acc_ref, *, t):
    p = pl.program_id(1)
    i = pl.program_id(2)

    @pl.when(i == 0)
    def _():
        sel_row = lax.broadcasted_iota(jnp.int32, (LANES, LANES), 0)
        for hh in range(HEADS_PER_BLOCK):
            sel = (sel_row == p * HEADS_PER_BLOCK + hh).astype(BF16)

            def fill(r, carry):
                off = pl.multiple_of(r * t, t)
                hi, mid, lo = _split3(cf_ref[pl.ds(off, t), :])
                cfb_ref[hh, pl.ds(off, t), :] = LOG2E * (
                    jnp.dot(hi, sel, preferred_element_type=F32)
                    + jnp.dot(mid, sel, preferred_element_type=F32)
                    + jnp.dot(lo, sel, preferred_element_type=F32))
                return carry

            lax.fori_loop(0, cf_ref.shape[0] // t, fill, 0)

    scores_h, consume_h, result = _fox_parts(q_ref, k_ref, vt_ref, cfb_ref, st_ref, m_ref,
                                             acc_ref, t)
    heads = range(HEADS_PER_BLOCK)

    def scores(j, slot):
        for hh in heads:
            scores_h(j, slot, hh)

    def step(j, slot, masked, nxt):
        for hh in heads:
            if nxt is not None:
                scores_h(*nxt, hh)
            consume_h(j, slot, masked, hh)

    _sweep_key_tiles(i, scores, step, lambda: jnp.int32(1))
    _write_heads(o_ref, result())


def _sb_kernel(q_ref, k_ref, vt_ref, tri_ref, o_ref, z_ref, r_ref, acc_ref, *, t):
    i = pl.program_id(2)
    scores_h, gate_sums_h, accumulate_h, result = _sb_parts(q_ref, k_ref, vt_ref, tri_ref,
                                                            z_ref, r_ref, acc_ref, t)
    heads = range(HEADS_PER_BLOCK)

    def scores(j, slot):
        for hh in heads:
            scores_h(j, slot, hh)

    def step(j, slot, masked, nxt):
        staged = []
        for hh in heads:
            if nxt is not None:
                scores_h(*nxt, hh)
            staged.append(gate_sums_h(slot, masked, hh))
        for hh in heads:
            accumulate_h(j, staged[hh], hh)

    def more_needed():
        return (jnp.min(r_ref[...]) < SB_EXIT).astype(jnp.int32)

    _sweep_key_tiles(i, scores, step, more_needed)
    _write_heads(o_ref, result())


def _attention(q, k, vt, cf):
    b, s, _ = q.shape
    nblk = H_FOX // HEADS_PER_BLOCK
    rr = lax.broadcasted_iota(jnp.int32, (MXU_DIM + BF16_ROWS, MXU_DIM), 0)
    cc = lax.broadcasted_iota(jnp.int32, (MXU_DIM + BF16_ROWS, MXU_DIM), 1)
    tri = ((cc > rr) | (rr >= MXU_DIM)).astype(BF16)

    def specs(t, first):
        return [pl.BlockSpec((None, t, LANES), lambda bi, p, i: (bi, i, p + first)),
                pl.BlockSpec((None, s, LANES), lambda bi, p, i: (bi, 0, p + first)),
                pl.BlockSpec((None, LANES, s), lambda bi, p, i: (bi, p + first, 0))]

    def out_spec(t):
        return pl.BlockSpec((None, t, LANES), lambda bi, p, i: (bi, i, p))

    params = _params(("arbitrary", "arbitrary", "arbitrary"))
    t = T_FOX
    o_fox = pl.pallas_call(
        functools.partial(_fox_kernel, t=t),
        grid=(b, nblk, s // t),
        in_specs=specs(t, 0) + [pl.BlockSpec((None, s, LANES), lambda bi, p, i: (bi, 0, 0),
                                             pipeline_mode=pl.Buffered(1))],
        out_specs=out_spec(t),
        out_shape=jax.ShapeDtypeStruct((b, s, H_FOX * HEAD_DIM), BF16),
        scratch_shapes=[pltpu.VMEM((HEADS_PER_BLOCK, s, LANES), F32),
                        pltpu.VMEM((2, HEADS_PER_BLOCK, t, t), F32),
                        pltpu.VMEM((HEADS_PER_BLOCK, 1, t), F32),
                        pltpu.VMEM((HEADS_PER_BLOCK, HEAD_DIM + BF16_ROWS, t), F32)],
        compiler_params=params,
        name="fox_attention",
    )(q, k, vt, cf)
    t = T_SB
    o_sb = pl.pallas_call(
        functools.partial(_sb_kernel, t=t),
        grid=(b, nblk, s // t),
        in_specs=specs(t, nblk) + [_resident(tri.shape)],
        out_specs=out_spec(t),
        out_shape=jax.ShapeDtypeStruct((b, s, H_SB * HEAD_DIM), BF16),
        scratch_shapes=[pltpu.VMEM((2, HEADS_PER_BLOCK, t, t), F32),
                        pltpu.VMEM((HEADS_PER_BLOCK, 1, t), F32),
                        pltpu.VMEM((HEADS_PER_BLOCK, HEAD_DIM, t), F32)],
        compiler_params=params,
        name="sb_attention",
    )(q, k, vt, tri)
    return o_fox, o_sb


def _attn_out_kernel(h_ref, of_ref, os_ref, wf_ref, ws_ref, o_ref):
    o_ref[...] = (h_ref[...]
                  + jnp.dot(of_ref[...], wf_ref[...], preferred_element_type=F32)
                  + jnp.dot(os_ref[...], ws_ref[...], preferred_element_type=F32))


def _attn_out(h, o_fox, o_sb, w_fox, w_sb):
    b, s, d = h.shape
    tm = TM_PROJ
    return pl.pallas_call(
        _attn_out_kernel,
        grid=(b, s // tm),
        in_specs=[
            pl.BlockSpec((None, tm, d), lambda bi, i: (bi, i, 0)),
            pl.BlockSpec((None, tm, o_fox.shape[-1]), lambda bi, i: (bi, i, 0)),
            pl.BlockSpec((None, tm, o_sb.shape[-1]), lambda bi, i: (bi, i, 0)),
            _resident(w_fox.shape), _resident(w_sb.shape),
        ],
        out_specs=pl.BlockSpec((None, tm, d), lambda bi, i: (bi, i, 0)),
        out_shape=jax.ShapeDtypeStruct(h.shape, F32),
        compiler_params=_params(("arbitrary", "arbitrary")),
        name="attn_out",
    )(h, o_fox, o_sb, w_fox, w_sb)


def _conv_mixer_kernel(h_ref, g_ref, win_ref, cw_ref, wout_ref, o_ref, ext_ref, tail_ref,
                       *, tm):
    i = pl.program_id(1)

    @pl.when(i == 0)
    def _():
        tail_ref[...] = jnp.zeros_like(tail_ref)

    x = h_ref[...]
    xn = _rms_norm(x, g_ref[...]).astype(BF16)
    d = x.shape[-1]
    gate_c = jnp.dot(xn, win_ref[:, d:2 * d], preferred_element_type=F32)
    u = jnp.dot(xn, win_ref[:, 2 * d:3 * d], preferred_element_type=F32)
    cu = gate_c * u
    y = _causal_conv3(ext_ref, tail_ref[...], cu, cw_ref[...], tm)
    tail_ref[...] = cu[tm - CONV_HALO:tm, :]
    gate_b = jnp.dot(xn, win_ref[:, 0:d], preferred_element_type=F32)
    y = (gate_b * y).astype(BF16)
    o_ref[...] = x + jnp.dot(y, wout_ref[...], preferred_element_type=F32)


def _conv_mixer(h, g, w_in, conv_w, w_out):
    b, s, d = h.shape
    tm = TM_PROJ
    return pl.pallas_call(
        functools.partial(_conv_mixer_kernel, tm=tm),
        grid=(b, s // tm),
        in_specs=[
            pl.BlockSpec((None, tm, d), lambda bi, i: (bi, i, 0)),
            _resident(g.shape), _resident(w_in.shape), _resident(conv_w.shape),
            _resident(w_out.shape),
        ],
        out_specs=pl.BlockSpec((None, tm, d), lambda bi, i: (bi, i, 0)),
        out_shape=jax.ShapeDtypeStruct(h.shape, F32),
        scratch_shapes=[pltpu.VMEM((tm + CONV_HALO, d), F32),
                        pltpu.VMEM((CONV_HALO, d), F32)],
        compiler_params=_params(("arbitrary", "arbitrary")),
        name="conv_mixer",
    )(h, g, w_in, conv_w, w_out)


def _ffn_kernel(h_ref, g_ref, wup_ref, cw_ref, wd_ref, o_ref,
                xn_ref, ext_ref, tail_ref, act_ref, *, tm, fc):
    i = pl.program_id(1)
    n_chunks = D_FF // fc

    @pl.when(i == 0)
    def _():
        tail_ref[...] = jnp.zeros_like(tail_ref)

    x = h_ref[...]
    xn_ref[...] = _rms_norm(x, g_ref[...]).astype(BF16)

    def up_project(c):
        slot = c % 2
        for part in range(2):
            cols = slice(part * D_FF + c * fc, part * D_FF + (c + 1) * fc)
            ext_ref[slot, part, 0:CONV_HALO, :] = tail_ref[part, :, c * fc:(c + 1) * fc]
            ext_ref[slot, part, CONV_HALO:CONV_HALO + tm, :] = jnp.dot(
                xn_ref[...], wup_ref[:, cols], preferred_element_type=F32)

    def conv_gate(c):
        slot = c % 2
        y = []
        for part in range(2):
            cols = slice(part * D_FF + c * fc, part * D_FF + (c + 1) * fc)
            w = cw_ref[:, cols]
            taps = [w[k:k + 1, :] * ext_ref[slot, part, CONV_HALO - 2 + k:CONV_HALO - 2 + k + tm, :]
                    for k in range(3)]
            y.append(taps[2] + taps[1] + taps[0])
            tail_ref[part, :, c * fc:(c + 1) * fc] = ext_ref[slot, part, tm:tm + CONV_HALO, :]
        act_ref[:, c * fc:(c + 1) * fc] = (y[0] / (1.0 + jnp.exp(-y[0])) * y[1]).astype(BF16)

    up_project(0)
    for c in range(n_chunks):
        if c + 1 < n_chunks:
            up_project(c + 1)
        conv_gate(c)
    o_ref[...] = x + jnp.dot(act_ref[...], wd_ref[...], preferred_element_type=F32)


def _ffn(h, g, w_up, conv_w, w_down):
    b, s, d = h.shape
    tm = TM_FFN
    fc = FF_CHUNK
    return pl.pallas_call(
        functools.partial(_ffn_kernel, tm=tm, fc=fc),
        grid=(b, s // tm),
        in_specs=[
            pl.BlockSpec((None, tm, d), lambda bi, i: (bi, i, 0)),
            _resident(g.shape), _resident(w_up.shape), _resident(conv_w.shape),
            _resident(w_down.shape),
        ],
        out_specs=pl.BlockSpec((None, tm, d), lambda bi, i: (bi, i, 0)),
        out_shape=jax.ShapeDtypeStruct(h.shape, F32),
        scratch_shapes=[pltpu.VMEM((tm, d), BF16),
                        pltpu.VMEM((2, 2, tm + CONV_HALO, fc), F32),
                        pltpu.VMEM((2, CONV_HALO, D_FF), F32),
                        pltpu.VMEM((tm, D_FF), BF16)],
        compiler_params=_params(("arbitrary", "arbitrary")),
        name="conv_ffn",
    )(h, g, w_up, conv_w, w_down)


def kernel(x, attn_norm, attn_w_in, attn_f_bias, fox_q_gain, fox_k_gain, sb_q_gain, sb_k_gain,
           attn_w_out, conv_norm, conv_w_in, conv_kernel, conv_w_out, ffn_norm, ffn_w_up,
           ffn_conv, ffn_w_down):
    depth = ffn_norm.shape[0]
    fox_w = H_FOX * HEAD_DIM
    h = x
    for layer in range(depth):
        i = layer // 2
        if layer % 2 == 0:
            w_in = attn_w_in[i]
            w_qk = w_in[:, :2 * MIX_WIDTH].astype(BF16)
            wvt = w_in[:, 2 * MIX_WIDTH:3 * MIX_WIDTH].T.astype(BF16)
            wf = jnp.zeros((D_MODEL, LANES), BF16).at[:, :H_FOX].set(
                w_in[:, 3 * MIX_WIDTH:].astype(BF16))
            fb = jnp.zeros((1, LANES), F32).at[0, :H_FOX].set(attn_f_bias[i])
            qk_gain = jnp.concatenate(
                [jnp.tile(fox_q_gain[i], H_FOX), jnp.tile(sb_q_gain[i], H_SB),
                 jnp.tile(fox_k_gain[i], H_FOX), jnp.tile(sb_k_gain[i], H_SB)])[None, :]
            q, k, vt, cf = _attn_inproj(h, attn_norm[i][None, :], w_qk, wvt, wf, fb, qk_gain)
            o_fox, o_sb = _attention(q, k, vt, cf)
            w_out = attn_w_out[i].astype(BF16)
            h = _attn_out(h, o_fox, o_sb, w_out[:fox_w], w_out[fox_w:])
        else:
            h = _conv_mixer(h, conv_norm[i][None, :], conv_w_in[i].astype(BF16),
                            conv_kernel[i], conv_w_out[i].astype(BF16))
        h = _ffn(h, ffn_norm[layer][None, :], ffn_w_up[layer].astype(BF16), ffn_conv[layer],
                 ffn_w_down[layer].astype(BF16))
    return h
```

```python
import functools

import jax
import jax.numpy as jnp
from jax import lax
from jax.experimental import pallas as pl
from jax.experimental.pallas import tpu as pltpu

F32 = jnp.float32
BF16 = jnp.bfloat16

D_MODEL = 1024
HEAD_DIM = 64
H_FOX = 8
H_SB = 8
MIX_WIDTH = (H_FOX + H_SB) * HEAD_DIM
D_FF = 2816
EPS = 1e-6
QK_SCALE = HEAD_DIM ** -0.5
LOG2E = 1.4426950408889634

LANES = 128
SUBLANES = 8
HEADS_PER_BLOCK = LANES // HEAD_DIM
BF16_ROWS = 2 * SUBLANES
MXU_DIM = 256
EXP2_CLAMP = 126.0
VMEM_LIMIT = 56 * 1024 * 1024

TM_PROJ = 256
TM_FFN = 512
FF_CHUNK = 256
T_FOX = 512
T_SB = 256
SB_EXIT = 160.0
FOX_EXIT = 160.0
NORM_SLACK = 1.01
CONV_HALO = SUBLANES


def _params(sem):
    return pltpu.CompilerParams(dimension_semantics=sem, vmem_limit_bytes=VMEM_LIMIT)


def _resident(shape):
    zeros = (0,) * len(shape)
    return pl.BlockSpec(shape, lambda *_: zeros, pipeline_mode=pl.Buffered(1))


def _rms_norm(x, g):
    ms = jnp.mean(x * x, axis=-1, keepdims=True)
    return x * lax.rsqrt(ms + EPS) * g


def _split3(x):
    hi = x.astype(BF16)
    r = x - hi.astype(F32)
    mid = r.astype(BF16)
    lo = (r - mid.astype(F32)).astype(BF16)
    return hi, mid, lo


def _dot_nt(a, b):
    return lax.dot_general(a, b, (((1,), (1,)), ((), ())), preferred_element_type=F32)


def _causal_conv3(ext_ref, tail, cur, w, tm):
    ext_ref[0:CONV_HALO, :] = tail
    ext_ref[CONV_HALO:CONV_HALO + tm, :] = cur
    return (w[2:3, :] * cur
            + w[1:2, :] * ext_ref[CONV_HALO - 1:CONV_HALO - 1 + tm, :]
            + w[0:1, :] * ext_ref[CONV_HALO - 2:CONV_HALO - 2 + tm, :])


def _attn_inproj_kernel(h_ref, g_ref, w_ref, wvt_ref, wf_ref, fb_ref, qkg_ref, gmat_ref,
                        tri_ref, q_ref, k_ref, vt_ref, cf_ref, carry_ref, *, tm):
    i = pl.program_id(1)

    @pl.when(i == 0)
    def _():
        carry_ref[...] = jnp.zeros_like(carry_ref)

    xn = _rms_norm(h_ref[...], g_ref[...]).astype(BF16)
    gmat = gmat_ref[...]

    n_blk = MIX_WIDTH // MXU_DIM
    proj = [jnp.dot(xn, w_ref[:, c * MXU_DIM:(c + 1) * MXU_DIM], preferred_element_type=F32)
            for c in range(2 * n_blk)]
    vt_ref[...] = _dot_nt(wvt_ref[...], xn).astype(BF16)
    fl = jnp.dot(xn, wf_ref[...], preferred_element_type=F32) + fb_ref[...]

    for c in range(2 * n_blk):
        out_ref, scale = (q_ref, QK_SCALE * LOG2E) if c < n_blk else (k_ref, 1.0)
        t = proj[c]
        ssq = jnp.dot((t * t).astype(BF16), gmat, preferred_element_type=F32)
        gain = qkg_ref[:, c * MXU_DIM:(c + 1) * MXU_DIM]
        tn = t * lax.rsqrt(ssq * (1.0 / HEAD_DIM) + EPS) * gain
        if scale != 1.0:
            tn = tn * scale
        lo = (c % n_blk) * MXU_DIM
        out_ref[:, lo:lo + MXU_DIM] = tn.astype(BF16)

    log_f = jnp.minimum(fl, 0.0) - jnp.log1p(jnp.exp(-jnp.abs(fl)))
    tri = tri_ref[...]
    hi, mid, lo = _split3(log_f)
    cs = (jnp.dot(tri, hi, preferred_element_type=F32)
          + jnp.dot(tri, mid, preferred_element_type=F32)
          + jnp.dot(tri, lo, preferred_element_type=F32))
    cf = cs + carry_ref[0:1, :]
    carry_ref[...] = jnp.broadcast_to(cf[tm - 1:tm, :], carry_ref.shape)
    cf_ref[...] = cf


def _attn_inproj(h, g, w_qk, wvt, wf, fb, qk_gain):
    b, s, d = h.shape
    tm = TM_PROJ
    r = lax.broadcasted_iota(jnp.int32, (MXU_DIM, MXU_DIM), 0) // HEAD_DIM
    c = lax.broadcasted_iota(jnp.int32, (MXU_DIM, MXU_DIM), 1) // HEAD_DIM
    gmat = (r == c).astype(BF16)
    rr = lax.broadcasted_iota(jnp.int32, (tm, tm), 0)
    cc = lax.broadcasted_iota(jnp.int32, (tm, tm), 1)
    tri = (cc <= rr).astype(BF16)
    row_spec = pl.BlockSpec((None, tm, MIX_WIDTH), lambda bi, i: (bi, i, 0))
    return pl.pallas_call(
        functools.partial(_attn_inproj_kernel, tm=tm),
        grid=(b, s // tm),
        in_specs=[
            pl.BlockSpec((None, tm, d), lambda bi, i: (bi, i, 0)),
            _resident(g.shape), _resident(w_qk.shape), _resident(wvt.shape),
            _resident(wf.shape), _resident(fb.shape), _resident(qk_gain.shape),
            _resident(gmat.shape), _resident(tri.shape),
        ],
        out_specs=[row_spec, row_spec,
                   pl.BlockSpec((None, MIX_WIDTH, tm), lambda bi, i: (bi, 0, i)),
                   pl.BlockSpec((None, tm, LANES), lambda bi, i: (bi, i, 0))],
        out_shape=[jax.ShapeDtypeStruct((b, s, MIX_WIDTH), BF16),
                   jax.ShapeDtypeStruct((b, s, MIX_WIDTH), BF16),
                   jax.ShapeDtypeStruct((b, MIX_WIDTH, s), BF16),
                   jax.ShapeDtypeStruct((b, s, LANES), F32)],
        scratch_shapes=[pltpu.VMEM((SUBLANES, LANES), F32)],
        compiler_params=_params(("arbitrary", "arbitrary")),
        name="attn_inproj",
    )(h, g, w_qk, wvt, wf, fb, qk_gain, gmat, tri)


def _head_masks(t):
    lane = lax.broadcasted_iota(jnp.int32, (t, LANES), 1)
    return [(lane >= HEAD_DIM * hh) & (lane < HEAD_DIM * (hh + 1))
            for hh in range(HEADS_PER_BLOCK)]


def _lane_pieces(t):
    return [slice(c * LANES, (c + 1) * LANES) for c in range(t // LANES)]


def _sweep_key_tiles(i, scores, step, more_needed):
    scores(i, 0)
    step(i, 0, True, (jnp.maximum(i - 1, 0), 1))

    def unfinished(carry):
        r, go = carry
        return jnp.logical_and(r < i // 2, go > 0)

    def pair(carry):
        r, _ = carry
        j = i - 1 - 2 * r
        step(j, 1, False, (jnp.maximum(j - 1, 0), 0))
        step(j - 1, 0, False, (jnp.maximum(j - 2, 0), 1))
        return r + 1, more_needed(jnp.maximum(j - 2, 0))

    _, go = lax.while_loop(unfinished, pair,
                           (jnp.int32(0), more_needed(jnp.maximum(i - 1, 0))))

    @pl.when(jnp.logical_and(i % 2 == 1, go > 0))
    def _():
        step(0, 1, False, None)


def _write_heads(o_ref, outs_t):
    o_ref[...] = jnp.concatenate(outs_t, axis=0).T.astype(o_ref.dtype)


def _fox_parts(q_ref, k_ref, vt_ref, cfb_ref, st_ref, m_ref, acc_ref, t):
    q = q_ref[...]
    qh = [jnp.where(mk, q, jnp.zeros_like(q)) for mk in _head_masks(t)]
    m_ref[...] = jnp.full_like(m_ref, -jnp.inf)
    acc_ref[...] = jnp.zeros_like(acc_ref)
    key = lax.broadcasted_iota(jnp.int32, (t, LANES), 0)
    qry = lax.broadcasted_iota(jnp.int32, (t, LANES), 1)
    pieces = _lane_pieces(t)
    ones_rows = jnp.ones((BF16_ROWS, t), BF16)

    def scores(j, slot, hh):
        off = pl.multiple_of(j * t, t)
        st = _dot_nt(k_ref[pl.ds(off, t), :], qh[hh])
        fcol = cfb_ref[hh, pl.ds(off, t), :]
        for cols in pieces:
            st_ref[slot, hh, :, cols] = st[:, cols] - fcol

    def consume(j, slot, masked, hh):
        off = pl.multiple_of(j * t, t)
        m_old = m_ref[hh]
        m_new, alpha, prob = [], [], []
        for c, cols in enumerate(pieces):
            x = st_ref[slot, hh, :, cols]
            if masked:
                x = jnp.where(key <= qry + c * LANES, x, -jnp.inf)
            m_o = m_old[:, cols]
            m_n = jnp.maximum(m_o, jnp.max(x, axis=0, keepdims=True))
            m_new.append(m_n)
            alpha.append(jnp.exp2(m_o - m_n))
            prob.append(jnp.exp2(x - m_n).astype(BF16))
        vt_h = jnp.concatenate(
            [vt_ref[hh * HEAD_DIM:(hh + 1) * HEAD_DIM, pl.ds(off, t)], ones_rows], axis=0)
        pv = jnp.dot(vt_h, jnp.concatenate(prob, axis=1), preferred_element_type=F32)
        acc_ref[hh] = jnp.concatenate(alpha, axis=1) * acc_ref[hh] + pv
        m_ref[hh] = jnp.concatenate(m_new, axis=1)

    def result():
        return [acc_ref[hh, 0:HEAD_DIM, :] / acc_ref[hh, HEAD_DIM:HEAD_DIM + 1, :]
                for hh in range(HEADS_PER_BLOCK)]

    return scores, consume, result


def _sb_parts(q_ref, k_ref, vt_ref, tri_ref, z_ref, r_ref, acc_ref, t):
    q = q_ref[...]
    qh = [jnp.where(mk, q, jnp.zeros_like(q)) for mk in _head_masks(t)]
    r_ref[...] = jnp.zeros_like(r_ref)
    acc_ref[...] = jnp.zeros_like(acc_ref)
    key = lax.broadcasted_iota(jnp.int32, (t, t), 0)
    qry = lax.broadcasted_iota(jnp.int32, (t, t), 1)
    strict = key < qry
    n_sub = t // MXU_DIM

    sub_rows = [slice(c * MXU_DIM, (c + 1) * MXU_DIM) for c in range(n_sub)]

    def scores(j, slot, hh):
        off = pl.multiple_of(j * t, t)
        z_ref[slot, hh] = _dot_nt(k_ref[pl.ds(off, t), :], qh[hh])

    def gate_sums(slot, masked, hh):
        z = z_ref[slot, hh]
        a = jnp.maximum(LOG2E * jnp.log(1.0 + jnp.exp2(jnp.minimum(z, EXP2_CLAMP))), z)
        base = z - a
        if masked:
            a = jnp.where(strict, a, 0.0)
            base = jnp.where(strict, base, -jnp.inf)
        tri = tri_ref[...]
        sums = [jnp.dot(tri, a[rows, :].astype(BF16), preferred_element_type=F32)
                for rows in sub_rows]
        return base, sums

    def accumulate(j, staged, hh):
        off = pl.multiple_of(j * t, t)
        base, sums = staged
        r_run = r_ref[hh]
        w = [None] * n_sub
        for c in reversed(range(n_sub)):
            later = sums[c][0:MXU_DIM, :]
            w[c] = jnp.exp2(base[sub_rows[c], :] - later - r_run).astype(BF16)
            r_run = r_run + sums[c][MXU_DIM:MXU_DIM + 1, :]
        vt_h = vt_ref[hh * HEAD_DIM:(hh + 1) * HEAD_DIM, pl.ds(off, t)]
        acc_ref[hh] += jnp.dot(vt_h, jnp.concatenate(w, axis=0), preferred_element_type=F32)
        r_ref[hh] = r_run

    def result():
        return [acc_ref[hh] for hh in range(HEADS_PER_BLOCK)]

    return scores, gate_sums, accumulate, result


def _fox_kernel(q_ref, k_ref, vt_ref, cf_ref, o_ref, cfb_ref, kmax_ref, st_ref, m_ref, acc_ref,
                *, t):
    p = pl.program_id(1)
    i = pl.program_id(2)
    heads = range(HEADS_PER_BLOCK)
    lane_row = lax.broadcasted_iota(jnp.int32, (LANES, LANES), 0)
    head_rows = [((lane_row >= HEAD_DIM * hh) & (lane_row < HEAD_DIM * (hh + 1))).astype(BF16)
                 for hh in heads]

    @pl.when(i == 0)
    def _():
        for hh in heads:
            sel = (lane_row == p * HEADS_PER_BLOCK + hh).astype(BF16)

            def fill(r, kmax2):
                off = pl.multiple_of(r * t, t)
                hi, mid, lo = _split3(cf_ref[pl.ds(off, t), :])
                cfb_ref[hh, pl.ds(off, t), :] = LOG2E * (
                    jnp.dot(hi, sel, preferred_element_type=F32)
                    + jnp.dot(mid, sel, preferred_element_type=F32)
                    + jnp.dot(lo, sel, preferred_element_type=F32))
                kt = k_ref[pl.ds(off, t), :].astype(F32)
                ksq = jnp.dot((kt * kt).astype(BF16), head_rows[hh],
                              preferred_element_type=F32)
                return jnp.maximum(kmax2, ksq)

            kmax2 = lax.fori_loop(0, cf_ref.shape[0] // t, fill, jnp.zeros((t, LANES), F32))
            kmax_ref[hh] = jnp.max(kmax2, axis=0, keepdims=True)

    scores_h, consume_h, result = _fox_parts(q_ref, k_ref, vt_ref, cfb_ref, st_ref, m_ref,
                                             acc_ref, t)

    q32 = q_ref[...].astype(F32)
    ones_lhs = jnp.ones((BF16_ROWS, LANES), BF16)
    qk_bound = []
    for hh, mk in zip(heads, _head_masks(t)):
        qsq = jnp.where(mk, q32 * q32, 0.0).astype(BF16)
        qn2 = _dot_nt(ones_lhs, qsq)[0:1, :]
        qk_bound.append(NORM_SLACK * jnp.sqrt(qn2 * kmax_ref[hh][:, 0:1]))

    def more_needed(j):
        row = (j + 1) * t - 1
        need = jnp.int32(0)
        for hh in heads:
            f_hi = cfb_ref[hh, pl.ds(row, 1), :][:, 0:1]
            gap = jnp.max(qk_bound[hh] - f_hi - m_ref[hh])
            need = need | (gap > -FOX_EXIT).astype(jnp.int32)
        return need

    def scores(j, slot):
        for hh in heads:
            scores_h(j, slot, hh)

    def step(j, slot, masked, nxt):
        for hh in heads:
            if nxt is not None:
                scores_h(*nxt, hh)
            consume_h(j, slot, masked, hh)

    _sweep_key_tiles(i, scores, step, more_needed)
    _write_heads(o_ref, result())


def _sb_kernel(q_ref, k_ref, vt_ref, tri_ref, o_ref, z_ref, r_ref, acc_ref, *, t):
    i = pl.program_id(2)
    scores_h, gate_sums_h, accumulate_h, result = _sb_parts(q_ref, k_ref, vt_ref, tri_ref,
                                                            z_ref, r_ref, acc_ref, t)
    heads = range(HEADS_PER_BLOCK)

    def scores(j, slot):
        for hh in heads:
            scores_h(j, slot, hh)

    def step(j, slot, masked, nxt):
        staged = []
        for hh in heads:
            if nxt is not None:
                scores_h(*nxt, hh)
            staged.append(gate_sums_h(slot, masked, hh))
        for hh in heads:
            accumulate_h(j, staged[hh], hh)

    def more_needed(_):
        return (jnp.min(r_ref[...]) < SB_EXIT).astype(jnp.int32)

    _sweep_key_tiles(i, scores, step, more_needed)
    _write_heads(o_ref, result())


def _attention(q, k, vt, cf):
    b, s, _ = q.shape
    nblk = H_FOX // HEADS_PER_BLOCK
    rr = lax.broadcasted_iota(jnp.int32, (MXU_DIM + BF16_ROWS, MXU_DIM), 0)
    cc = lax.broadcasted_iota(jnp.int32, (MXU_DIM + BF16_ROWS, MXU_DIM), 1)
    tri = ((cc > rr) | (rr >= MXU_DIM)).astype(BF16)

    def specs(t, first):
        return [pl.BlockSpec((None, t, LANES), lambda bi, p, i: (bi, i, p + first)),
                pl.BlockSpec((None, s, LANES), lambda bi, p, i: (bi, 0, p + first)),
                pl.BlockSpec((None, LANES, s), lambda bi, p, i: (bi, p + first, 0))]

    def out_spec(t):
        return pl.BlockSpec((None, t, LANES), lambda bi, p, i: (bi, i, p))

    params = _params(("arbitrary", "arbitrary", "arbitrary"))
    t = T_FOX
    o_fox = pl.pallas_call(
        functools.partial(_fox_kernel, t=t),
        grid=(b, nblk, s // t),
        in_specs=specs(t, 0) + [pl.BlockSpec((None, s, LANES), lambda bi, p, i: (bi, 0, 0),
                                             pipeline_mode=pl.Buffered(1))],
        out_specs=out_spec(t),
        out_shape=jax.ShapeDtypeStruct((b, s, H_FOX * HEAD_DIM), BF16),
        scratch_shapes=[pltpu.VMEM((HEADS_PER_BLOCK, s, LANES), F32),
                        pltpu.VMEM((HEADS_PER_BLOCK, 1, LANES), F32),
                        pltpu.VMEM((2, HEADS_PER_BLOCK, t, t), F32),
                        pltpu.VMEM((HEADS_PER_BLOCK, 1, t), F32),
                        pltpu.VMEM((HEADS_PER_BLOCK, HEAD_DIM + BF16_ROWS, t), F32)],
        compiler_params=params,
        name="fox_attention",
    )(q, k, vt, cf)
    t = T_SB
    o_sb = pl.pallas_call(
        functools.partial(_sb_kernel, t=t),
        grid=(b, nblk, s // t),
        in_specs=specs(t, nblk) + [_resident(tri.shape)],
        out_specs=out_spec(t),
        out_shape=jax.ShapeDtypeStruct((b, s, H_SB * HEAD_DIM), BF16),
        scratch_shapes=[pltpu.VMEM((2, HEADS_PER_BLOCK, t, t), F32),
                        pltpu.VMEM((HEADS_PER_BLOCK, 1, t), F32),
                        pltpu.VMEM((HEADS_PER_BLOCK, HEAD_DIM, t), F32)],
        compiler_params=params,
        name="sb_attention",
    )(q, k, vt, tri)
    return o_fox, o_sb


def _attn_out_kernel(h_ref, of_ref, os_ref, wf_ref, ws_ref, o_ref):
    o_ref[...] = (h_ref[...]
                  + jnp.dot(of_ref[...], wf_ref[...], preferred_element_type=F32)
                  + jnp.dot(os_ref[...], ws_ref[...], preferred_element_type=F32))


def _attn_out(h, o_fox, o_sb, w_fox, w_sb):
    b, s, d = h.shape
    tm = TM_PROJ
    return pl.pallas_call(
        _attn_out_kernel,
        grid=(b, s // tm),
        in_specs=[
            pl.BlockSpec((None, tm, d), lambda bi, i: (bi, i, 0)),
            pl.BlockSpec((None, tm, o_fox.shape[-1]), lambda bi, i: (bi, i, 0)),
            pl.BlockSpec((None, tm, o_sb.shape[-1]), lambda bi, i: (bi, i, 0)),
            _resident(w_fox.shape), _resident(w_sb.shape),
        ],
        out_specs=pl.BlockSpec((None, tm, d), lambda bi, i: (bi, i, 0)),
        out_shape=jax.ShapeDtypeStruct(h.shape, F32),
        compiler_params=_params(("arbitrary", "arbitrary")),
        name="attn_out",
    )(h, o_fox, o_sb, w_fox, w_sb)


def _conv_mixer_kernel(h_ref, g_ref, win_ref, cw_ref, wout_ref, o_ref, ext_ref, tail_ref,
                       *, tm):
    i = pl.program_id(1)

    @pl.when(i == 0)
    def _():
        tail_ref[...] = jnp.zeros_like(tail_ref)

    x = h_ref[...]
    xn = _rms_norm(x, g_ref[...]).astype(BF16)
    d = x.shape[-1]
    gate_c = jnp.dot(xn, win_ref[:, d:2 * d], preferred_element_type=F32)
    u = jnp.dot(xn, win_ref[:, 2 * d:3 * d], preferred_element_type=F32)
    cu = gate_c * u
    y = _causal_conv3(ext_ref, tail_ref[...], cu, cw_ref[...], tm)
    tail_ref[...] = cu[tm - CONV_HALO:tm, :]
    gate_b = jnp.dot(xn, win_ref[:, 0:d], preferred_element_type=F32)
    y = (gate_b * y).astype(BF16)
    o_ref[...] = x + jnp.dot(y, wout_ref[...], preferred_element_type=F32)


def _conv_mixer(h, g, w_in, conv_w, w_out):
    b, s, d = h.shape
    tm = TM_PROJ
    return pl.pallas_call(
        functools.partial(_conv_mixer_kernel, tm=tm),
        grid=(b, s // tm),
        in_specs=[
            pl.BlockSpec((None, tm, d), lambda bi, i: (bi, i, 0)),
            _resident(g.shape), _resident(w_in.shape), _resident(conv_w.shape),
            _resident(w_out.shape),
        ],
        out_specs=pl.BlockSpec((None, tm, d), lambda bi, i: (bi, i, 0)),
        out_shape=jax.ShapeDtypeStruct(h.shape, F32),
        scratch_shapes=[pltpu.VMEM((tm + CONV_HALO, d), F32),
                        pltpu.VMEM((CONV_HALO, d), F32)],
        compiler_params=_params(("arbitrary", "arbitrary")),
        name="conv_mixer",
    )(h, g, w_in, conv_w, w_out)


def _ffn_kernel(h_ref, g_ref, wup_ref, cw_ref, wd_ref, o_ref,
                xn_ref, ext_ref, tail_ref, act_ref, *, tm, fc):
    i = pl.program_id(1)
    n_chunks = D_FF // fc

    @pl.when(i == 0)
    def _():
        tail_ref[...] = jnp.zeros_like(tail_ref)

    x = h_ref[...]
    xn_ref[...] = _rms_norm(x, g_ref[...]).astype(BF16)

    def up_project(c):
        slot = c % 2
        for part in range(2):
            cols = slice(part * D_FF + c * fc, part * D_FF + (c + 1) * fc)
            ext_ref[slot, part, 0:CONV_HALO, :] = tail_ref[part, :, c * fc:(c + 1) * fc]
            ext_ref[slot, part, CONV_HALO:CONV_HALO + tm, :] = jnp.dot(
                xn_ref[...], wup_ref[:, cols], preferred_element_type=F32)

    def conv_gate(c):
        slot = c % 2
        y = []
        for part in range(2):
            cols = slice(part * D_FF + c * fc, part * D_FF + (c + 1) * fc)
            w = cw_ref[:, cols]
            taps = [w[k:k + 1, :] * ext_ref[slot, part, CONV_HALO - 2 + k:CONV_HALO - 2 + k + tm, :]
                    for k in range(3)]
            y.append(taps[2] + taps[1] + taps[0])
            tail_ref[part, :, c * fc:(c + 1) * fc] = ext_ref[slot, part, tm:tm + CONV_HALO, :]
        act_ref[:, c * fc:(c + 1) * fc] = (y[0] / (1.0 + jnp.exp(-y[0])) * y[1]).astype(BF16)

    up_project(0)
    for c in range(n_chunks):
        if c + 1 < n_chunks:
            up_project(c + 1)
        conv_gate(c)
    o_ref[...] = x + jnp.dot(act_ref[...], wd_ref[...], preferred_element_type=F32)


def _ffn(h, g, w_up, conv_w, w_down):
    b, s, d = h.shape
    tm = TM_FFN
    fc = FF_CHUNK
    return pl.pallas_call(
        functools.partial(_ffn_kernel, tm=tm, fc=fc),
        grid=(b, s // tm),
        in_specs=[
            pl.BlockSpec((None, tm, d), lambda bi, i: (bi, i, 0)),
            _resident(g.shape), _resident(w_up.shape), _resident(conv_w.shape),
            _resident(w_down.shape),
        ],
        out_specs=pl.BlockSpec((None, tm, d), lambda bi, i: (bi, i, 0)),
        out_shape=jax.ShapeDtypeStruct(h.shape, F32),
        scratch_shapes=[pltpu.VMEM((tm, d), BF16),
                        pltpu.VMEM((2, 2, tm + CONV_HALO, fc), F32),
                        pltpu.VMEM((2, CONV_HALO, D_FF), F32),
                        pltpu.VMEM((tm, D_FF), BF16)],
        compiler_params=_params(("arbitrary", "arbitrary")),
        name="conv_ffn",
    )(h, g, w_up, conv_w, w_down)


def kernel(x, attn_norm, attn_w_in, attn_f_bias, fox_q_gain, fox_k_gain, sb_q_gain, sb_k_gain,
           attn_w_out, conv_norm, conv_w_in, conv_kernel, conv_w_out, ffn_norm, ffn_w_up,
           ffn_conv, ffn_w_down):
    depth = ffn_norm.shape[0]
    fox_w = H_FOX * HEAD_DIM
    h = x
    for layer in range(depth):
        i = layer // 2
        if layer % 2 == 0:
            w_in = attn_w_in[i]
            w_qk = w_in[:, :2 * MIX_WIDTH].astype(BF16)
            wvt = w_in[:, 2 * MIX_WIDTH:3 * MIX_WIDTH].T.astype(BF16)
            wf = jnp.zeros((D_MODEL, LANES), BF16).at[:, :H_FOX].set(
                w_in[:, 3 * MIX_WIDTH:].astype(BF16))
            fb = jnp.zeros((1, LANES), F32).at[0, :H_FOX].set(attn_f_bias[i])
            qk_gain = jnp.concatenate(
                [jnp.tile(fox_q_gain[i], H_FOX), jnp.tile(sb_q_gain[i], H_SB),
                 jnp.tile(fox_k_gain[i], H_FOX), jnp.tile(sb_k_gain[i], H_SB)])[None, :]
            q, k, vt, cf = _attn_inproj(h, attn_norm[i][None, :], w_qk, wvt, wf, fb, qk_gain)
            o_fox, o_sb = _attention(q, k, vt, cf)
            w_out = attn_w_out[i].astype(BF16)
            h = _attn_out(h, o_fox, o_sb, w_out[:fox_w], w_out[fox_w:])
        else:
            h = _conv_mixer(h, conv_norm[i][None, :], conv_w_in[i].astype(BF16),
                            conv_kernel[i], conv_w_out[i].astype(BF16))
        h = _ffn(h, ffn_norm[layer][None, :], ffn_w_up[layer].astype(BF16), ffn_conv[layer],
                 ffn_w_down[layer].astype(BF16))
    return h
```

```python
import functools

import jax
import jax.numpy as jnp
from jax import lax
from jax.experimental import pallas as pl
from jax.experimental.pallas import tpu as pltpu

F32 = jnp.float32
BF16 = jnp.bfloat16

D_MODEL = 1024
HEAD_DIM = 64
H_FOX = 8
H_SB = 8
MIX_WIDTH = (H_FOX + H_SB) * HEAD_DIM
D_FF = 2816
EPS = 1e-6
QK_SCALE = HEAD_DIM ** -0.5
LOG2E = 1.4426950408889634

LANES = 128
SUBLANES = 8
HEADS_PER_BLOCK = LANES // HEAD_DIM
BF16_ROWS = 2 * SUBLANES
MXU_DIM = 256
EXP2_CLAMP = 126.0
VMEM_LIMIT = 56 * 1024 * 1024

TM_PROJ = 256
TM_FFN = 512
FF_CHUNK = 256
T_FOX = 512
T_SB = 256
SB_EXIT = 160.0
FOX_EXIT = 160.0
NORM_SLACK = 1.01
CONV_HALO = SUBLANES


def _params(sem):
    return pltpu.CompilerParams(dimension_semantics=sem, vmem_limit_bytes=VMEM_LIMIT)


def _resident(shape):
    zeros = (0,) * len(shape)
    return pl.BlockSpec(shape, lambda *_: zeros, pipeline_mode=pl.Buffered(1))


def _rms_norm(x, g):
    ms = jnp.mean(x * x, axis=-1, keepdims=True)
    return x * lax.rsqrt(ms + EPS) * g


def _split3(x):
    hi = x.astype(BF16)
    r = x - hi.astype(F32)
    mid = r.astype(BF16)
    lo = (r - mid.astype(F32)).astype(BF16)
    return hi, mid, lo


def _dot_nt(a, b):
    return lax.dot_general(a, b, (((1,), (1,)), ((), ())), preferred_element_type=F32)


def _causal_conv3(ext_ref, tail, cur, w, tm):
    ext_ref[0:CONV_HALO, :] = tail
    ext_ref[CONV_HALO:CONV_HALO + tm, :] = cur
    return (w[2:3, :] * cur
            + w[1:2, :] * ext_ref[CONV_HALO - 1:CONV_HALO - 1 + tm, :]
            + w[0:1, :] * ext_ref[CONV_HALO - 2:CONV_HALO - 2 + tm, :])


def _attn_inproj_kernel(h_ref, g_ref, w_ref, wvt_ref, wf_ref, fb_ref, qkg_ref, gmat_ref,
                        tri_ref, q_ref, k_ref, vt_ref, cf_ref, carry_ref, *, tm):
    i = pl.program_id(1)

    @pl.when(i == 0)
    def _():
        carry_ref[...] = jnp.zeros_like(carry_ref)

    xn = _rms_norm(h_ref[...], g_ref[...]).astype(BF16)
    gmat = gmat_ref[...]

    n_blk = MIX_WIDTH // MXU_DIM
    proj = [jnp.dot(xn, w_ref[:, c * MXU_DIM:(c + 1) * MXU_DIM], preferred_element_type=F32)
            for c in range(2 * n_blk)]
    vt_ref[...] = _dot_nt(wvt_ref[...], xn).astype(BF16)
    fl = jnp.dot(xn, wf_ref[...], preferred_element_type=F32) + fb_ref[...]

    for c in range(2 * n_blk):
        out_ref, scale = (q_ref, QK_SCALE * LOG2E) if c < n_blk else (k_ref, 1.0)
        t = proj[c]
        ssq = jnp.dot((t * t).astype(BF16), gmat, preferred_element_type=F32)
        gain = qkg_ref[:, c * MXU_DIM:(c + 1) * MXU_DIM]
        tn = t * lax.rsqrt(ssq * (1.0 / HEAD_DIM) + EPS) * gain
        if scale != 1.0:
            tn = tn * scale
        lo = (c % n_blk) * MXU_DIM
        out_ref[:, lo:lo + MXU_DIM] = tn.astype(BF16)

    log_f = jnp.minimum(fl, 0.0) - jnp.log1p(jnp.exp(-jnp.abs(fl)))
    tri = tri_ref[...]
    hi, mid, lo = _split3(log_f)
    cs = (jnp.dot(tri, hi, preferred_element_type=F32)
          + jnp.dot(tri, mid, preferred_element_type=F32)
          + jnp.dot(tri, lo, preferred_element_type=F32))
    cf = cs + carry_ref[0:1, :]
    carry_ref[...] = jnp.broadcast_to(cf[tm - 1:tm, :], carry_ref.shape)
    cf_ref[...] = cf


def _attn_inproj(h, g, w_qk, wvt, wf, fb, qk_gain):
    b, s, d = h.shape
    tm = TM_PROJ
    r = lax.broadcasted_iota(jnp.int32, (MXU_DIM, MXU_DIM), 0) // HEAD_DIM
    c = lax.broadcasted_iota(jnp.int32, (MXU_DIM, MXU_DIM), 1) // HEAD_DIM
    gmat = (r == c).astype(BF16)
    rr = lax.broadcasted_iota(jnp.int32, (tm, tm), 0)
    cc = lax.broadcasted_iota(jnp.int32, (tm, tm), 1)
    tri = (cc <= rr).astype(BF16)
    row_spec = pl.BlockSpec((None, tm, MIX_WIDTH), lambda bi, i: (bi, i, 0))
    return pl.pallas_call(
        functools.partial(_attn_inproj_kernel, tm=tm),
        grid=(b, s // tm),
        in_specs=[
            pl.BlockSpec((None, tm, d), lambda bi, i: (bi, i, 0)),
            _resident(g.shape), _resident(w_qk.shape), _resident(wvt.shape),
            _resident(wf.shape), _resident(fb.shape), _resident(qk_gain.shape),
            _resident(gmat.shape), _resident(tri.shape),
        ],
        out_specs=[row_spec, row_spec,
                   pl.BlockSpec((None, MIX_WIDTH, tm), lambda bi, i: (bi, 0, i)),
                   pl.BlockSpec((None, tm, LANES), lambda bi, i: (bi, i, 0))],
        out_shape=[jax.ShapeDtypeStruct((b, s, MIX_WIDTH), BF16),
                   jax.ShapeDtypeStruct((b, s, MIX_WIDTH), BF16),
                   jax.ShapeDtypeStruct((b, MIX_WIDTH, s), BF16),
                   jax.ShapeDtypeStruct((b, s, LANES), F32)],
        scratch_shapes=[pltpu.VMEM((SUBLANES, LANES), F32)],
        compiler_params=_params(("arbitrary", "arbitrary")),
        name="attn_inproj",
    )(h, g, w_qk, wvt, wf, fb, qk_gain, gmat, tri)


def _head_masks(t):
    lane = lax.broadcasted_iota(jnp.int32, (t, LANES), 1)
    return [(lane >= HEAD_DIM * hh) & (lane < HEAD_DIM * (hh + 1))
            for hh in range(HEADS_PER_BLOCK)]


def _lane_pieces(t):
    return [slice(c * LANES, (c + 1) * LANES) for c in range(t // LANES)]


def _sweep_key_tiles(i, scores, step, more_needed):
    scores(i, 0)
    step(i, 0, True, (jnp.maximum(i - 1, 0), 1))

    def unfinished(carry):
        r, go = carry
        return jnp.logical_and(r < i // 2, go > 0)

    def pair(carry):
        r, _ = carry
        j = i - 1 - 2 * r
        step(j, 1, False, (jnp.maximum(j - 1, 0), 0))
        go_mid = more_needed(j - 1)

        @pl.when(go_mid > 0)
        def _():
            step(j - 1, 0, False, (jnp.maximum(j - 2, 0), 1))

        return r + 1, go_mid & more_needed(jnp.maximum(j - 2, 0))

    _, go = lax.while_loop(unfinished, pair,
                           (jnp.int32(0), more_needed(jnp.maximum(i - 1, 0))))

    @pl.when(jnp.logical_and(i % 2 == 1, go > 0))
    def _():
        step(0, 1, False, None)


def _write_heads(o_ref, outs_t):
    o_ref[...] = jnp.concatenate(outs_t, axis=0).T.astype(o_ref.dtype)


def _fox_parts(q_ref, k_ref, vt_ref, cfb_ref, st_ref, m_ref, acc_ref, t):
    q = q_ref[...]
    qh = [jnp.where(mk, q, jnp.zeros_like(q)) for mk in _head_masks(t)]
    m_ref[...] = jnp.full_like(m_ref, -jnp.inf)
    acc_ref[...] = jnp.zeros_like(acc_ref)
    key = lax.broadcasted_iota(jnp.int32, (t, LANES), 0)
    qry = lax.broadcasted_iota(jnp.int32, (t, LANES), 1)
    pieces = _lane_pieces(t)
    ones_rows = jnp.ones((BF16_ROWS, t), BF16)

    def scores(j, slot, hh):
        off = pl.multiple_of(j * t, t)
        st = _dot_nt(k_ref[pl.ds(off, t), :], qh[hh])
        fcol = cfb_ref[hh, pl.ds(off, t), :]
        for cols in pieces:
            st_ref[slot, hh, :, cols] = st[:, cols] - fcol

    def consume(j, slot, masked, hh):
        off = pl.multiple_of(j * t, t)
        m_old = m_ref[hh]
        m_new, alpha, prob = [], [], []
        for c, cols in enumerate(pieces):
            x = st_ref[slot, hh, :, cols]
            if masked:
                x = jnp.where(key <= qry + c * LANES, x, -jnp.inf)
            m_o = m_old[:, cols]
            m_n = jnp.maximum(m_o, jnp.max(x, axis=0, keepdims=True))
            m_new.append(m_n)
            alpha.append(jnp.exp2(m_o - m_n))
            prob.append(jnp.exp2(x - m_n).astype(BF16))
        vt_h = jnp.concatenate(
            [vt_ref[hh * HEAD_DIM:(hh + 1) * HEAD_DIM, pl.ds(off, t)], ones_rows], axis=0)
        pv = jnp.dot(vt_h, jnp.concatenate(prob, axis=1), preferred_element_type=F32)
        acc_ref[hh] = jnp.concatenate(alpha, axis=1) * acc_ref[hh] + pv
        m_ref[hh] = jnp.concatenate(m_new, axis=1)

    def result():
        return [acc_ref[hh, 0:HEAD_DIM, :] / acc_ref[hh, HEAD_DIM:HEAD_DIM + 1, :]
                for hh in range(HEADS_PER_BLOCK)]

    return scores, consume, result


def _sb_parts(q_ref, k_ref, vt_ref, tri_ref, z_ref, r_ref, acc_ref, t):
    q = q_ref[...]
    qh = [jnp.where(mk, q, jnp.zeros_like(q)) for mk in _head_masks(t)]
    r_ref[...] = jnp.zeros_like(r_ref)
    acc_ref[...] = jnp.zeros_like(acc_ref)
    key = lax.broadcasted_iota(jnp.int32, (t, t), 0)
    qry = lax.broadcasted_iota(jnp.int32, (t, t), 1)
    strict = key < qry
    n_sub = t // MXU_DIM

    sub_rows = [slice(c * MXU_DIM, (c + 1) * MXU_DIM) for c in range(n_sub)]

    def scores(j, slot, hh):
        off = pl.multiple_of(j * t, t)
        z_ref[slot, hh] = _dot_nt(k_ref[pl.ds(off, t), :], qh[hh])

    def gate_sums(slot, masked, hh):
        z = z_ref[slot, hh]
        a = jnp.maximum(LOG2E * jnp.log(1.0 + jnp.exp2(jnp.minimum(z, EXP2_CLAMP))), z)
        base = z - a
        if masked:
            a = jnp.where(strict, a, 0.0)
            base = jnp.where(strict, base, -jnp.inf)
        tri = tri_ref[...]
        sums = [jnp.dot(tri, a[rows, :].astype(BF16), preferred_element_type=F32)
                for rows in sub_rows]
        return base, sums

    def accumulate(j, staged, hh):
        off = pl.multiple_of(j * t, t)
        base, sums = staged
        r_run = r_ref[hh]
        w = [None] * n_sub
        for c in reversed(range(n_sub)):
            later = sums[c][0:MXU_DIM, :]
            w[c] = jnp.exp2(base[sub_rows[c], :] - later - r_run).astype(BF16)
            r_run = r_run + sums[c][MXU_DIM:MXU_DIM + 1, :]
        vt_h = vt_ref[hh * HEAD_DIM:(hh + 1) * HEAD_DIM, pl.ds(off, t)]
        acc_ref[hh] += jnp.dot(vt_h, jnp.concatenate(w, axis=0), preferred_element_type=F32)
        r_ref[hh] = r_run

    def result():
        return [acc_ref[hh] for hh in range(HEADS_PER_BLOCK)]

    return scores, gate_sums, accumulate, result


def _fox_kernel(q_ref, k_ref, vt_ref, cf_ref, o_ref, cfb_ref, kmax_ref, st_ref, m_ref, acc_ref,
                *, t):
    p = pl.program_id(1)
    i = pl.program_id(2)
    heads = range(HEADS_PER_BLOCK)
    lane_row = lax.broadcasted_iota(jnp.int32, (LANES, LANES), 0)
    head_rows = [((lane_row >= HEAD_DIM * hh) & (lane_row < HEAD_DIM * (hh + 1))).astype(BF16)
                 for hh in heads]

    @pl.when(i == 0)
    def _():
        for hh in heads:
            sel = (lane_row == p * HEADS_PER_BLOCK + hh).astype(BF16)

            def fill(r, kmax2):
                off = pl.multiple_of(r * t, t)
                hi, mid, lo = _split3(cf_ref[pl.ds(off, t), :])
                cfb_ref[hh, pl.ds(off, t), :] = LOG2E * (
                    jnp.dot(hi, sel, preferred_element_type=F32)
                    + jnp.dot(mid, sel, preferred_element_type=F32)
                    + jnp.dot(lo, sel, preferred_element_type=F32))
                kt = k_ref[pl.ds(off, t), :].astype(F32)
                ksq = jnp.dot((kt * kt).astype(BF16), head_rows[hh],
                              preferred_element_type=F32)
                return jnp.maximum(kmax2, ksq)

            kmax2 = lax.fori_loop(0, cf_ref.shape[0] // t, fill, jnp.zeros((t, LANES), F32))
            kmax_ref[hh] = jnp.max(kmax2, axis=0, keepdims=True)

    scores_h, consume_h, result = _fox_parts(q_ref, k_ref, vt_ref, cfb_ref, st_ref, m_ref,
                                             acc_ref, t)

    q32 = q_ref[...].astype(F32)
    ones_lhs = jnp.ones((BF16_ROWS, LANES), BF16)
    qk_bound = []
    for hh, mk in zip(heads, _head_masks(t)):
        qsq = jnp.where(mk, q32 * q32, 0.0).astype(BF16)
        qn2 = _dot_nt(ones_lhs, qsq)[0:1, :]
        qk_bound.append(NORM_SLACK * jnp.sqrt(qn2 * kmax_ref[hh][:, 0:1]))

    def more_needed(j):
        row = (j + 1) * t - 1
        need = jnp.int32(0)
        for hh in heads:
            f_hi = cfb_ref[hh, pl.ds(row, 1), :][:, 0:1]
            gap = jnp.max(qk_bound[hh] - f_hi - m_ref[hh])
            need = need | (gap > -FOX_EXIT).astype(jnp.int32)
        return need

    def scores(j, slot):
        for hh in heads:
            scores_h(j, slot, hh)

    def step(j, slot, masked, nxt):
        for hh in heads:
            if nxt is not None:
                scores_h(*nxt, hh)
            consume_h(j, slot, masked, hh)

    _sweep_key_tiles(i, scores, step, more_needed)
    _write_heads(o_ref, result())


def _sb_kernel(q_ref, k_ref, vt_ref, tri_ref, o_ref, z_ref, r_ref, acc_ref, *, t):
    i = pl.program_id(2)
    scores_h, gate_sums_h, accumulate_h, result = _sb_parts(q_ref, k_ref, vt_ref, tri_ref,
                                                            z_ref, r_ref, acc_ref, t)
    heads = range(HEADS_PER_BLOCK)

    def scores(j, slot):
        for hh in heads:
            scores_h(j, slot, hh)

    def step(j, slot, masked, nxt):
        staged = []
        for hh in heads:
            if nxt is not None:
                scores_h(*nxt, hh)
            staged.append(gate_sums_h(slot, masked, hh))
        for hh in heads:
            accumulate_h(j, staged[hh], hh)

    def more_needed(_):
        return (jnp.min(r_ref[...]) < SB_EXIT).astype(jnp.int32)

    _sweep_key_tiles(i, scores, step, more_needed)
    _write_heads(o_ref, result())


def _attention(q, k, vt, cf):
    b, s, _ = q.shape
    nblk = H_FOX // HEADS_PER_BLOCK
    rr = lax.broadcasted_iota(jnp.int32, (MXU_DIM + BF16_ROWS, MXU_DIM), 0)
    cc = lax.broadcasted_iota(jnp.int32, (MXU_DIM + BF16_ROWS, MXU_DIM), 1)
    tri = ((cc > rr) | (rr >= MXU_DIM)).astype(BF16)

    def specs(t, first):
        return [pl.BlockSpec((None, t, LANES), lambda bi, p, i: (bi, i, p + first)),
                pl.BlockSpec((None, s, LANES), lambda bi, p, i: (bi, 0, p + first)),
                pl.BlockSpec((None, LANES, s), lambda bi, p, i: (bi, p + first, 0))]

    def out_spec(t):
        return pl.BlockSpec((None, t, LANES), lambda bi, p, i: (bi, i, p))

    params = _params(("arbitrary", "arbitrary", "arbitrary"))
    t = T_FOX
    o_fox = pl.pallas_call(
        functools.partial(_fox_kernel, t=t),
        grid=(b, nblk, s // t),
        in_specs=specs(t, 0) + [pl.BlockSpec((None, s, LANES), lambda bi, p, i: (bi, 0, 0),
                                             pipeline_mode=pl.Buffered(1))],
        out_specs=out_spec(t),
        out_shape=jax.ShapeDtypeStruct((b, s, H_FOX * HEAD_DIM), BF16),
        scratch_shapes=[pltpu.VMEM((HEADS_PER_BLOCK, s, LANES), F32),
                        pltpu.VMEM((HEADS_PER_BLOCK, 1, LANES), F32),
                        pltpu.VMEM((2, HEADS_PER_BLOCK, t, t), F32),
                        pltpu.VMEM((HEADS_PER_BLOCK, 1, t), F32),
                        pltpu.VMEM((HEADS_PER_BLOCK, HEAD_DIM + BF16_ROWS, t), F32)],
        compiler_params=params,
        name="fox_attention",
    )(q, k, vt, cf)
    t = T_SB
    o_sb = pl.pallas_call(
        functools.partial(_sb_kernel, t=t),
        grid=(b, nblk, s // t),
        in_specs=specs(t, nblk) + [_resident(tri.shape)],
        out_specs=out_spec(t),
        out_shape=jax.ShapeDtypeStruct((b, s, H_SB * HEAD_DIM), BF16),
        scratch_shapes=[pltpu.VMEM((2, HEADS_PER_BLOCK, t, t), F32),
                        pltpu.VMEM((HEADS_PER_BLOCK, 1, t), F32),
                        pltpu.VMEM((HEADS_PER_BLOCK, HEAD_DIM, t), F32)],
        compiler_params=params,
        name="sb_attention",
    )(q, k, vt, tri)
    return o_fox, o_sb


def _attn_out_kernel(h_ref, of_ref, os_ref, wf_ref, ws_ref, o_ref):
    o_ref[...] = (h_ref[...]
                  + jnp.dot(of_ref[...], wf_ref[...], preferred_element_type=F32)
                  + jnp.dot(os_ref[...], ws_ref[...], preferred_element_type=F32))


def _attn_out(h, o_fox, o_sb, w_fox, w_sb):
    b, s, d = h.shape
    tm = TM_PROJ
    return pl.pallas_call(
        _attn_out_kernel,
        grid=(b, s // tm),
        in_specs=[
            pl.BlockSpec((None, tm, d), lambda bi, i: (bi, i, 0)),
            pl.BlockSpec((None, tm, o_fox.shape[-1]), lambda bi, i: (bi, i, 0)),
            pl.BlockSpec((None, tm, o_sb.shape[-1]), lambda bi, i: (bi, i, 0)),
            _resident(w_fox.shape), _resident(w_sb.shape),
        ],
        out_specs=pl.BlockSpec((None, tm, d), lambda bi, i: (bi, i, 0)),
        out_shape=jax.ShapeDtypeStruct(h.shape, F32),
        compiler_params=_params(("arbitrary", "arbitrary")),
        name="attn_out",
    )(h, o_fox, o_sb, w_fox, w_sb)


def _conv_mixer_kernel(h_ref, g_ref, win_ref, cw_ref, wout_ref, o_ref, ext_ref, tail_ref,
                       *, tm):
    i = pl.program_id(1)

    @pl.when(i == 0)
    def _():
        tail_ref[...] = jnp.zeros_like(tail_ref)

    x = h_ref[...]
    xn = _rms_norm(x, g_ref[...]).astype(BF16)
    d = x.shape[-1]
    gate_c = jnp.dot(xn, win_ref[:, d:2 * d], preferred_element_type=F32)
    u = jnp.dot(xn, win_ref[:, 2 * d:3 * d], preferred_element_type=F32)
    cu = gate_c * u
    y = _causal_conv3(ext_ref, tail_ref[...], cu, cw_ref[...], tm)
    tail_ref[...] = cu[tm - CONV_HALO:tm, :]
    gate_b = jnp.dot(xn, win_ref[:, 0:d], preferred_element_type=F32)
    y = (gate_b * y).astype(BF16)
    o_ref[...] = x + jnp.dot(y, wout_ref[...], preferred_element_type=F32)


def _conv_mixer(h, g, w_in, conv_w, w_out):
    b, s, d = h.shape
    tm = TM_PROJ
    return pl.pallas_call(
        functools.partial(_conv_mixer_kernel, tm=tm),
        grid=(b, s // tm),
        in_specs=[
            pl.BlockSpec((None, tm, d), lambda bi, i: (bi, i, 0)),
            _resident(g.shape), _resident(w_in.shape), _resident(conv_w.shape),
            _resident(w_out.shape),
        ],
        out_specs=pl.BlockSpec((None, tm, d), lambda bi, i: (bi, i, 0)),
        out_shape=jax.ShapeDtypeStruct(h.shape, F32),
        scratch_shapes=[pltpu.VMEM((tm + CONV_HALO, d), F32),
                        pltpu.VMEM((CONV_HALO, d), F32)],
        compiler_params=_params(("arbitrary", "arbitrary")),
        name="conv_mixer",
    )(h, g, w_in, conv_w, w_out)


def _ffn_kernel(h_ref, g_ref, wup_ref, cw_ref, wd_ref, o_ref,
                xn_ref, ext_ref, tail_ref, act_ref, *, tm, fc):
    i = pl.program_id(1)
    n_chunks = D_FF // fc

    @pl.when(i == 0)
    def _():
        tail_ref[...] = jnp.zeros_like(tail_ref)

    x = h_ref[...]
    xn_ref[...] = _rms_norm(x, g_ref[...]).astype(BF16)

    def up_project(c):
        slot = c % 2
        for part in range(2):
            cols = slice(part * D_FF + c * fc, part * D_FF + (c + 1) * fc)
            ext_ref[slot, part, 0:CONV_HALO, :] = tail_ref[part, :, c * fc:(c + 1) * fc]
            ext_ref[slot, part, CONV_HALO:CONV_HALO + tm, :] = jnp.dot(
                xn_ref[...], wup_ref[:, cols], preferred_element_type=F32)

    def conv_gate(c):
        slot = c % 2
        y = []
        for part in range(2):
            cols = slice(part * D_FF + c * fc, part * D_FF + (c + 1) * fc)
            w = cw_ref[:, cols]
            taps = [w[k:k + 1, :] * ext_ref[slot, part, CONV_HALO - 2 + k:CONV_HALO - 2 + k + tm, :]
                    for k in range(3)]
            y.append(taps[2] + taps[1] + taps[0])
            tail_ref[part, :, c * fc:(c + 1) * fc] = ext_ref[slot, part, tm:tm + CONV_HALO, :]
        act_ref[:, c * fc:(c + 1) * fc] = (y[0] / (1.0 + jnp.exp(-y[0])) * y[1]).astype(BF16)

    up_project(0)
    for c in range(n_chunks):
        if c + 1 < n_chunks:
            up_project(c + 1)
        conv_gate(c)
    o_ref[...] = x + jnp.dot(act_ref[...], wd_ref[...], preferred_element_type=F32)


def _ffn(h, g, w_up, conv_w, w_down):
    b, s, d = h.shape
    tm = TM_FFN
    fc = FF_CHUNK
    return pl.pallas_call(
        functools.partial(_ffn_kernel, tm=tm, fc=fc),
        grid=(b, s // tm),
        in_specs=[
            pl.BlockSpec((None, tm, d), lambda bi, i: (bi, i, 0)),
            _resident(g.shape), _resident(w_up.shape), _resident(conv_w.shape),
            _resident(w_down.shape),
        ],
        out_specs=pl.BlockSpec((None, tm, d), lambda bi, i: (bi, i, 0)),
        out_shape=jax.ShapeDtypeStruct(h.shape, F32),
        scratch_shapes=[pltpu.VMEM((tm, d), BF16),
                        pltpu.VMEM((2, 2, tm + CONV_HALO, fc), F32),
                        pltpu.VMEM((2, CONV_HALO, D_FF), F32),
                        pltpu.VMEM((tm, D_FF), BF16)],
        compiler_params=_params(("arbitrary", "arbitrary")),
        name="conv_ffn",
    )(h, g, w_up, conv_w, w_down)


def kernel(x, attn_norm, attn_w_in, attn_f_bias, fox_q_gain, fox_k_gain, sb_q_gain, sb_k_gain,
           attn_w_out, conv_norm, conv_w_in, conv_kernel, conv_w_out, ffn_norm, ffn_w_up,
           ffn_conv, ffn_w_down):
    depth = ffn_norm.shape[0]
    fox_w = H_FOX * HEAD_DIM
    h = x
    for layer in range(depth):
        i = layer // 2
        if layer % 2 == 0:
            w_in = attn_w_in[i]
            w_qk = w_in[:, :2 * MIX_WIDTH].astype(BF16)
            wvt = w_in[:, 2 * MIX_WIDTH:3 * MIX_WIDTH].T.astype(BF16)
            wf = jnp.zeros((D_MODEL, LANES), BF16).at[:, :H_FOX].set(
                w_in[:, 3 * MIX_WIDTH:].astype(BF16))
            fb = jnp.zeros((1, LANES), F32).at[0, :H_FOX].set(attn_f_bias[i])
            qk_gain = jnp.concatenate(
                [jnp.tile(fox_q_gain[i], H_FOX), jnp.tile(sb_q_gain[i], H_SB),
                 jnp.tile(fox_k_gain[i], H_FOX), jnp.tile(sb_k_gain[i], H_SB)])[None, :]
            q, k, vt, cf = _attn_inproj(h, attn_norm[i][None, :], w_qk, wvt, wf, fb, qk_gain)
            o_fox, o_sb = _attention(q, k, vt, cf)
            w_out = attn_w_out[i].astype(BF16)
            h = _attn_out(h, o_fox, o_sb, w_out[:fox_w], w_out[fox_w:])
        else:
            h = _conv_mixer(h, conv_norm[i][None, :], conv_w_in[i].astype(BF16),
                            conv_kernel[i], conv_w_out[i].astype(BF16))
        h = _ffn(h, ffn_norm[layer][None, :], ffn_w_up[layer].astype(BF16), ffn_conv[layer],
                 ffn_w_down[layer].astype(BF16))
    return h
```

```python
import functools

import jax
import jax.numpy as jnp
from jax import lax
from jax.experimental import pallas as pl
from jax.experimental.pallas import tpu as pltpu

F32 = jnp.float32
BF16 = jnp.bfloat16

D_MODEL = 1024
HEAD_DIM = 64
H_FOX = 8
H_SB = 8
MIX_WIDTH = (H_FOX + H_SB) * HEAD_DIM
D_FF = 2816
EPS = 1e-6
QK_SCALE = HEAD_DIM ** -0.5
LOG2E = 1.4426950408889634

LANES = 128
SUBLANES = 8
HEADS_PER_BLOCK = LANES // HEAD_DIM
BF16_ROWS = 2 * SUBLANES
MXU_DIM = 256
EXP2_CLAMP = 126.0
VMEM_LIMIT = 56 * 1024 * 1024

TM_PROJ = 256
TM_FFN = 512
FF_CHUNK = 256
T_FOX = 512
T_SB = 256
SB_EXIT = 160.0
FOX_EXIT = 160.0
NORM_SLACK = 1.01
CONV_HALO = SUBLANES


def _params(sem):
    return pltpu.CompilerParams(dimension_semantics=sem, vmem_limit_bytes=VMEM_LIMIT)


def _resident(shape):
    zeros = (0,) * len(shape)
    return pl.BlockSpec(shape, lambda *_: zeros, pipeline_mode=pl.Buffered(1))


def _rms_norm(x, g):
    ms = jnp.mean(x * x, axis=-1, keepdims=True)
    return x * lax.rsqrt(ms + EPS) * g


def _split3(x):
    hi = x.astype(BF16)
    r = x - hi.astype(F32)
    mid = r.astype(BF16)
    lo = (r - mid.astype(F32)).astype(BF16)
    return hi, mid, lo


def _dot_nt(a, b):
    return lax.dot_general(a, b, (((1,), (1,)), ((), ())), preferred_element_type=F32)


def _causal_conv3(ext_ref, tail, cur, w, tm):
    ext_ref[0:CONV_HALO, :] = tail
    ext_ref[CONV_HALO:CONV_HALO + tm, :] = cur
    return (w[2:3, :] * cur
            + w[1:2, :] * ext_ref[CONV_HALO - 1:CONV_HALO - 1 + tm, :]
            + w[0:1, :] * ext_ref[CONV_HALO - 2:CONV_HALO - 2 + tm, :])


def _attn_inproj_kernel(h_ref, g_ref, w_ref, wvt_ref, wf_ref, fb_ref, qkg_ref, gmat_ref,
                        tri_ref, q_ref, k_ref, vt_ref, cf_ref, carry_ref, *, tm):
    i = pl.program_id(1)

    @pl.when(i == 0)
    def _():
        carry_ref[...] = jnp.zeros_like(carry_ref)

    xn = _rms_norm(h_ref[...], g_ref[...]).astype(BF16)
    gmat = gmat_ref[...]

    n_blk = MIX_WIDTH // MXU_DIM
    proj = [jnp.dot(xn, w_ref[:, c * MXU_DIM:(c + 1) * MXU_DIM], preferred_element_type=F32)
            for c in range(2 * n_blk)]
    vt_ref[...] = _dot_nt(wvt_ref[...], xn).astype(BF16)
    fl = jnp.dot(xn, wf_ref[...], preferred_element_type=F32) + fb_ref[...]

    for c in range(2 * n_blk):
        out_ref, scale = (q_ref, QK_SCALE * LOG2E) if c < n_blk else (k_ref, 1.0)
        t = proj[c]
        ssq = jnp.dot((t * t).astype(BF16), gmat, preferred_element_type=F32)
        gain = qkg_ref[:, c * MXU_DIM:(c + 1) * MXU_DIM]
        tn = t * lax.rsqrt(ssq * (1.0 / HEAD_DIM) + EPS) * gain
        if scale != 1.0:
            tn = tn * scale
        lo = (c % n_blk) * MXU_DIM
        out_ref[:, lo:lo + MXU_DIM] = tn.astype(BF16)

    log_f = jnp.minimum(fl, 0.0) - jnp.log1p(jnp.exp(-jnp.abs(fl)))
    tri = tri_ref[...]
    hi, mid, lo = _split3(log_f)
    cs = (jnp.dot(tri, hi, preferred_element_type=F32)
          + jnp.dot(tri, mid, preferred_element_type=F32)
          + jnp.dot(tri, lo, preferred_element_type=F32))
    cf = cs + carry_ref[0:1, :]
    carry_ref[...] = jnp.broadcast_to(cf[tm - 1:tm, :], carry_ref.shape)
    cf_ref[...] = cf


def _attn_inproj(h, g, w_qk, wvt, wf, fb, qk_gain):
    b, s, d = h.shape
    tm = TM_PROJ
    r = lax.broadcasted_iota(jnp.int32, (MXU_DIM, MXU_DIM), 0) // HEAD_DIM
    c = lax.broadcasted_iota(jnp.int32, (MXU_DIM, MXU_DIM), 1) // HEAD_DIM
    gmat = (r == c).astype(BF16)
    rr = lax.broadcasted_iota(jnp.int32, (tm, tm), 0)
    cc = lax.broadcasted_iota(jnp.int32, (tm, tm), 1)
    tri = (cc <= rr).astype(BF16)
    row_spec = pl.BlockSpec((None, tm, MIX_WIDTH), lambda bi, i: (bi, i, 0))
    return pl.pallas_call(
        functools.partial(_attn_inproj_kernel, tm=tm),
        grid=(b, s // tm),
        in_specs=[
            pl.BlockSpec((None, tm, d), lambda bi, i: (bi, i, 0)),
            _resident(g.shape), _resident(w_qk.shape), _resident(wvt.shape),
            _resident(wf.shape), _resident(fb.shape), _resident(qk_gain.shape),
            _resident(gmat.shape), _resident(tri.shape),
        ],
        out_specs=[row_spec, row_spec,
                   pl.BlockSpec((None, MIX_WIDTH, tm), lambda bi, i: (bi, 0, i)),
                   pl.BlockSpec((None, tm, LANES), lambda bi, i: (bi, i, 0))],
        out_shape=[jax.ShapeDtypeStruct((b, s, MIX_WIDTH), BF16),
                   jax.ShapeDtypeStruct((b, s, MIX_WIDTH), BF16),
                   jax.ShapeDtypeStruct((b, MIX_WIDTH, s), BF16),
                   jax.ShapeDtypeStruct((b, s, LANES), F32)],
        scratch_shapes=[pltpu.VMEM((SUBLANES, LANES), F32)],
        compiler_params=_params(("arbitrary", "arbitrary")),
        name="attn_inproj",
    )(h, g, w_qk, wvt, wf, fb, qk_gain, gmat, tri)


def _head_masks(t):
    lane = lax.broadcasted_iota(jnp.int32, (t, LANES), 1)
    return [(lane >= HEAD_DIM * hh) & (lane < HEAD_DIM * (hh + 1))
            for hh in range(HEADS_PER_BLOCK)]


def _lane_pieces(t):
    return [slice(c * LANES, (c + 1) * LANES) for c in range(t // LANES)]


def _sweep_key_tiles(i, scores, step, more_needed, test_every_tile):
    scores(i, 0)
    step(i, 0, True, (jnp.maximum(i - 1, 0), 1))

    def unfinished(carry):
        r, go = carry
        return jnp.logical_and(r < i // 2, go > 0)

    def pair(carry):
        r, _ = carry
        j = i - 1 - 2 * r
        step(j, 1, False, (jnp.maximum(j - 1, 0), 0))
        if not test_every_tile:
            step(j - 1, 0, False, (jnp.maximum(j - 2, 0), 1))
            return r + 1, more_needed(jnp.maximum(j - 2, 0))
        go_mid = more_needed(j - 1)

        @pl.when(go_mid > 0)
        def _():
            step(j - 1, 0, False, (jnp.maximum(j - 2, 0), 1))

        return r + 1, go_mid & more_needed(jnp.maximum(j - 2, 0))

    _, go = lax.while_loop(unfinished, pair,
                           (jnp.int32(0), more_needed(jnp.maximum(i - 1, 0))))

    @pl.when(jnp.logical_and(i % 2 == 1, go > 0))
    def _():
        step(0, 1, False, None)


def _write_heads(o_ref, outs_t):
    o_ref[...] = jnp.concatenate(outs_t, axis=0).T.astype(o_ref.dtype)


def _fox_parts(q_ref, k_ref, vt_ref, cfb_ref, st_ref, m_ref, acc_ref, t):
    q = q_ref[...]
    qh = [jnp.where(mk, q, jnp.zeros_like(q)) for mk in _head_masks(t)]
    m_ref[...] = jnp.full_like(m_ref, -jnp.inf)
    acc_ref[...] = jnp.zeros_like(acc_ref)
    key = lax.broadcasted_iota(jnp.int32, (t, LANES), 0)
    qry = lax.broadcasted_iota(jnp.int32, (t, LANES), 1)
    pieces = _lane_pieces(t)
    ones_rows = jnp.ones((BF16_ROWS, t), BF16)

    def scores(j, slot, hh):
        off = pl.multiple_of(j * t, t)
        st = _dot_nt(k_ref[pl.ds(off, t), :], qh[hh])
        fcol = cfb_ref[hh, pl.ds(off, t), :]
        for cols in pieces:
            st_ref[slot, hh, :, cols] = st[:, cols] - fcol

    def consume(j, slot, masked, hh):
        off = pl.multiple_of(j * t, t)
        m_old = m_ref[hh]
        m_new, alpha, prob = [], [], []
        for c, cols in enumerate(pieces):
            x = st_ref[slot, hh, :, cols]
            if masked:
                x = jnp.where(key <= qry + c * LANES, x, -jnp.inf)
            m_o = m_old[:, cols]
            m_n = jnp.maximum(m_o, jnp.max(x, axis=0, keepdims=True))
            m_new.append(m_n)
            alpha.append(jnp.exp2(m_o - m_n))
            prob.append(jnp.exp2(x - m_n).astype(BF16))
        vt_h = jnp.concatenate(
            [vt_ref[hh * HEAD_DIM:(hh + 1) * HEAD_DIM, pl.ds(off, t)], ones_rows], axis=0)
        pv = jnp.dot(vt_h, jnp.concatenate(prob, axis=1), preferred_element_type=F32)
        acc_ref[hh] = jnp.concatenate(alpha, axis=1) * acc_ref[hh] + pv
        m_ref[hh] = jnp.concatenate(m_new, axis=1)

    def result():
        return [acc_ref[hh, 0:HEAD_DIM, :] / acc_ref[hh, HEAD_DIM:HEAD_DIM + 1, :]
                for hh in range(HEADS_PER_BLOCK)]

    return scores, consume, result


def _sb_parts(q_ref, k_ref, vt_ref, tri_ref, z_ref, r_ref, acc_ref, t):
    q = q_ref[...]
    qh = [jnp.where(mk, q, jnp.zeros_like(q)) for mk in _head_masks(t)]
    r_ref[...] = jnp.zeros_like(r_ref)
    acc_ref[...] = jnp.zeros_like(acc_ref)
    key = lax.broadcasted_iota(jnp.int32, (t, t), 0)
    qry = lax.broadcasted_iota(jnp.int32, (t, t), 1)
    strict = key < qry
    n_sub = t // MXU_DIM

    sub_rows = [slice(c * MXU_DIM, (c + 1) * MXU_DIM) for c in range(n_sub)]

    def scores(j, slot, hh):
        off = pl.multiple_of(j * t, t)
        z_ref[slot, hh] = _dot_nt(k_ref[pl.ds(off, t), :], qh[hh])

    def gate_sums(slot, masked, hh):
        z = z_ref[slot, hh]
        a = jnp.maximum(LOG2E * jnp.log(1.0 + jnp.exp2(jnp.minimum(z, EXP2_CLAMP))), z)
        base = z - a
        if masked:
            a = jnp.where(strict, a, 0.0)
            base = jnp.where(strict, base, -jnp.inf)
        tri = tri_ref[...]
        sums = [jnp.dot(tri, a[rows, :].astype(BF16), preferred_element_type=F32)
                for rows in sub_rows]
        return base, sums

    def accumulate(j, staged, hh):
        off = pl.multiple_of(j * t, t)
        base, sums = staged
        r_run = r_ref[hh]
        w = [None] * n_sub
        for c in reversed(range(n_sub)):
            later = sums[c][0:MXU_DIM, :]
            w[c] = jnp.exp2(base[sub_rows[c], :] - later - r_run).astype(BF16)
            r_run = r_run + sums[c][MXU_DIM:MXU_DIM + 1, :]
        vt_h = vt_ref[hh * HEAD_DIM:(hh + 1) * HEAD_DIM, pl.ds(off, t)]
        acc_ref[hh] += jnp.dot(vt_h, jnp.concatenate(w, axis=0), preferred_element_type=F32)
        r_ref[hh] = r_run

    def result():
        return [acc_ref[hh] for hh in range(HEADS_PER_BLOCK)]

    return scores, gate_sums, accumulate, result


def _fox_kernel(q_ref, k_ref, vt_ref, cf_ref, o_ref, cfb_ref, kmax_ref, st_ref, m_ref, acc_ref,
                *, t):
    p = pl.program_id(1)
    i = pl.program_id(2)
    heads = range(HEADS_PER_BLOCK)
    lane_row = lax.broadcasted_iota(jnp.int32, (LANES, LANES), 0)
    head_rows = [((lane_row >= HEAD_DIM * hh) & (lane_row < HEAD_DIM * (hh + 1))).astype(BF16)
                 for hh in heads]

    @pl.when(i == 0)
    def _():
        for hh in heads:
            sel = (lane_row == p * HEADS_PER_BLOCK + hh).astype(BF16)

            def fill(r, kmax2):
                off = pl.multiple_of(r * t, t)
                hi, mid, lo = _split3(cf_ref[pl.ds(off, t), :])
                cfb_ref[hh, pl.ds(off, t), :] = LOG2E * (
                    jnp.dot(hi, sel, preferred_element_type=F32)
                    + jnp.dot(mid, sel, preferred_element_type=F32)
                    + jnp.dot(lo, sel, preferred_element_type=F32))
                kt = k_ref[pl.ds(off, t), :].astype(F32)
                ksq = jnp.dot((kt * kt).astype(BF16), head_rows[hh],
                              preferred_element_type=F32)
                return jnp.maximum(kmax2, ksq)

            kmax2 = lax.fori_loop(0, cf_ref.shape[0] // t, fill, jnp.zeros((t, LANES), F32))
            kmax_ref[hh] = jnp.max(kmax2, axis=0, keepdims=True)

    scores_h, consume_h, result = _fox_parts(q_ref, k_ref, vt_ref, cfb_ref, st_ref, m_ref,
                                             acc_ref, t)

    q32 = q_ref[...].astype(F32)
    ones_lhs = jnp.ones((BF16_ROWS, LANES), BF16)
    qk_bound = []
    for hh, mk in zip(heads, _head_masks(t)):
        qsq = jnp.where(mk, q32 * q32, 0.0).astype(BF16)
        qn2 = _dot_nt(ones_lhs, qsq)[0:1, :]
        qk_bound.append(NORM_SLACK * jnp.sqrt(qn2 * kmax_ref[hh][:, 0:1]))

    def more_needed(j):
        row = (j + 1) * t - 1
        need = jnp.int32(0)
        for hh in heads:
            f_hi = cfb_ref[hh, pl.ds(row, 1), :][:, 0:1]
            gap = jnp.max(qk_bound[hh] - f_hi - m_ref[hh])
            need = need | (gap > -FOX_EXIT).astype(jnp.int32)
        return need

    def scores(j, slot):
        for hh in heads:
            scores_h(j, slot, hh)

    def step(j, slot, masked, nxt):
        for hh in heads:
            if nxt is not None:
                scores_h(*nxt, hh)
            consume_h(j, slot, masked, hh)

    _sweep_key_tiles(i, scores, step, more_needed, test_every_tile=False)
    _write_heads(o_ref, result())


def _sb_kernel(q_ref, k_ref, vt_ref, tri_ref, o_ref, z_ref, r_ref, acc_ref, *, t):
    i = pl.program_id(2)
    scores_h, gate_sums_h, accumulate_h, result = _sb_parts(q_ref, k_ref, vt_ref, tri_ref,
                                                            z_ref, r_ref, acc_ref, t)
    heads = range(HEADS_PER_BLOCK)

    def scores(j, slot):
        for hh in heads:
            scores_h(j, slot, hh)

    def step(j, slot, masked, nxt):
        staged = []
        for hh in heads:
            if nxt is not None:
                scores_h(*nxt, hh)
            staged.append(gate_sums_h(slot, masked, hh))
        for hh in heads:
            accumulate_h(j, staged[hh], hh)

    def more_needed(_):
        return (jnp.min(r_ref[...]) < SB_EXIT).astype(jnp.int32)

    _sweep_key_tiles(i, scores, step, more_needed, test_every_tile=True)
    _write_heads(o_ref, result())


def _attention(q, k, vt, cf):
    b, s, _ = q.shape
    nblk = H_FOX // HEADS_PER_BLOCK
    rr = lax.broadcasted_iota(jnp.int32, (MXU_DIM + BF16_ROWS, MXU_DIM), 0)
    cc = lax.broadcasted_iota(jnp.int32, (MXU_DIM + BF16_ROWS, MXU_DIM), 1)
    tri = ((cc > rr) | (rr >= MXU_DIM)).astype(BF16)

    def specs(t, first):
        return [pl.BlockSpec((None, t, LANES), lambda bi, p, i: (bi, i, p + first)),
                pl.BlockSpec((None, s, LANES), lambda bi, p, i: (bi, 0, p + first)),
                pl.BlockSpec((None, LANES, s), lambda bi, p, i: (bi, p + first, 0))]

    def out_spec(t):
        return pl.BlockSpec((None, t, LANES), lambda bi, p, i: (bi, i, p))

    params = _params(("arbitrary", "arbitrary", "arbitrary"))
    t = T_FOX
    o_fox = pl.pallas_call(
        functools.partial(_fox_kernel, t=t),
        grid=(b, nblk, s // t),
        in_specs=specs(t, 0) + [pl.BlockSpec((None, s, LANES), lambda bi, p, i: (bi, 0, 0),
                                             pipeline_mode=pl.Buffered(1))],
        out_specs=out_spec(t),
        out_shape=jax.ShapeDtypeStruct((b, s, H_FOX * HEAD_DIM), BF16),
        scratch_shapes=[pltpu.VMEM((HEADS_PER_BLOCK, s, LANES), F32),
                        pltpu.VMEM((HEADS_PER_BLOCK, 1, LANES), F32),
                        pltpu.VMEM((2, HEADS_PER_BLOCK, t, t), F32),
                        pltpu.VMEM((HEADS_PER_BLOCK, 1, t), F32),
                        pltpu.VMEM((HEADS_PER_BLOCK, HEAD_DIM + BF16_ROWS, t), F32)],
        compiler_params=params,
        name="fox_attention",
    )(q, k, vt, cf)
    t = T_SB
    o_sb = pl.pallas_call(
        functools.partial(_sb_kernel, t=t),
        grid=(b, nblk, s // t),
        in_specs=specs(t, nblk) + [_resident(tri.shape)],
        out_specs=out_spec(t),
        out_shape=jax.ShapeDtypeStruct((b, s, H_SB * HEAD_DIM), BF16),
        scratch_shapes=[pltpu.VMEM((2, HEADS_PER_BLOCK, t, t), F32),
                        pltpu.VMEM((HEADS_PER_BLOCK, 1, t), F32),
                        pltpu.VMEM((HEADS_PER_BLOCK, HEAD_DIM, t), F32)],
        compiler_params=params,
        name="sb_attention",
    )(q, k, vt, tri)
    return o_fox, o_sb


def _conv_mixer_kernel(h_ref, g_ref, win_ref, cw_ref, o_ref, ext_ref, tail_ref, *, tm):
    i = pl.program_id(1)

    @pl.when(i == 0)
    def _():
        tail_ref[...] = jnp.zeros_like(tail_ref)

    x = h_ref[...]
    xn = _rms_norm(x, g_ref[...]).astype(BF16)
    d = x.shape[-1]
    gate_c = jnp.dot(xn, win_ref[:, d:2 * d], preferred_element_type=F32)
    u = jnp.dot(xn, win_ref[:, 2 * d:3 * d], preferred_element_type=F32)
    cu = gate_c * u
    y = _causal_conv3(ext_ref, tail_ref[...], cu, cw_ref[...], tm)
    tail_ref[...] = cu[tm - CONV_HALO:tm, :]
    gate_b = jnp.dot(xn, win_ref[:, 0:d], preferred_element_type=F32)
    o_ref[...] = (gate_b * y).astype(o_ref.dtype)


def _conv_mixer(h, g, w_in, conv_w):
    b, s, d = h.shape
    tm = TM_PROJ
    return pl.pallas_call(
        functools.partial(_conv_mixer_kernel, tm=tm),
        grid=(b, s // tm),
        in_specs=[
            pl.BlockSpec((None, tm, d), lambda bi, i: (bi, i, 0)),
            _resident(g.shape), _resident(w_in.shape), _resident(conv_w.shape),
        ],
        out_specs=pl.BlockSpec((None, tm, d), lambda bi, i: (bi, i, 0)),
        out_shape=jax.ShapeDtypeStruct(h.shape, BF16),
        scratch_shapes=[pltpu.VMEM((tm + CONV_HALO, d), F32),
                        pltpu.VMEM((CONV_HALO, d), F32)],
        compiler_params=_params(("arbitrary", "arbitrary")),
        name="conv_mixer",
    )(h, g, w_in, conv_w)


def _ffn_kernel(h_ref, a0_ref, a1_ref, w0_ref, w1_ref, g_ref, wup_ref, cw_ref, wd_ref, o_ref,
                xn_ref, ext_ref, tail_ref, act_ref, *, tm, fc):
    i = pl.program_id(1)
    n_chunks = D_FF // fc

    @pl.when(i == 0)
    def _():
        tail_ref[...] = jnp.zeros_like(tail_ref)

    x = (h_ref[...]
         + jnp.dot(a0_ref[...], w0_ref[...], preferred_element_type=F32)
         + jnp.dot(a1_ref[...], w1_ref[...], preferred_element_type=F32))
    xn_ref[...] = _rms_norm(x, g_ref[...]).astype(BF16)
    o_ref[...] = x

    def up_project(c):
        slot = c % 2
        for part in range(2):
            cols = slice(part * D_FF + c * fc, part * D_FF + (c + 1) * fc)
            ext_ref[slot, part, 0:CONV_HALO, :] = tail_ref[part, :, c * fc:(c + 1) * fc]
            ext_ref[slot, part, CONV_HALO:CONV_HALO + tm, :] = jnp.dot(
                xn_ref[...], wup_ref[:, cols], preferred_element_type=F32)

    def conv_gate(c):
        slot = c % 2
        y = []
        for part in range(2):
            cols = slice(part * D_FF + c * fc, part * D_FF + (c + 1) * fc)
            w = cw_ref[:, cols]
            taps = [w[k:k + 1, :] * ext_ref[slot, part, CONV_HALO - 2 + k:CONV_HALO - 2 + k + tm, :]
                    for k in range(3)]
            y.append(taps[2] + taps[1] + taps[0])
            tail_ref[part, :, c * fc:(c + 1) * fc] = ext_ref[slot, part, tm:tm + CONV_HALO, :]
        act_ref[:, c * fc:(c + 1) * fc] = (y[0] / (1.0 + jnp.exp(-y[0])) * y[1]).astype(BF16)

    up_project(0)
    for c in range(n_chunks):
        if c + 1 < n_chunks:
            up_project(c + 1)
        conv_gate(c)
    o_ref[...] += jnp.dot(act_ref[...], wd_ref[...], preferred_element_type=F32)


def _mixer_out_ffn(h, mix, w_mix, g, w_up, conv_w, w_down):
    b, s, d = h.shape
    tm = TM_FFN
    fc = FF_CHUNK
    half = d // 2
    (a0, blk0), (a1, blk1) = mix
    w0, w1 = w_mix[:half], w_mix[half:]

    def half_spec(blk):
        return pl.BlockSpec((None, tm, half), lambda bi, i: (bi, i, blk))

    return pl.pallas_call(
        functools.partial(_ffn_kernel, tm=tm, fc=fc),
        grid=(b, s // tm),
        in_specs=[
            pl.BlockSpec((None, tm, d), lambda bi, i: (bi, i, 0)),
            half_spec(blk0), half_spec(blk1), _resident(w0.shape), _resident(w1.shape),
            _resident(g.shape), _resident(w_up.shape), _resident(conv_w.shape),
            _resident(w_down.shape),
        ],
        out_specs=pl.BlockSpec((None, tm, d), lambda bi, i: (bi, i, 0)),
        out_shape=jax.ShapeDtypeStruct(h.shape, F32),
        scratch_shapes=[pltpu.VMEM((tm, d), BF16),
                        pltpu.VMEM((2, 2, tm + CONV_HALO, fc), F32),
                        pltpu.VMEM((2, CONV_HALO, D_FF), F32),
                        pltpu.VMEM((tm, D_FF), BF16)],
        compiler_params=_params(("arbitrary", "arbitrary")),
        name="conv_ffn",
    )(h, a0, a1, w0, w1, g, w_up, conv_w, w_down)


def kernel(x, attn_norm, attn_w_in, attn_f_bias, fox_q_gain, fox_k_gain, sb_q_gain, sb_k_gain,
           attn_w_out, conv_norm, conv_w_in, conv_kernel, conv_w_out, ffn_norm, ffn_w_up,
           ffn_conv, ffn_w_down):
    depth = ffn_norm.shape[0]
    h = x
    for layer in range(depth):
        i = layer // 2
        if layer % 2 == 0:
            w_in = attn_w_in[i]
            w_qk = w_in[:, :2 * MIX_WIDTH].astype(BF16)
            wvt = w_in[:, 2 * MIX_WIDTH:3 * MIX_WIDTH].T.astype(BF16)
            wf = jnp.zeros((D_MODEL, LANES), BF16).at[:, :H_FOX].set(
                w_in[:, 3 * MIX_WIDTH:].astype(BF16))
            fb = jnp.zeros((1, LANES), F32).at[0, :H_FOX].set(attn_f_bias[i])
            qk_gain = jnp.concatenate(
                [jnp.tile(fox_q_gain[i], H_FOX), jnp.tile(sb_q_gain[i], H_SB),
                 jnp.tile(fox_k_gain[i], H_FOX), jnp.tile(sb_k_gain[i], H_SB)])[None, :]
            q, k, vt, cf = _attn_inproj(h, attn_norm[i][None, :], w_qk, wvt, wf, fb, qk_gain)
            o_fox, o_sb = _attention(q, k, vt, cf)
            mix = ((o_fox, 0), (o_sb, 0))
            w_mix = attn_w_out[i]
        else:
            y = _conv_mixer(h, conv_norm[i][None, :], conv_w_in[i].astype(BF16), conv_kernel[i])
            mix = ((y, 0), (y, 1))
            w_mix = conv_w_out[i]
        h = _mixer_out_ffn(h, mix, w_mix.astype(BF16), ffn_norm[layer][None, :],
                           ffn_w_up[layer].astype(BF16), ffn_conv[layer],
                           ffn_w_down[layer].astype(BF16))
    return h
```

```python
import functools

import jax
import jax.numpy as jnp
from jax import lax
from jax.experimental import pallas as pl
from jax.experimental.pallas import tpu as pltpu

F32 = jnp.float32
BF16 = jnp.bfloat16

D_MODEL = 1024
HEAD_DIM = 64
H_FOX = 8
H_SB = 8
MIX_WIDTH = (H_FOX + H_SB) * HEAD_DIM
D_FF = 2816
EPS = 1e-6
QK_SCALE = HEAD_DIM ** -0.5
LOG2E = 1.4426950408889634

LANES = 128
SUBLANES = 8
HEADS_PER_BLOCK = LANES // HEAD_DIM
BF16_ROWS = 2 * SUBLANES
MXU_DIM = 256
EXP2_CLAMP = 126.0
VMEM_LIMIT = 56 * 1024 * 1024

TM_PROJ = 256
TM_FFN = 512
FF_CHUNK = 256
T_FOX = 512
T_SB = 256
SB_EXIT = 160.0
FOX_EXIT = 160.0
NORM_SLACK = 1.01
CONV_HALO = SUBLANES


def _params(sem):
    return pltpu.CompilerParams(dimension_semantics=sem, vmem_limit_bytes=VMEM_LIMIT)


def _resident(shape):
    zeros = (0,) * len(shape)
    return pl.BlockSpec(shape, lambda *_: zeros, pipeline_mode=pl.Buffered(1))


def _rms_norm(x, g):
    ms = jnp.mean(x * x, axis=-1, keepdims=True)
    return x * lax.rsqrt(ms + EPS) * g


def _split3(x):
    hi = x.astype(BF16)
    r = x - hi.astype(F32)
    mid = r.astype(BF16)
    lo = (r - mid.astype(F32)).astype(BF16)
    return hi, mid, lo


def _dot_nt(a, b):
    return lax.dot_general(a, b, (((1,), (1,)), ((), ())), preferred_element_type=F32)


def _causal_conv3(ext_ref, tail, cur, w, tm):
    ext_ref[0:CONV_HALO, :] = tail
    ext_ref[CONV_HALO:CONV_HALO + tm, :] = cur
    return (w[2:3, :] * cur
            + w[1:2, :] * ext_ref[CONV_HALO - 1:CONV_HALO - 1 + tm, :]
            + w[0:1, :] * ext_ref[CONV_HALO - 2:CONV_HALO - 2 + tm, :])


def _attn_inproj_kernel(h_ref, g_ref, w_ref, wvt_ref, wf_ref, fb_ref, qkg_ref, gmat_ref,
                        tri_ref, q_ref, k_ref, vt_ref, cf_ref, carry_ref, *, tm):
    i = pl.program_id(1)

    @pl.when(i == 0)
    def _():
        carry_ref[...] = jnp.zeros_like(carry_ref)

    xn = _rms_norm(h_ref[...], g_ref[...]).astype(BF16)
    gmat = gmat_ref[...]

    n_blk = MIX_WIDTH // MXU_DIM
    proj = [jnp.dot(xn, w_ref[:, c * MXU_DIM:(c + 1) * MXU_DIM], preferred_element_type=F32)
            for c in range(2 * n_blk)]
    vt_ref[...] = _dot_nt(wvt_ref[...], xn).astype(BF16)
    fl = jnp.dot(xn, wf_ref[...], preferred_element_type=F32) + fb_ref[...]

    for c in range(2 * n_blk):
        out_ref, scale = (q_ref, QK_SCALE * LOG2E) if c < n_blk else (k_ref, 1.0)
        t = proj[c]
        ssq = jnp.dot((t * t).astype(BF16), gmat, preferred_element_type=F32)
        gain = qkg_ref[:, c * MXU_DIM:(c + 1) * MXU_DIM]
        tn = t * lax.rsqrt(ssq * (1.0 / HEAD_DIM) + EPS) * gain
        if scale != 1.0:
            tn = tn * scale
        lo = (c % n_blk) * MXU_DIM
        out_ref[:, lo:lo + MXU_DIM] = tn.astype(BF16)

    log_f = jnp.minimum(fl, 0.0) - jnp.log1p(jnp.exp(-jnp.abs(fl)))
    tri = tri_ref[...]
    hi, mid, lo = _split3(log_f)
    cs = (jnp.dot(tri, hi, preferred_element_type=F32)
          + jnp.dot(tri, mid, preferred_element_type=F32)
          + jnp.dot(tri, lo, preferred_element_type=F32))
    cf = cs + carry_ref[0:1, :]
    carry_ref[...] = jnp.broadcast_to(cf[tm - 1:tm, :], carry_ref.shape)
    cf_ref[...] = cf


def _attn_inproj(h, g, w_qk, wvt, wf, fb, qk_gain):
    b, s, d = h.shape
    tm = TM_PROJ
    r = lax.broadcasted_iota(jnp.int32, (MXU_DIM, MXU_DIM), 0) // HEAD_DIM
    c = lax.broadcasted_iota(jnp.int32, (MXU_DIM, MXU_DIM), 1) // HEAD_DIM
    gmat = (r == c).astype(BF16)
    rr = lax.broadcasted_iota(jnp.int32, (tm, tm), 0)
    cc = lax.broadcasted_iota(jnp.int32, (tm, tm), 1)
    tri = (cc <= rr).astype(BF16)
    row_spec = pl.BlockSpec((None, tm, MIX_WIDTH), lambda bi, i: (bi, i, 0))
    return pl.pallas_call(
        functools.partial(_attn_inproj_kernel, tm=tm),
        grid=(b, s // tm),
        in_specs=[
            pl.BlockSpec((None, tm, d), lambda bi, i: (bi, i, 0)),
            _resident(g.shape), _resident(w_qk.shape), _resident(wvt.shape),
            _resident(wf.shape), _resident(fb.shape), _resident(qk_gain.shape),
            _resident(gmat.shape), _resident(tri.shape),
        ],
        out_specs=[row_spec, row_spec,
                   pl.BlockSpec((None, MIX_WIDTH, tm), lambda bi, i: (bi, 0, i)),
                   pl.BlockSpec((None, tm, LANES), lambda bi, i: (bi, i, 0))],
        out_shape=[jax.ShapeDtypeStruct((b, s, MIX_WIDTH), BF16),
                   jax.ShapeDtypeStruct((b, s, MIX_WIDTH), BF16),
                   jax.ShapeDtypeStruct((b, MIX_WIDTH, s), BF16),
                   jax.ShapeDtypeStruct((b, s, LANES), F32)],
        scratch_shapes=[pltpu.VMEM((SUBLANES, LANES), F32)],
        compiler_params=_params(("arbitrary", "arbitrary")),
        name="attn_inproj",
    )(h, g, w_qk, wvt, wf, fb, qk_gain, gmat, tri)


def _head_masks(t):
    lane = lax.broadcasted_iota(jnp.int32, (t, LANES), 1)
    return [(lane >= HEAD_DIM * hh) & (lane < HEAD_DIM * (hh + 1))
            for hh in range(HEADS_PER_BLOCK)]


def _lane_pieces(t):
    return [slice(c * LANES, (c + 1) * LANES) for c in range(t // LANES)]


def _sweep_key_tiles(i, scores, step, more_needed, test_every_tile):
    scores(i, 0)
    step(i, 0, True, (jnp.maximum(i - 1, 0), 1))

    def unfinished(carry):
        r, go = carry
        return jnp.logical_and(r < i // 2, go > 0)

    def pair(carry):
        r, _ = carry
        j = i - 1 - 2 * r
        step(j, 1, False, (jnp.maximum(j - 1, 0), 0))
        if not test_every_tile:
            step(j - 1, 0, False, (jnp.maximum(j - 2, 0), 1))
            return r + 1, more_needed(jnp.maximum(j - 2, 0))
        go_mid = more_needed(j - 1)

        @pl.when(go_mid > 0)
        def _():
            step(j - 1, 0, False, (jnp.maximum(j - 2, 0), 1))

        return r + 1, go_mid & more_needed(jnp.maximum(j - 2, 0))

    _, go = lax.while_loop(unfinished, pair,
                           (jnp.int32(0), more_needed(jnp.maximum(i - 1, 0))))

    @pl.when(jnp.logical_and(i % 2 == 1, go > 0))
    def _():
        step(0, 1, False, None)


def _write_heads(o_ref, outs_t):
    o_ref[...] = jnp.concatenate(outs_t, axis=0).T.astype(o_ref.dtype)


def _fox_parts(q_ref, k_ref, vt_ref, cfb_ref, st_ref, m_ref, acc_ref, t):
    q = q_ref[...]
    qh = [jnp.where(mk, q, jnp.zeros_like(q)) for mk in _head_masks(t)]
    m_ref[...] = jnp.full_like(m_ref, -jnp.inf)
    acc_ref[...] = jnp.zeros_like(acc_ref)
    key = lax.broadcasted_iota(jnp.int32, (t, LANES), 0)
    qry = lax.broadcasted_iota(jnp.int32, (t, LANES), 1)
    pieces = _lane_pieces(t)
    ones_rows = jnp.ones((BF16_ROWS, t), BF16)

    def scores(j, slot, hh):
        off = pl.multiple_of(j * t, t)
        st = _dot_nt(k_ref[pl.ds(off, t), :], qh[hh])
        fcol = cfb_ref[hh, pl.ds(off, t), :]
        for cols in pieces:
            st_ref[slot, hh, :, cols] = st[:, cols] - fcol

    def consume(j, slot, masked, hh):
        off = pl.multiple_of(j * t, t)
        m_old = m_ref[hh]
        m_new, alpha, prob = [], [], []
        for c, cols in enumerate(pieces):
            x = st_ref[slot, hh, :, cols]
            if masked:
                x = jnp.where(key <= qry + c * LANES, x, -jnp.inf)
            m_o = m_old[:, cols]
            m_n = jnp.maximum(m_o, jnp.max(x, axis=0, keepdims=True))
            m_new.append(m_n)
            alpha.append(jnp.exp2(m_o - m_n))
            prob.append(jnp.exp2(x - m_n).astype(BF16))
        vt_h = jnp.concatenate(
            [vt_ref[hh * HEAD_DIM:(hh + 1) * HEAD_DIM, pl.ds(off, t)], ones_rows], axis=0)
        pv = jnp.dot(vt_h, jnp.concatenate(prob, axis=1), preferred_element_type=F32)
        acc_ref[hh] = jnp.concatenate(alpha, axis=1) * acc_ref[hh] + pv
        m_ref[hh] = jnp.concatenate(m_new, axis=1)

    def result():
        return [acc_ref[hh, 0:HEAD_DIM, :] / acc_ref[hh, HEAD_DIM:HEAD_DIM + 1, :]
                for hh in range(HEADS_PER_BLOCK)]

    return scores, consume, result


def _sb_parts(q_ref, k_ref, vt_ref, tri_ref, z_ref, r_ref, acc_ref, t):
    q = q_ref[...]
    qh = [jnp.where(mk, q, jnp.zeros_like(q)) for mk in _head_masks(t)]
    r_ref[...] = jnp.zeros_like(r_ref)
    acc_ref[...] = jnp.zeros_like(acc_ref)
    key = lax.broadcasted_iota(jnp.int32, (t, t), 0)
    qry = lax.broadcasted_iota(jnp.int32, (t, t), 1)
    strict = key < qry
    n_sub = t // MXU_DIM

    sub_rows = [slice(c * MXU_DIM, (c + 1) * MXU_DIM) for c in range(n_sub)]

    def scores(j, slot, hh):
        off = pl.multiple_of(j * t, t)
        z_ref[slot, hh] = _dot_nt(k_ref[pl.ds(off, t), :], qh[hh])

    def gate_sums(slot, masked, hh):
        z = z_ref[slot, hh]
        a = jnp.maximum(LOG2E * jnp.log(1.0 + jnp.exp2(jnp.minimum(z, EXP2_CLAMP))), z)
        base = z - a
        if masked:
            a = jnp.where(strict, a, 0.0)
            base = jnp.where(strict, base, -jnp.inf)
        tri = tri_ref[...]
        sums = [jnp.dot(tri, a[rows, :].astype(BF16), preferred_element_type=F32)
                for rows in sub_rows]
        return base, sums

    def accumulate(j, staged, hh):
        off = pl.multiple_of(j * t, t)
        base, sums = staged
        r_run = r_ref[hh]
        w = [None] * n_sub
        for c in reversed(range(n_sub)):
            later = sums[c][0:MXU_DIM, :]
            w[c] = jnp.exp2(base[sub_rows[c], :] - later - r_run).astype(BF16)
            r_run = r_run + sums[c][MXU_DIM:MXU_DIM + 1, :]
        vt_h = vt_ref[hh * HEAD_DIM:(hh + 1) * HEAD_DIM, pl.ds(off, t)]
        acc_ref[hh] += jnp.dot(vt_h, jnp.concatenate(w, axis=0), preferred_element_type=F32)
        r_ref[hh] = r_run

    def result():
        return [acc_ref[hh] for hh in range(HEADS_PER_BLOCK)]

    return scores, gate_sums, accumulate, result


def _fox_kernel(q_ref, k_ref, vt_ref, cf_ref, o_ref, cfb_ref, kmax_ref, st_ref, m_ref, acc_ref,
                *, t):
    p = pl.program_id(1)
    i = pl.program_id(2)
    heads = range(HEADS_PER_BLOCK)
    lane_row = lax.broadcasted_iota(jnp.int32, (LANES, LANES), 0)
    head_rows = [((lane_row >= HEAD_DIM * hh) & (lane_row < HEAD_DIM * (hh + 1))).astype(BF16)
                 for hh in heads]

    @pl.when(i == 0)
    def _():
        for hh in heads:
            sel = (lane_row == p * HEADS_PER_BLOCK + hh).astype(BF16)

            def fill(r, kmax2):
                off = pl.multiple_of(r * t, t)
                hi, mid, lo = _split3(cf_ref[pl.ds(off, t), :])
                cfb_ref[hh, pl.ds(off, t), :] = LOG2E * (
                    jnp.dot(hi, sel, preferred_element_type=F32)
                    + jnp.dot(mid, sel, preferred_element_type=F32)
                    + jnp.dot(lo, sel, preferred_element_type=F32))
                kt = k_ref[pl.ds(off, t), :].astype(F32)
                ksq = jnp.dot((kt * kt).astype(BF16), head_rows[hh],
                              preferred_element_type=F32)
                return jnp.maximum(kmax2, ksq)

            kmax2 = lax.fori_loop(0, cf_ref.shape[0] // t, fill, jnp.zeros((t, LANES), F32))
            kmax_ref[hh] = jnp.max(kmax2, axis=0, keepdims=True)

    scores_h, consume_h, result = _fox_parts(q_ref, k_ref, vt_ref, cfb_ref, st_ref, m_ref,
                                             acc_ref, t)

    q32 = q_ref[...].astype(F32)
    ones_lhs = jnp.ones((BF16_ROWS, LANES), BF16)
    qk_bound = []
    for hh, mk in zip(heads, _head_masks(t)):
        qsq = jnp.where(mk, q32 * q32, 0.0).astype(BF16)
        qn2 = _dot_nt(ones_lhs, qsq)[0:1, :]
        qk_bound.append(NORM_SLACK * jnp.sqrt(qn2 * kmax_ref[hh][:, 0:1]))

    def more_needed(j):
        row = (j + 1) * t - 1
        need = jnp.int32(0)
        for hh in heads:
            f_hi = cfb_ref[hh, pl.ds(row, 1), :][:, 0:1]
            gap = jnp.max(qk_bound[hh] - f_hi - m_ref[hh])
            need = need | (gap > -FOX_EXIT).astype(jnp.int32)
        return need

    def scores(j, slot):
        for hh in heads:
            scores_h(j, slot, hh)

    def step(j, slot, masked, nxt):
        for hh in heads:
            if nxt is not None:
                scores_h(*nxt, hh)
            consume_h(j, slot, masked, hh)

    _sweep_key_tiles(i, scores, step, more_needed, test_every_tile=False)
    _write_heads(o_ref, result())


def _sb_kernel(q_ref, k_ref, vt_ref, tri_ref, o_ref, z_ref, r_ref, acc_ref, *, t):
    i = pl.program_id(2)
    scores_h, gate_sums_h, accumulate_h, result = _sb_parts(q_ref, k_ref, vt_ref, tri_ref,
                                                            z_ref, r_ref, acc_ref, t)
    heads = range(HEADS_PER_BLOCK)

    def scores(j, slot):
        for hh in heads:
            scores_h(j, slot, hh)

    def step(j, slot, masked, nxt):
        staged = []
        for hh in heads:
            if nxt is not None:
                scores_h(*nxt, hh)
            staged.append(gate_sums_h(slot, masked, hh))
        for hh in heads:
            accumulate_h(j, staged[hh], hh)

    def more_needed(_):
        return (jnp.min(r_ref[...]) < SB_EXIT).astype(jnp.int32)

    _sweep_key_tiles(i, scores, step, more_needed, test_every_tile=True)
    _write_heads(o_ref, result())


def _attention(q, k, vt, cf):
    b, s, _ = q.shape
    nblk = H_FOX // HEADS_PER_BLOCK
    rr = lax.broadcasted_iota(jnp.int32, (MXU_DIM + BF16_ROWS, MXU_DIM), 0)
    cc = lax.broadcasted_iota(jnp.int32, (MXU_DIM + BF16_ROWS, MXU_DIM), 1)
    tri = ((cc > rr) | (rr >= MXU_DIM)).astype(BF16)

    def specs(t, first):
        return [pl.BlockSpec((None, t, LANES), lambda bi, p, i: (bi, i, p + first)),
                pl.BlockSpec((None, s, LANES), lambda bi, p, i: (bi, 0, p + first)),
                pl.BlockSpec((None, LANES, s), lambda bi, p, i: (bi, p + first, 0))]

    def out_spec(t):
        return pl.BlockSpec((None, t, LANES), lambda bi, p, i: (bi, i, p))

    params = _params(("arbitrary", "arbitrary", "arbitrary"))
    t = T_FOX
    o_fox = pl.pallas_call(
        functools.partial(_fox_kernel, t=t),
        grid=(b, nblk, s // t),
        in_specs=specs(t, 0) + [pl.BlockSpec((None, s, LANES), lambda bi, p, i: (bi, 0, 0),
                                             pipeline_mode=pl.Buffered(1))],
        out_specs=out_spec(t),
        out_shape=jax.ShapeDtypeStruct((b, s, H_FOX * HEAD_DIM), BF16),
        scratch_shapes=[pltpu.VMEM((HEADS_PER_BLOCK, s, LANES), F32),
                        pltpu.VMEM((HEADS_PER_BLOCK, 1, LANES), F32),
                        pltpu.VMEM((2, HEADS_PER_BLOCK, t, t), F32),
                        pltpu.VMEM((HEADS_PER_BLOCK, 1, t), F32),
                        pltpu.VMEM((HEADS_PER_BLOCK, HEAD_DIM + BF16_ROWS, t), F32)],
        compiler_params=params,
        name="fox_attention",
    )(q, k, vt, cf)
    t = T_SB
    o_sb = pl.pallas_call(
        functools.partial(_sb_kernel, t=t),
        grid=(b, nblk, s // t),
        in_specs=specs(t, nblk) + [_resident(tri.shape)],
        out_specs=out_spec(t),
        out_shape=jax.ShapeDtypeStruct((b, s, H_SB * HEAD_DIM), BF16),
        scratch_shapes=[pltpu.VMEM((2, HEADS_PER_BLOCK, t, t), F32),
                        pltpu.VMEM((HEADS_PER_BLOCK, 1, t), F32),
                        pltpu.VMEM((HEADS_PER_BLOCK, HEAD_DIM, t), F32)],
        compiler_params=params,
        name="sb_attention",
    )(q, k, vt, tri)
    return o_fox, o_sb


def _conv_mixer_kernel(h_ref, g_ref, win_ref, cw_ref, o_ref, ext_ref, tail_ref, *, tm):
    i = pl.program_id(1)

    @pl.when(i == 0)
    def _():
        tail_ref[...] = jnp.zeros_like(tail_ref)

    x = h_ref[...]
    xn = _rms_norm(x, g_ref[...]).astype(BF16)
    d = x.shape[-1]
    gate_c = jnp.dot(xn, win_ref[:, d:2 * d], preferred_element_type=F32)
    u = jnp.dot(xn, win_ref[:, 2 * d:3 * d], preferred_element_type=F32)
    cu = gate_c * u
    y = _causal_conv3(ext_ref, tail_ref[...], cu, cw_ref[...], tm)
    tail_ref[...] = cu[tm - CONV_HALO:tm, :]
    gate_b = jnp.dot(xn, win_ref[:, 0:d], preferred_element_type=F32)
    o_ref[...] = (gate_b * y).astype(o_ref.dtype)


def _conv_mixer(h, g, w_in, conv_w):
    b, s, d = h.shape
    tm = TM_PROJ
    return pl.pallas_call(
        functools.partial(_conv_mixer_kernel, tm=tm),
        grid=(b, s // tm),
        in_specs=[
            pl.BlockSpec((None, tm, d), lambda bi, i: (bi, i, 0)),
            _resident(g.shape), _resident(w_in.shape), _resident(conv_w.shape),
        ],
        out_specs=pl.BlockSpec((None, tm, d), lambda bi, i: (bi, i, 0)),
        out_shape=jax.ShapeDtypeStruct(h.shape, BF16),
        scratch_shapes=[pltpu.VMEM((tm + CONV_HALO, d), F32),
                        pltpu.VMEM((CONV_HALO, d), F32)],
        compiler_params=_params(("arbitrary", "arbitrary")),
        name="conv_mixer",
    )(h, g, w_in, conv_w)


def _ffn_kernel(h_ref, a0_ref, a1_ref, w0_ref, w1_ref, g_ref, wup_ref, cw_ref, wd_ref, o_ref,
                xn_ref, ext_ref, tail_ref, act_ref, *, tm, fc):
    i = pl.program_id(1)
    n_chunks = D_FF // fc

    @pl.when(i == 0)
    def _():
        tail_ref[...] = jnp.zeros_like(tail_ref)

    x = (h_ref[...]
         + jnp.dot(a0_ref[...], w0_ref[...], preferred_element_type=F32)
         + jnp.dot(a1_ref[...], w1_ref[...], preferred_element_type=F32))
    xn_ref[...] = _rms_norm(x, g_ref[...]).astype(BF16)
    o_ref[...] = x

    def up_project(c):
        slot = c % 2
        for part in range(2):
            cols = slice(part * D_FF + c * fc, part * D_FF + (c + 1) * fc)
            ext_ref[slot, part, 0:CONV_HALO, :] = tail_ref[part, :, c * fc:(c + 1) * fc]
            ext_ref[slot, part, CONV_HALO:CONV_HALO + tm, :] = jnp.dot(
                xn_ref[...], wup_ref[:, cols], preferred_element_type=F32)

    def conv_gate(c):
        slot = c % 2
        y = []
        for part in range(2):
            cols = slice(part * D_FF + c * fc, part * D_FF + (c + 1) * fc)
            w = cw_ref[:, cols]
            taps = [w[k:k + 1, :] * ext_ref[slot, part, CONV_HALO - 2 + k:CONV_HALO - 2 + k + tm, :]
                    for k in range(3)]
            y.append(taps[2] + taps[1] + taps[0])
            tail_ref[part, :, c * fc:(c + 1) * fc] = ext_ref[slot, part, tm:tm + CONV_HALO, :]
        half_g = 0.5 * y[0]
        act_ref[:, c * fc:(c + 1) * fc] = (
            (half_g + half_g * jnp.tanh(half_g)) * y[1]).astype(BF16)

    def down_project(rows):
        o_ref[...] += jnp.dot(act_ref[:, rows], wd_ref[rows, :], preferred_element_type=F32)

    last = n_chunks - 1
    up_project(0)
    for c in range(last):
        up_project(c + 1)
        conv_gate(c)
    down_project(slice(0, last * fc))
    conv_gate(last)
    down_project(slice(last * fc, D_FF))


def _mixer_out_ffn(h, mix, w_mix, g, w_up, conv_w, w_down):
    b, s, d = h.shape
    tm = TM_FFN
    fc = FF_CHUNK
    half = d // 2
    (a0, blk0), (a1, blk1) = mix
    w0, w1 = w_mix[:half], w_mix[half:]

    def half_spec(blk):
        return pl.BlockSpec((None, tm, half), lambda bi, i: (bi, i, blk))

    return pl.pallas_call(
        functools.partial(_ffn_kernel, tm=tm, fc=fc),
        grid=(b, s // tm),
        in_specs=[
            pl.BlockSpec((None, tm, d), lambda bi, i: (bi, i, 0)),
            half_spec(blk0), half_spec(blk1), _resident(w0.shape), _resident(w1.shape),
            _resident(g.shape), _resident(w_up.shape), _resident(conv_w.shape),
            _resident(w_down.shape),
        ],
        out_specs=pl.BlockSpec((None, tm, d), lambda bi, i: (bi, i, 0)),
        out_shape=jax.ShapeDtypeStruct(h.shape, F32),
        scratch_shapes=[pltpu.VMEM((tm, d), BF16),
                        pltpu.VMEM((2, 2, tm + CONV_HALO, fc), F32),
                        pltpu.VMEM((2, CONV_HALO, D_FF), F32),
                        pltpu.VMEM((tm, D_FF), BF16)],
        compiler_params=_params(("arbitrary", "arbitrary")),
        name="conv_ffn",
    )(h, a0, a1, w0, w1, g, w_up, conv_w, w_down)


def kernel(x, attn_norm, attn_w_in, attn_f_bias, fox_q_gain, fox_k_gain, sb_q_gain, sb_k_gain,
           attn_w_out, conv_norm, conv_w_in, conv_kernel, conv_w_out, ffn_norm, ffn_w_up,
           ffn_conv, ffn_w_down):
    depth = ffn_norm.shape[0]
    h = x
    for layer in range(depth):
        i = layer // 2
        if layer % 2 == 0:
            w_in = attn_w_in[i]
            w_qk = w_in[:, :2 * MIX_WIDTH].astype(BF16)
            wvt = w_in[:, 2 * MIX_WIDTH:3 * MIX_WIDTH].T.astype(BF16)
            wf = jnp.zeros((D_MODEL, LANES), BF16).at[:, :H_FOX].set(
                w_in[:, 3 * MIX_WIDTH:].astype(BF16))
            fb = jnp.zeros((1, LANES), F32).at[0, :H_FOX].set(attn_f_bias[i])
            qk_gain = jnp.concatenate(
                [jnp.tile(fox_q_gain[i], H_FOX), jnp.tile(sb_q_gain[i], H_SB),
                 jnp.tile(fox_k_gain[i], H_FOX), jnp.tile(sb_k_gain[i], H_SB)])[None, :]
            q, k, vt, cf = _attn_inproj(h, attn_norm[i][None, :], w_qk, wvt, wf, fb, qk_gain)
            o_fox, o_sb = _attention(q, k, vt, cf)
            mix = ((o_fox, 0), (o_sb, 0))
            w_mix = attn_w_out[i]
        else:
            y = _conv_mixer(h, conv_norm[i][None, :], conv_w_in[i].astype(BF16), conv_kernel[i])
            mix = ((y, 0), (y, 1))
            w_mix = conv_w_out[i]
        h = _mixer_out_ffn(h, mix, w_mix.astype(BF16), ffn_norm[layer][None, :],
                           ffn_w_up[layer].astype(BF16), ffn_conv[layer],
                           ffn_w_down[layer].astype(BF16))
    return h
```

```python
import functools

import jax
import jax.numpy as jnp
from jax import lax
from jax.experimental import pallas as pl
from jax.experimental.pallas import tpu as pltpu

F32 = jnp.float32
BF16 = jnp.bfloat16

D_MODEL = 1024
HEAD_DIM = 64
H_FOX = 8
H_SB = 8
MIX_WIDTH = (H_FOX + H_SB) * HEAD_DIM
D_FF = 2816
EPS = 1e-6
QK_SCALE = HEAD_DIM ** -0.5
LOG2E = 1.4426950408889634

LANES = 128
SUBLANES = 8
HEADS_PER_BLOCK = LANES // HEAD_DIM
BF16_ROWS = 2 * SUBLANES
MXU_DIM = 256
EXP2_CLAMP = 126.0
VMEM_LIMIT = 56 * 1024 * 1024

TM_PROJ = 256
TM_FFN = 512
FF_CHUNK = 256
T_FOX = 512
T_SB = 256
SB_HEADS_PER_STEP = 4
SB_EXIT = 160.0
FOX_EXIT = 160.0
NORM_SLACK = 1.01
CONV_HALO = SUBLANES


def _params(sem):
    return pltpu.CompilerParams(dimension_semantics=sem, vmem_limit_bytes=VMEM_LIMIT)


def _resident(shape):
    zeros = (0,) * len(shape)
    return pl.BlockSpec(shape, lambda *_: zeros, pipeline_mode=pl.Buffered(1))


def _rms_norm(x, g):
    ms = jnp.mean(x * x, axis=-1, keepdims=True)
    return x * lax.rsqrt(ms + EPS) * g


def _split3(x):
    hi = x.astype(BF16)
    r = x - hi.astype(F32)
    mid = r.astype(BF16)
    lo = (r - mid.astype(F32)).astype(BF16)
    return hi, mid, lo


def _dot_nt(a, b):
    return lax.dot_general(a, b, (((1,), (1,)), ((), ())), preferred_element_type=F32)


def _causal_conv3(ext_ref, tail, cur, w, tm):
    ext_ref[0:CONV_HALO, :] = tail
    ext_ref[CONV_HALO:CONV_HALO + tm, :] = cur
    return (w[2:3, :] * cur
            + w[1:2, :] * ext_ref[CONV_HALO - 1:CONV_HALO - 1 + tm, :]
            + w[0:1, :] * ext_ref[CONV_HALO - 2:CONV_HALO - 2 + tm, :])


def _attn_inproj_kernel(h_ref, g_ref, w_ref, wvt_ref, wf_ref, fb_ref, qkg_ref, gmat_ref,
                        tri_ref, q_ref, k_ref, vt_ref, cf_ref, carry_ref, *, tm):
    i = pl.program_id(1)

    @pl.when(i == 0)
    def _():
        carry_ref[...] = jnp.zeros_like(carry_ref)

    xn = _rms_norm(h_ref[...], g_ref[...]).astype(BF16)
    gmat = gmat_ref[...]

    n_blk = MIX_WIDTH // MXU_DIM
    proj = [jnp.dot(xn, w_ref[:, c * MXU_DIM:(c + 1) * MXU_DIM], preferred_element_type=F32)
            for c in range(2 * n_blk)]
    vt_ref[...] = _dot_nt(wvt_ref[...], xn).astype(BF16)
    fl = jnp.dot(xn, wf_ref[...], preferred_element_type=F32) + fb_ref[...]

    for c in range(2 * n_blk):
        out_ref, scale = (q_ref, QK_SCALE * LOG2E) if c < n_blk else (k_ref, 1.0)
        t = proj[c]
        ssq = jnp.dot((t * t).astype(BF16), gmat, preferred_element_type=F32)
        gain = qkg_ref[:, c * MXU_DIM:(c + 1) * MXU_DIM]
        tn = t * lax.rsqrt(ssq * (1.0 / HEAD_DIM) + EPS) * gain
        if scale != 1.0:
            tn = tn * scale
        lo = (c % n_blk) * MXU_DIM
        out_ref[:, lo:lo + MXU_DIM] = tn.astype(BF16)

    log_f = jnp.minimum(fl, 0.0) - jnp.log1p(jnp.exp(-jnp.abs(fl)))
    tri = tri_ref[...]
    hi, mid, lo = _split3(log_f)
    cs = (jnp.dot(tri, hi, preferred_element_type=F32)
          + jnp.dot(tri, mid, preferred_element_type=F32)
          + jnp.dot(tri, lo, preferred_element_type=F32))
    cf = cs + carry_ref[0:1, :]
    carry_ref[...] = jnp.broadcast_to(cf[tm - 1:tm, :], carry_ref.shape)
    cf_ref[...] = cf


def _attn_inproj(h, g, w_qk, wvt, wf, fb, qk_gain):
    b, s, d = h.shape
    tm = TM_PROJ
    r = lax.broadcasted_iota(jnp.int32, (MXU_DIM, MXU_DIM), 0) // HEAD_DIM
    c = lax.broadcasted_iota(jnp.int32, (MXU_DIM, MXU_DIM), 1) // HEAD_DIM
    gmat = (r == c).astype(BF16)
    rr = lax.broadcasted_iota(jnp.int32, (tm, tm), 0)
    cc = lax.broadcasted_iota(jnp.int32, (tm, tm), 1)
    tri = (cc <= rr).astype(BF16)
    row_spec = pl.BlockSpec((None, tm, MIX_WIDTH), lambda bi, i: (bi, i, 0))
    return pl.pallas_call(
        functools.partial(_attn_inproj_kernel, tm=tm),
        grid=(b, s // tm),
        in_specs=[
            pl.BlockSpec((None, tm, d), lambda bi, i: (bi, i, 0)),
            _resident(g.shape), _resident(w_qk.shape), _resident(wvt.shape),
            _resident(wf.shape), _resident(fb.shape), _resident(qk_gain.shape),
            _resident(gmat.shape), _resident(tri.shape),
        ],
        out_specs=[row_spec, row_spec,
                   pl.BlockSpec((None, MIX_WIDTH, tm), lambda bi, i: (bi, 0, i)),
                   pl.BlockSpec((None, tm, LANES), lambda bi, i: (bi, i, 0))],
        out_shape=[jax.ShapeDtypeStruct((b, s, MIX_WIDTH), BF16),
                   jax.ShapeDtypeStruct((b, s, MIX_WIDTH), BF16),
                   jax.ShapeDtypeStruct((b, MIX_WIDTH, s), BF16),
                   jax.ShapeDtypeStruct((b, s, LANES), F32)],
        scratch_shapes=[pltpu.VMEM((SUBLANES, LANES), F32)],
        compiler_params=_params(("arbitrary", "arbitrary")),
        name="attn_inproj",
    )(h, g, w_qk, wvt, wf, fb, qk_gain, gmat, tri)


def _head_masks(t, n_heads=HEADS_PER_BLOCK):
    lane = lax.broadcasted_iota(jnp.int32, (t, n_heads * HEAD_DIM), 1)
    return [(lane >= HEAD_DIM * hh) & (lane < HEAD_DIM * (hh + 1)) for hh in range(n_heads)]


def _lane_pieces(t):
    return [slice(c * LANES, (c + 1) * LANES) for c in range(t // LANES)]


def _sweep_key_tiles(i, scores, step, more_needed, test_every_tile):
    scores(i, 0)
    step(i, 0, True, (jnp.maximum(i - 1, 0), 1))

    def unfinished(carry):
        r, go = carry
        return jnp.logical_and(r < i // 2, go > 0)

    def pair(carry):
        r, _ = carry
        j = i - 1 - 2 * r
        step(j, 1, False, (jnp.maximum(j - 1, 0), 0))
        if not test_every_tile:
            step(j - 1, 0, False, (jnp.maximum(j - 2, 0), 1))
            return r + 1, more_needed(jnp.maximum(j - 2, 0))
        go_mid = more_needed(j - 1)

        @pl.when(go_mid > 0)
        def _():
            step(j - 1, 0, False, (jnp.maximum(j - 2, 0), 1))

        return r + 1, go_mid & more_needed(jnp.maximum(j - 2, 0))

    _, go = lax.while_loop(unfinished, pair,
                           (jnp.int32(0), more_needed(jnp.maximum(i - 1, 0))))

    @pl.when(jnp.logical_and(i % 2 == 1, go > 0))
    def _():
        step(0, 1, False, None)


def _write_heads(o_ref, outs_t):
    o_ref[...] = jnp.concatenate(outs_t, axis=0).T.astype(o_ref.dtype)


def _fox_parts(q_ref, k_ref, vt_ref, cfb_ref, st_ref, m_ref, acc_ref, t):
    q = q_ref[...]
    qh = [jnp.where(mk, q, jnp.zeros_like(q)) for mk in _head_masks(t)]
    m_ref[...] = jnp.full_like(m_ref, -jnp.inf)
    acc_ref[...] = jnp.zeros_like(acc_ref)
    key = lax.broadcasted_iota(jnp.int32, (t, LANES), 0)
    qry = lax.broadcasted_iota(jnp.int32, (t, LANES), 1)
    pieces = _lane_pieces(t)
    ones_rows = jnp.ones((BF16_ROWS, t), BF16)

    def scores(j, slot, hh):
        off = pl.multiple_of(j * t, t)
        st = _dot_nt(k_ref[pl.ds(off, t), :], qh[hh])
        fcol = cfb_ref[hh, pl.ds(off, t), :]
        for cols in pieces:
            st_ref[slot, hh, :, cols] = st[:, cols] - fcol

    def consume(j, slot, masked, hh):
        off = pl.multiple_of(j * t, t)
        m_old = m_ref[hh]
        m_new, alpha, prob = [], [], []
        for c, cols in enumerate(pieces):
            x = st_ref[slot, hh, :, cols]
            if masked:
                x = jnp.where(key <= qry + c * LANES, x, -jnp.inf)
            m_o = m_old[:, cols]
            m_n = jnp.maximum(m_o, jnp.max(x, axis=0, keepdims=True))
            m_new.append(m_n)
            alpha.append(jnp.exp2(m_o - m_n))
            prob.append(jnp.exp2(x - m_n).astype(BF16))
        vt_h = jnp.concatenate(
            [vt_ref[hh * HEAD_DIM:(hh + 1) * HEAD_DIM, pl.ds(off, t)], ones_rows], axis=0)
        pv = jnp.dot(vt_h, jnp.concatenate(prob, axis=1), preferred_element_type=F32)
        acc_ref[hh] = jnp.concatenate(alpha, axis=1) * acc_ref[hh] + pv
        m_ref[hh] = jnp.concatenate(m_new, axis=1)

    def result():
        return [acc_ref[hh, 0:HEAD_DIM, :] / acc_ref[hh, HEAD_DIM:HEAD_DIM + 1, :]
                for hh in range(HEADS_PER_BLOCK)]

    return scores, consume, result


def _sb_parts(q_ref, k_ref, vt_ref, tri_ref, z_ref, r_ref, acc_ref, t, n_heads):
    q = q_ref[...]
    qh = [jnp.where(mk, q, jnp.zeros_like(q)) for mk in _head_masks(t, n_heads)]
    r_ref[...] = jnp.zeros_like(r_ref)
    acc_ref[...] = jnp.zeros_like(acc_ref)
    key = lax.broadcasted_iota(jnp.int32, (t, t), 0)
    qry = lax.broadcasted_iota(jnp.int32, (t, t), 1)
    strict = key < qry
    n_sub = t // MXU_DIM

    sub_rows = [slice(c * MXU_DIM, (c + 1) * MXU_DIM) for c in range(n_sub)]

    def scores(j, slot, hh):
        off = pl.multiple_of(j * t, t)
        z_ref[slot, hh] = _dot_nt(k_ref[pl.ds(off, t), :], qh[hh])

    def gate_sums(slot, masked, hh):
        z = z_ref[slot, hh]
        a = jnp.maximum(LOG2E * jnp.log(1.0 + jnp.exp2(jnp.minimum(z, EXP2_CLAMP))), z)
        base = z - a
        if masked:
            a = jnp.where(strict, a, 0.0)
            base = jnp.where(strict, base, -jnp.inf)
        tri = tri_ref[...]
        sums = [jnp.dot(tri, a[rows, :].astype(BF16), preferred_element_type=F32)
                for rows in sub_rows]
        return base, sums

    def accumulate(j, staged, hh):
        off = pl.multiple_of(j * t, t)
        base, sums = staged
        r_run = r_ref[hh]
        w = [None] * n_sub
        for c in reversed(range(n_sub)):
            later = sums[c][0:MXU_DIM, :]
            w[c] = jnp.exp2(base[sub_rows[c], :] - later - r_run).astype(BF16)
            r_run = r_run + sums[c][MXU_DIM:MXU_DIM + 1, :]
        vt_h = vt_ref[hh * HEAD_DIM:(hh + 1) * HEAD_DIM, pl.ds(off, t)]
        acc_ref[hh] += jnp.dot(vt_h, jnp.concatenate(w, axis=0), preferred_element_type=F32)
        r_ref[hh] = r_run

    def result():
        return [acc_ref[hh] for hh in range(n_heads)]

    return scores, gate_sums, accumulate, result


def _fox_kernel(q_ref, k_ref, vt_ref, cf_ref, o_ref, cfb_ref, kmax_ref, st_ref, m_ref, acc_ref,
                *, t):
    p = pl.program_id(1)
    i = pl.program_id(2)
    heads = range(HEADS_PER_BLOCK)
    lane_row = lax.broadcasted_iota(jnp.int32, (LANES, LANES), 0)
    head_rows = [((lane_row >= HEAD_DIM * hh) & (lane_row < HEAD_DIM * (hh + 1))).astype(BF16)
                 for hh in heads]

    @pl.when(i == 0)
    def _():
        for hh in heads:
            sel = (lane_row == p * HEADS_PER_BLOCK + hh).astype(BF16)

            def fill(r, kmax2):
                off = pl.multiple_of(r * t, t)
                hi, mid, lo = _split3(cf_ref[pl.ds(off, t), :])
                cfb_ref[hh, pl.ds(off, t), :] = LOG2E * (
                    jnp.dot(hi, sel, preferred_element_type=F32)
                    + jnp.dot(mid, sel, preferred_element_type=F32)
                    + jnp.dot(lo, sel, preferred_element_type=F32))
                kt = k_ref[pl.ds(off, t), :].astype(F32)
                ksq = jnp.dot((kt * kt).astype(BF16), head_rows[hh],
                              preferred_element_type=F32)
                return jnp.maximum(kmax2, ksq)

            kmax2 = lax.fori_loop(0, cf_ref.shape[0] // t, fill, jnp.zeros((t, LANES), F32))
            kmax_ref[hh] = jnp.max(kmax2, axis=0, keepdims=True)

    scores_h, consume_h, result = _fox_parts(q_ref, k_ref, vt_ref, cfb_ref, st_ref, m_ref,
                                             acc_ref, t)

    q32 = q_ref[...].astype(F32)
    ones_lhs = jnp.ones((BF16_ROWS, LANES), BF16)
    qk_bound = []
    for hh, mk in zip(heads, _head_masks(t)):
        qsq = jnp.where(mk, q32 * q32, 0.0).astype(BF16)
        qn2 = _dot_nt(ones_lhs, qsq)[0:1, :]
        qk_bound.append(NORM_SLACK * jnp.sqrt(qn2 * kmax_ref[hh][:, 0:1]))

    def more_needed(j):
        row = (j + 1) * t - 1
        need = jnp.int32(0)
        for hh in heads:
            f_hi = cfb_ref[hh, pl.ds(row, 1), :][:, 0:1]
            gap = jnp.max(qk_bound[hh] - f_hi - m_ref[hh])
            need = need | jnp.logical_not(gap <= -FOX_EXIT).astype(jnp.int32)
        return need

    def scores(j, slot):
        for hh in heads:
            scores_h(j, slot, hh)

    def step(j, slot, masked, nxt):
        for hh in heads:
            if nxt is not None:
                scores_h(*nxt, hh)
            consume_h(j, slot, masked, hh)

    _sweep_key_tiles(i, scores, step, more_needed, test_every_tile=False)
    _write_heads(o_ref, result())


def _sb_kernel(q_ref, k_ref, vt_ref, tri_ref, o_ref, z_ref, r_ref, acc_ref, *, t, n_heads):
    i = pl.program_id(2)
    scores_h, gate_sums_h, accumulate_h, result = _sb_parts(
        q_ref, k_ref, vt_ref, tri_ref, z_ref, r_ref, acc_ref, t, n_heads)
    heads = range(n_heads)

    def scores(j, slot):
        for hh in heads:
            scores_h(j, slot, hh)

    def step(j, slot, masked, nxt):
        staged = []
        for hh in heads:
            if nxt is not None:
                scores_h(*nxt, hh)
            staged.append(gate_sums_h(slot, masked, hh))
        for hh in heads:
            accumulate_h(j, staged[hh], hh)

    def more_needed(_):
        return (jnp.min(r_ref[...]) < SB_EXIT).astype(jnp.int32)

    _sweep_key_tiles(i, scores, step, more_needed, test_every_tile=True)
    _write_heads(o_ref, result())


def _attention(q, k, vt, cf):
    b, s, _ = q.shape
    rr = lax.broadcasted_iota(jnp.int32, (MXU_DIM + BF16_ROWS, MXU_DIM), 0)
    cc = lax.broadcasted_iota(jnp.int32, (MXU_DIM + BF16_ROWS, MXU_DIM), 1)
    tri = ((cc > rr) | (rr >= MXU_DIM)).astype(BF16)

    def specs(t, w, first):
        return [pl.BlockSpec((None, t, w), lambda bi, p, i: (bi, i, p + first)),
                pl.BlockSpec((None, s, w), lambda bi, p, i: (bi, 0, p + first)),
                pl.BlockSpec((None, w, s), lambda bi, p, i: (bi, p + first, 0))]

    def out_spec(t, w):
        return pl.BlockSpec((None, t, w), lambda bi, p, i: (bi, i, p))

    params = _params(("arbitrary", "arbitrary", "arbitrary"))
    t = T_FOX
    o_fox = pl.pallas_call(
        functools.partial(_fox_kernel, t=t),
        grid=(b, H_FOX // HEADS_PER_BLOCK, s // t),
        in_specs=specs(t, LANES, 0) + [
            pl.BlockSpec((None, s, LANES), lambda bi, p, i: (bi, 0, 0),
                         pipeline_mode=pl.Buffered(1))],
        out_specs=out_spec(t, LANES),
        out_shape=jax.ShapeDtypeStruct((b, s, H_FOX * HEAD_DIM), BF16),
        scratch_shapes=[pltpu.VMEM((HEADS_PER_BLOCK, s, LANES), F32),
                        pltpu.VMEM((HEADS_PER_BLOCK, 1, LANES), F32),
                        pltpu.VMEM((2, HEADS_PER_BLOCK, t, t), F32),
                        pltpu.VMEM((HEADS_PER_BLOCK, 1, t), F32),
                        pltpu.VMEM((HEADS_PER_BLOCK, HEAD_DIM + BF16_ROWS, t), F32)],
        compiler_params=params,
        name="fox_attention",
    )(q, k, vt, cf)
    t = T_SB
    nh = SB_HEADS_PER_STEP
    w = nh * HEAD_DIM
    o_sb = pl.pallas_call(
        functools.partial(_sb_kernel, t=t, n_heads=nh),
        grid=(b, H_SB // nh, s // t),
        in_specs=specs(t, w, H_FOX // nh) + [_resident(tri.shape)],
        out_specs=out_spec(t, w),
        out_shape=jax.ShapeDtypeStruct((b, s, H_SB * HEAD_DIM), BF16),
        scratch_shapes=[pltpu.VMEM((2, nh, t, t), F32),
                        pltpu.VMEM((nh, 1, t), F32),
                        pltpu.VMEM((nh, HEAD_DIM, t), F32)],
        compiler_params=params,
        name="sb_attention",
    )(q, k, vt, tri)
    return o_fox, o_sb


def _conv_mixer_kernel(h_ref, g_ref, win_ref, cw_ref, o_ref, ext_ref, tail_ref, *, tm):
    i = pl.program_id(1)

    @pl.when(i == 0)
    def _():
        tail_ref[...] = jnp.zeros_like(tail_ref)

    x = h_ref[...]
    xn = _rms_norm(x, g_ref[...]).astype(BF16)
    d = x.shape[-1]
    gate_c = jnp.dot(xn, win_ref[:, d:2 * d], preferred_element_type=F32)
    u = jnp.dot(xn, win_ref[:, 2 * d:3 * d], preferred_element_type=F32)
    cu = gate_c * u
    y = _causal_conv3(ext_ref, tail_ref[...], cu, cw_ref[...], tm)
    tail_ref[...] = cu[tm - CONV_HALO:tm, :]
    gate_b = jnp.dot(xn, win_ref[:, 0:d], preferred_element_type=F32)
    o_ref[...] = (gate_b * y).astype(o_ref.dtype)


def _conv_mixer(h, g, w_in, conv_w):
    b, s, d = h.shape
    tm = TM_PROJ
    return pl.pallas_call(
        functools.partial(_conv_mixer_kernel, tm=tm),
        grid=(b, s // tm),
        in_specs=[
            pl.BlockSpec((None, tm, d), lambda bi, i: (bi, i, 0)),
            _resident(g.shape), _resident(w_in.shape), _resident(conv_w.shape),
        ],
        out_specs=pl.BlockSpec((None, tm, d), lambda bi, i: (bi, i, 0)),
        out_shape=jax.ShapeDtypeStruct(h.shape, BF16),
        scratch_shapes=[pltpu.VMEM((tm + CONV_HALO, d), F32),
                        pltpu.VMEM((CONV_HALO, d), F32)],
        compiler_params=_params(("arbitrary", "arbitrary")),
        name="conv_mixer",
    )(h, g, w_in, conv_w)


def _ffn_kernel(h_ref, a0_ref, a1_ref, w0_ref, w1_ref, g_ref, wup_ref, cw_ref, wd_ref, o_ref,
                xn_ref, ext_ref, tail_ref, act_ref, *, tm, fc):
    i = pl.program_id(1)
    n_chunks = D_FF // fc

    @pl.when(i == 0)
    def _():
        tail_ref[...] = jnp.zeros_like(tail_ref)

    x = (h_ref[...]
         + jnp.dot(a0_ref[...], w0_ref[...], preferred_element_type=F32)
         + jnp.dot(a1_ref[...], w1_ref[...], preferred_element_type=F32))
    xn_ref[...] = _rms_norm(x, g_ref[...]).astype(BF16)
    o_ref[...] = x

    def up_project(c):
        slot = c % 2
        for part in range(2):
            cols = slice(part * D_FF + c * fc, part * D_FF + (c + 1) * fc)
            ext_ref[slot, part, 0:CONV_HALO, :] = tail_ref[part, :, c * fc:(c + 1) * fc]
            ext_ref[slot, part, CONV_HALO:CONV_HALO + tm, :] = jnp.dot(
                xn_ref[...], wup_ref[:, cols], preferred_element_type=F32)

    def conv_gate(c):
        slot = c % 2
        y = []
        for part in range(2):
            cols = slice(part * D_FF + c * fc, part * D_FF + (c + 1) * fc)
            w = cw_ref[:, cols]
            taps = [w[k:k + 1, :] * ext_ref[slot, part, CONV_HALO - 2 + k:CONV_HALO - 2 + k + tm, :]
                    for k in range(3)]
            y.append(taps[2] + taps[1] + taps[0])
            tail_ref[part, :, c * fc:(c + 1) * fc] = ext_ref[slot, part, tm:tm + CONV_HALO, :]
        half_g = 0.5 * y[0]
        act_ref[:, c * fc:(c + 1) * fc] = (
            (half_g + half_g * jnp.tanh(half_g)) * y[1]).astype(BF16)

    def down_project(rows):
        o_ref[...] += jnp.dot(act_ref[:, rows], wd_ref[rows, :], preferred_element_type=F32)

    last = n_chunks - 1
    up_project(0)
    for c in range(last):
        up_project(c + 1)
        conv_gate(c)
    down_project(slice(0, last * fc))
    conv_gate(last)
    down_project(slice(last * fc, D_FF))


def _mixer_out_ffn(h, mix, w_mix, g, w_up, conv_w, w_down):
    b, s, d = h.shape
    tm = TM_FFN
    fc = FF_CHUNK
    half = d // 2
    (a0, blk0), (a1, blk1) = mix
    w0, w1 = w_mix[:half], w_mix[half:]

    def half_spec(blk):
        return pl.BlockSpec((None, tm, half), lambda bi, i: (bi, i, blk))

    return pl.pallas_call(
        functools.partial(_ffn_kernel, tm=tm, fc=fc),
        grid=(b, s // tm),
        in_specs=[
            pl.BlockSpec((None, tm, d), lambda bi, i: (bi, i, 0)),
            half_spec(blk0), half_spec(blk1), _resident(w0.shape), _resident(w1.shape),
            _resident(g.shape), _resident(w_up.shape), _resident(conv_w.shape),
            _resident(w_down.shape),
        ],
        out_specs=pl.BlockSpec((None, tm, d), lambda bi, i: (bi, i, 0)),
        out_shape=jax.ShapeDtypeStruct(h.shape, F32),
        scratch_shapes=[pltpu.VMEM((tm, d), BF16),
                        pltpu.VMEM((2, 2, tm + CONV_HALO, fc), F32),
                        pltpu.VMEM((2, CONV_HALO, D_FF), F32),
                        pltpu.VMEM((tm, D_FF), BF16)],
        compiler_params=_params(("arbitrary", "arbitrary")),
        name="conv_ffn",
    )(h, a0, a1, w0, w1, g, w_up, conv_w, w_down)


def kernel(x, attn_norm, attn_w_in, attn_f_bias, fox_q_gain, fox_k_gain, sb_q_gain, sb_k_gain,
           attn_w_out, conv_norm, conv_w_in, conv_kernel, conv_w_out, ffn_norm, ffn_w_up,
           ffn_conv, ffn_w_down):
    depth = ffn_norm.shape[0]
    h = x
    for layer in range(depth):
        i = layer // 2
        if layer % 2 == 0:
            w_in = attn_w_in[i]
            w_qk = w_in[:, :2 * MIX_WIDTH].astype(BF16)
            wvt = w_in[:, 2 * MIX_WIDTH:3 * MIX_WIDTH].T.astype(BF16)
            wf = jnp.zeros((D_MODEL, LANES), BF16).at[:, :H_FOX].set(
                w_in[:, 3 * MIX_WIDTH:].astype(BF16))
            fb = jnp.zeros((1, LANES), F32).at[0, :H_FOX].set(attn_f_bias[i])
            qk_gain = jnp.concatenate(
                [jnp.tile(fox_q_gain[i], H_FOX), jnp.tile(sb_q_gain[i], H_SB),
                 jnp.tile(fox_k_gain[i], H_FOX), jnp.tile(sb_k_gain[i], H_SB)])[None, :]
            q, k, vt, cf = _attn_inproj(h, attn_norm[i][None, :], w_qk, wvt, wf, fb, qk_gain)
            o_fox, o_sb = _attention(q, k, vt, cf)
            mix = ((o_fox, 0), (o_sb, 0))
            w_mix = attn_w_out[i]
        else:
            y = _conv_mixer(h, conv_norm[i][None, :], conv_w_in[i].astype(BF16), conv_kernel[i])
            mix = ((y, 0), (y, 1))
            w_mix = conv_w_out[i]
        h = _mixer_out_ffn(h, mix, w_mix.astype(BF16), ffn_norm[layer][None, :],
                           ffn_w_up[layer].astype(BF16), ffn_conv[layer],
                           ffn_w_down[layer].astype(BF16))
    return h
```

```python
import functools

import jax
import jax.numpy as jnp
from jax import lax
from jax.experimental import pallas as pl
from jax.experimental.pallas import tpu as pltpu

F32 = jnp.float32
BF16 = jnp.bfloat16

D_MODEL = 1024
HEAD_DIM = 64
H_FOX = 8
H_SB = 8
MIX_WIDTH = (H_FOX + H_SB) * HEAD_DIM
D_FF = 2816
EPS = 1e-6
QK_SCALE = HEAD_DIM ** -0.5
LOG2E = 1.4426950408889634

LANES = 128
SUBLANES = 8
HEADS_PER_BLOCK = LANES // HEAD_DIM
BF16_ROWS = 2 * SUBLANES
MXU_DIM = 256
EXP2_CLAMP = 126.0
VMEM_LIMIT = 56 * 1024 * 1024

TM_PROJ = 256
TM_FFN = 512
FF_CHUNK = 256
T_FOX = 512
T_SB = 256
SB_HEADS_PER_STEP = 4
SB_EXIT = 160.0
FOX_EXIT = 160.0
NORM_SLACK = 1.01
CONV_HALO = SUBLANES


def _params(sem):
    return pltpu.CompilerParams(dimension_semantics=sem, vmem_limit_bytes=VMEM_LIMIT)


def _resident(shape):
    zeros = (0,) * len(shape)
    return pl.BlockSpec(shape, lambda *_: zeros, pipeline_mode=pl.Buffered(1))


def _rms_norm(x, g):
    ms = jnp.mean(x * x, axis=-1, keepdims=True)
    return x * lax.rsqrt(ms + EPS) * g


def _split3(x):
    hi = x.astype(BF16)
    r = x - hi.astype(F32)
    mid = r.astype(BF16)
    lo = (r - mid.astype(F32)).astype(BF16)
    return hi, mid, lo


def _dot_nt(a, b):
    return lax.dot_general(a, b, (((1,), (1,)), ((), ())), preferred_element_type=F32)


def _causal_conv3(ext_ref, tail, cur, w, tm):
    ext_ref[0:CONV_HALO, :] = tail
    ext_ref[CONV_HALO:CONV_HALO + tm, :] = cur
    return (w[2:3, :] * cur
            + w[1:2, :] * ext_ref[CONV_HALO - 1:CONV_HALO - 1 + tm, :]
            + w[0:1, :] * ext_ref[CONV_HALO - 2:CONV_HALO - 2 + tm, :])


def _attn_inproj_kernel(h_ref, g_ref, w_ref, wvt_ref, wf_ref, fb_ref, qkg_ref, gmat_ref,
                        tri_ref, q_ref, k_ref, vt_ref, cf_ref, carry_ref, *, tm):
    i = pl.program_id(1)

    @pl.when(i == 0)
    def _():
        carry_ref[...] = jnp.zeros_like(carry_ref)

    xn = _rms_norm(h_ref[...], g_ref[...]).astype(BF16)
    gmat = gmat_ref[...]

    n_blk = MIX_WIDTH // MXU_DIM
    proj = [jnp.dot(xn, w_ref[:, c * MXU_DIM:(c + 1) * MXU_DIM], preferred_element_type=F32)
            for c in range(2 * n_blk)]
    vt_ref[...] = _dot_nt(wvt_ref[...], xn).astype(BF16)
    fl = jnp.dot(xn, wf_ref[...], preferred_element_type=F32) + fb_ref[...]

    for c in range(2 * n_blk):
        out_ref, scale = (q_ref, QK_SCALE * LOG2E) if c < n_blk else (k_ref, 1.0)
        t = proj[c]
        ssq = jnp.dot((t * t).astype(BF16), gmat, preferred_element_type=F32)
        gain = qkg_ref[:, c * MXU_DIM:(c + 1) * MXU_DIM]
        tn = t * lax.rsqrt(ssq * (1.0 / HEAD_DIM) + EPS) * gain
        if scale != 1.0:
            tn = tn * scale
        lo = (c % n_blk) * MXU_DIM
        out_ref[:, lo:lo + MXU_DIM] = tn.astype(BF16)

    log_f = jnp.minimum(fl, 0.0) - jnp.log1p(jnp.exp(-jnp.abs(fl)))
    tri = tri_ref[...]
    hi, mid, lo = _split3(log_f)
    cs = (jnp.dot(tri, hi, preferred_element_type=F32)
          + jnp.dot(tri, mid, preferred_element_type=F32)
          + jnp.dot(tri, lo, preferred_element_type=F32))
    cf = cs + carry_ref[0:1, :]
    carry_ref[...] = jnp.broadcast_to(cf[tm - 1:tm, :], carry_ref.shape)
    cf_ref[...] = cf


def _attn_inproj(h, g, w_qk, wvt, wf, fb, qk_gain):
    b, s, d = h.shape
    tm = TM_PROJ
    r = lax.broadcasted_iota(jnp.int32, (MXU_DIM, MXU_DIM), 0) // HEAD_DIM
    c = lax.broadcasted_iota(jnp.int32, (MXU_DIM, MXU_DIM), 1) // HEAD_DIM
    gmat = (r == c).astype(BF16)
    rr = lax.broadcasted_iota(jnp.int32, (tm, tm), 0)
    cc = lax.broadcasted_iota(jnp.int32, (tm, tm), 1)
    tri = (cc <= rr).astype(BF16)
    row_spec = pl.BlockSpec((None, tm, MIX_WIDTH), lambda bi, i: (bi, i, 0))
    return pl.pallas_call(
        functools.partial(_attn_inproj_kernel, tm=tm),
        grid=(b, s // tm),
        in_specs=[
            pl.BlockSpec((None, tm, d), lambda bi, i: (bi, i, 0)),
            _resident(g.shape), _resident(w_qk.shape), _resident(wvt.shape),
            _resident(wf.shape), _resident(fb.shape), _resident(qk_gain.shape),
            _resident(gmat.shape), _resident(tri.shape),
        ],
        out_specs=[row_spec, row_spec,
                   pl.BlockSpec((None, MIX_WIDTH, tm), lambda bi, i: (bi, 0, i)),
                   pl.BlockSpec((None, tm, LANES), lambda bi, i: (bi, i, 0))],
        out_shape=[jax.ShapeDtypeStruct((b, s, MIX_WIDTH), BF16),
                   jax.ShapeDtypeStruct((b, s, MIX_WIDTH), BF16),
                   jax.ShapeDtypeStruct((b, MIX_WIDTH, s), BF16),
                   jax.ShapeDtypeStruct((b, s, LANES), F32)],
        scratch_shapes=[pltpu.VMEM((SUBLANES, LANES), F32)],
        compiler_params=_params(("arbitrary", "arbitrary")),
        name="attn_inproj",
    )(h, g, w_qk, wvt, wf, fb, qk_gain, gmat, tri)


def _head_masks(t, n_heads=HEADS_PER_BLOCK):
    lane = lax.broadcasted_iota(jnp.int32, (t, n_heads * HEAD_DIM), 1)
    return [(lane >= HEAD_DIM * hh) & (lane < HEAD_DIM * (hh + 1)) for hh in range(n_heads)]


def _lane_pieces(t):
    return [slice(c * LANES, (c + 1) * LANES) for c in range(t // LANES)]


def _sweep_key_tiles(i, scores, step, more_needed, test_every_tile):
    def clamp(j):
        return jnp.maximum(j, 0)

    scores(i, 0)
    step(i, 0, True, (clamp(i - 1), 1))
    go_single = jnp.logical_and(i >= 1, more_needed(clamp(i - 1)) > 0)

    @pl.when(go_single)
    def _():
        step(i - 1, 1, False, (clamp(i - 2), 0))

    rest = clamp(i - 1)

    def unfinished(carry):
        r, go = carry
        return jnp.logical_and(r < rest // 2, go > 0)

    def pair(carry):
        r, _ = carry
        j = i - 2 - 2 * r
        step(j, 0, False, (clamp(j - 1), 1))
        if not test_every_tile:
            step(j - 1, 1, False, (clamp(j - 2), 0))
            return r + 1, more_needed(clamp(j - 2))
        go_mid = more_needed(j - 1)

        @pl.when(go_mid > 0)
        def _():
            step(j - 1, 1, False, (clamp(j - 2), 0))

        return r + 1, go_mid & more_needed(clamp(j - 2))

    go_pairs = go_single.astype(jnp.int32) & more_needed(clamp(i - 2))
    _, go = lax.while_loop(unfinished, pair, (jnp.int32(0), go_pairs))

    @pl.when(jnp.logical_and(rest % 2 == 1, go > 0))
    def _():
        step(0, 0, False, None)


def _write_heads(o_ref, outs_t):
    o_ref[...] = jnp.concatenate(outs_t, axis=0).T.astype(o_ref.dtype)


def _fox_parts(q_ref, k_ref, vt_ref, cfb_ref, st_ref, m_ref, acc_ref, t):
    q = q_ref[...]
    qh = [jnp.where(mk, q, jnp.zeros_like(q)) for mk in _head_masks(t)]
    m_ref[...] = jnp.full_like(m_ref, -jnp.inf)
    acc_ref[...] = jnp.zeros_like(acc_ref)
    key = lax.broadcasted_iota(jnp.int32, (t, LANES), 0)
    qry = lax.broadcasted_iota(jnp.int32, (t, LANES), 1)
    pieces = _lane_pieces(t)
    ones_rows = jnp.ones((BF16_ROWS, t), BF16)

    def scores(j, slot, hh):
        off = pl.multiple_of(j * t, t)
        st = _dot_nt(k_ref[pl.ds(off, t), :], qh[hh])
        fcol = cfb_ref[hh, pl.ds(off, t), :]
        for cols in pieces:
            st_ref[slot, hh, :, cols] = st[:, cols] - fcol

    def consume(j, slot, masked, hh):
        off = pl.multiple_of(j * t, t)
        m_old = m_ref[hh]
        m_new, alpha, prob = [], [], []
        for c, cols in enumerate(pieces):
            x = st_ref[slot, hh, :, cols]
            if masked:
                x = jnp.where(key <= qry + c * LANES, x, -jnp.inf)
            m_o = m_old[:, cols]
            m_n = jnp.maximum(m_o, jnp.max(x, axis=0, keepdims=True))
            m_new.append(m_n)
            alpha.append(jnp.exp2(m_o - m_n))
            prob.append(jnp.exp2(x - m_n).astype(BF16))
        vt_h = jnp.concatenate(
            [vt_ref[hh * HEAD_DIM:(hh + 1) * HEAD_DIM, pl.ds(off, t)], ones_rows], axis=0)
        pv = jnp.dot(vt_h, jnp.concatenate(prob, axis=1), preferred_element_type=F32)
        acc_ref[hh] = jnp.concatenate(alpha, axis=1) * acc_ref[hh] + pv
        m_ref[hh] = jnp.concatenate(m_new, axis=1)

    def result():
        return [acc_ref[hh, 0:HEAD_DIM, :] / acc_ref[hh, HEAD_DIM:HEAD_DIM + 1, :]
                for hh in range(HEADS_PER_BLOCK)]

    return scores, consume, result


def _sb_parts(q_ref, k_ref, vt_ref, tri_ref, z_ref, r_ref, acc_ref, t, n_heads):
    q = q_ref[...]
    qh = [jnp.where(mk, q, jnp.zeros_like(q)) for mk in _head_masks(t, n_heads)]
    r_ref[...] = jnp.zeros_like(r_ref)
    acc_ref[...] = jnp.zeros_like(acc_ref)
    key = lax.broadcasted_iota(jnp.int32, (t, t), 0)
    qry = lax.broadcasted_iota(jnp.int32, (t, t), 1)
    strict = key < qry
    n_sub = t // MXU_DIM

    sub_rows = [slice(c * MXU_DIM, (c + 1) * MXU_DIM) for c in range(n_sub)]

    def scores(j, slot, hh):
        off = pl.multiple_of(j * t, t)
        z_ref[slot, hh] = _dot_nt(k_ref[pl.ds(off, t), :], qh[hh])

    def gate_sums(slot, masked, hh):
        z = z_ref[slot, hh]
        a = jnp.maximum(LOG2E * jnp.log(1.0 + jnp.exp2(jnp.minimum(z, EXP2_CLAMP))), z)
        base = z - a
        if masked:
            a = jnp.where(strict, a, 0.0)
            base = jnp.where(strict, base, -jnp.inf)
        tri = tri_ref[...]
        sums = [jnp.dot(tri, a[rows, :].astype(BF16), preferred_element_type=F32)
                for rows in sub_rows]
        return base, sums

    def accumulate(j, staged, hh):
        off = pl.multiple_of(j * t, t)
        base, sums = staged
        r_run = r_ref[hh]
        w = [None] * n_sub
        for c in reversed(range(n_sub)):
            later = sums[c][0:MXU_DIM, :]
            w[c] = jnp.exp2(base[sub_rows[c], :] - later - r_run).astype(BF16)
            r_run = r_run + sums[c][MXU_DIM:MXU_DIM + 1, :]
        vt_h = vt_ref[hh * HEAD_DIM:(hh + 1) * HEAD_DIM, pl.ds(off, t)]
        acc_ref[hh] += jnp.dot(vt_h, jnp.concatenate(w, axis=0), preferred_element_type=F32)
        r_ref[hh] = r_run

    def result():
        return [acc_ref[hh] for hh in range(n_heads)]

    return scores, gate_sums, accumulate, result


def _fox_kernel(q_ref, k_ref, vt_ref, cf_ref, o_ref, cfb_ref, kmax_ref, st_ref, m_ref, acc_ref,
                *, t):
    p = pl.program_id(1)
    i = pl.program_id(2)
    heads = range(HEADS_PER_BLOCK)
    lane_row = lax.broadcasted_iota(jnp.int32, (LANES, LANES), 0)
    head_rows = [((lane_row >= HEAD_DIM * hh) & (lane_row < HEAD_DIM * (hh + 1))).astype(BF16)
                 for hh in heads]

    @pl.when(i == 0)
    def _():
        for hh in heads:
            sel = (lane_row == p * HEADS_PER_BLOCK + hh).astype(BF16)

            def fill(r, kmax2):
                off = pl.multiple_of(r * t, t)
                hi, mid, lo = _split3(cf_ref[pl.ds(off, t), :])
                cfb_ref[hh, pl.ds(off, t), :] = LOG2E * (
                    jnp.dot(hi, sel, preferred_element_type=F32)
                    + jnp.dot(mid, sel, preferred_element_type=F32)
                    + jnp.dot(lo, sel, preferred_element_type=F32))
                kt = k_ref[pl.ds(off, t), :].astype(F32)
                ksq = jnp.dot((kt * kt).astype(BF16), head_rows[hh],
                              preferred_element_type=F32)
                return jnp.maximum(kmax2, ksq)

            kmax2 = lax.fori_loop(0, cf_ref.shape[0] // t, fill, jnp.zeros((t, LANES), F32))
            kmax_ref[hh] = jnp.max(kmax2, axis=0, keepdims=True)

    scores_h, consume_h, result = _fox_parts(q_ref, k_ref, vt_ref, cfb_ref, st_ref, m_ref,
                                             acc_ref, t)

    q32 = q_ref[...].astype(F32)
    ones_lhs = jnp.ones((BF16_ROWS, LANES), BF16)
    qk_bound = []
    for hh, mk in zip(heads, _head_masks(t)):
        qsq = jnp.where(mk, q32 * q32, 0.0).astype(BF16)
        qn2 = _dot_nt(ones_lhs, qsq)[0:1, :]
        qk_bound.append(NORM_SLACK * jnp.sqrt(qn2 * kmax_ref[hh][:, 0:1]))

    def more_needed(j):
        row = (j + 1) * t - 1
        need = jnp.int32(0)
        for hh in heads:
            f_hi = cfb_ref[hh, pl.ds(row, 1), :][:, 0:1]
            gap = jnp.max(qk_bound[hh] - f_hi - m_ref[hh])
            need = need | jnp.logical_not(gap <= -FOX_EXIT).astype(jnp.int32)
        return need

    def scores(j, slot):
        for hh in heads:
            scores_h(j, slot, hh)

    def step(j, slot, masked, nxt):
        for hh in heads:
            if nxt is not None:
                scores_h(*nxt, hh)
            consume_h(j, slot, masked, hh)

    _sweep_key_tiles(i, scores, step, more_needed, test_every_tile=False)
    _write_heads(o_ref, result())


def _sb_kernel(q_ref, k_ref, vt_ref, tri_ref, o_ref, z_ref, r_ref, acc_ref, *, t, n_heads):
    i = pl.program_id(2)
    scores_h, gate_sums_h, accumulate_h, result = _sb_parts(
        q_ref, k_ref, vt_ref, tri_ref, z_ref, r_ref, acc_ref, t, n_heads)
    heads = range(n_heads)

    def scores(j, slot):
        for hh in heads:
            scores_h(j, slot, hh)

    def step(j, slot, masked, nxt):
        staged = []
        for hh in heads:
            if nxt is not None:
                scores_h(*nxt, hh)
            staged.append(gate_sums_h(slot, masked, hh))
        for hh in heads:
            accumulate_h(j, staged[hh], hh)

    def more_needed(_):
        return (jnp.min(r_ref[...]) < SB_EXIT).astype(jnp.int32)

    _sweep_key_tiles(i, scores, step, more_needed, test_every_tile=True)
    _write_heads(o_ref, result())


def _attention(q, k, vt, cf):
    b, s, _ = q.shape
    rr = lax.broadcasted_iota(jnp.int32, (MXU_DIM + BF16_ROWS, MXU_DIM), 0)
    cc = lax.broadcasted_iota(jnp.int32, (MXU_DIM + BF16_ROWS, MXU_DIM), 1)
    tri = ((cc > rr) | (rr >= MXU_DIM)).astype(BF16)

    def specs(t, w, first):
        return [pl.BlockSpec((None, t, w), lambda bi, p, i: (bi, i, p + first)),
                pl.BlockSpec((None, s, w), lambda bi, p, i: (bi, 0, p + first)),
                pl.BlockSpec((None, w, s), lambda bi, p, i: (bi, p + first, 0))]

    def out_spec(t, w):
        return pl.BlockSpec((None, t, w), lambda bi, p, i: (bi, i, p))

    params = _params(("arbitrary", "arbitrary", "arbitrary"))
    t = T_FOX
    o_fox = pl.pallas_call(
        functools.partial(_fox_kernel, t=t),
        grid=(b, H_FOX // HEADS_PER_BLOCK, s // t),
        in_specs=specs(t, LANES, 0) + [
            pl.BlockSpec((None, s, LANES), lambda bi, p, i: (bi, 0, 0),
                         pipeline_mode=pl.Buffered(1))],
        out_specs=out_spec(t, LANES),
        out_shape=jax.ShapeDtypeStruct((b, s, H_FOX * HEAD_DIM), BF16),
        scratch_shapes=[pltpu.VMEM((HEADS_PER_BLOCK, s, LANES), F32),
                        pltpu.VMEM((HEADS_PER_BLOCK, 1, LANES), F32),
                        pltpu.VMEM((2, HEADS_PER_BLOCK, t, t), F32),
                        pltpu.VMEM((HEADS_PER_BLOCK, 1, t), F32),
                        pltpu.VMEM((HEADS_PER_BLOCK, HEAD_DIM + BF16_ROWS, t), F32)],
        compiler_params=params,
        name="fox_attention",
    )(q, k, vt, cf)
    t = T_SB
    nh = SB_HEADS_PER_STEP
    w = nh * HEAD_DIM
    o_sb = pl.pallas_call(
        functools.partial(_sb_kernel, t=t, n_heads=nh),
        grid=(b, H_SB // nh, s // t),
        in_specs=specs(t, w, H_FOX // nh) + [_resident(tri.shape)],
        out_specs=out_spec(t, w),
        out_shape=jax.ShapeDtypeStruct((b, s, H_SB * HEAD_DIM), BF16),
        scratch_shapes=[pltpu.VMEM((2, nh, t, t), F32),
                        pltpu.VMEM((nh, 1, t), F32),
                        pltpu.VMEM((nh, HEAD_DIM, t), F32)],
        compiler_params=params,
        name="sb_attention",
    )(q, k, vt, tri)
    return o_fox, o_sb


def _conv_mixer_kernel(h_ref, g_ref, win_ref, cw_ref, o_ref, ext_ref, tail_ref, *, tm):
    i = pl.program_id(1)

    @pl.when(i == 0)
    def _():
        tail_ref[...] = jnp.zeros_like(tail_ref)

    x = h_ref[...]
    xn = _rms_norm(x, g_ref[...]).astype(BF16)
    d = x.shape[-1]
    gate_c = jnp.dot(xn, win_ref[:, d:2 * d], preferred_element_type=F32)
    u = jnp.dot(xn, win_ref[:, 2 * d:3 * d], preferred_element_type=F32)
    cu = gate_c * u
    y = _causal_conv3(ext_ref, tail_ref[...], cu, cw_ref[...], tm)
    tail_ref[...] = cu[tm - CONV_HALO:tm, :]
    gate_b = jnp.dot(xn, win_ref[:, 0:d], preferred_element_type=F32)
    o_ref[...] = (gate_b * y).astype(o_ref.dtype)


def _conv_mixer(h, g, w_in, conv_w):
    b, s, d = h.shape
    tm = TM_PROJ
    return pl.pallas_call(
        functools.partial(_conv_mixer_kernel, tm=tm),
        grid=(b, s // tm),
        in_specs=[
            pl.BlockSpec((None, tm, d), lambda bi, i: (bi, i, 0)),
            _resident(g.shape), _resident(w_in.shape), _resident(conv_w.shape),
        ],
        out_specs=pl.BlockSpec((None, tm, d), lambda bi, i: (bi, i, 0)),
        out_shape=jax.ShapeDtypeStruct(h.shape, BF16),
        scratch_shapes=[pltpu.VMEM((tm + CONV_HALO, d), F32),
                        pltpu.VMEM((CONV_HALO, d), F32)],
        compiler_params=_params(("arbitrary", "arbitrary")),
        name="conv_mixer",
    )(h, g, w_in, conv_w)


def _ffn_kernel(h_ref, a0_ref, a1_ref, w0_ref, w1_ref, g_ref, wup_ref, cw_ref, wd_ref, o_ref,
                xn_ref, ext_ref, tail_ref, act_ref, *, tm, fc):
    i = pl.program_id(1)
    n_chunks = D_FF // fc

    @pl.when(i == 0)
    def _():
        tail_ref[...] = jnp.zeros_like(tail_ref)

    x = (h_ref[...]
         + jnp.dot(a0_ref[...], w0_ref[...], preferred_element_type=F32)
         + jnp.dot(a1_ref[...], w1_ref[...], preferred_element_type=F32))
    xn_ref[...] = _rms_norm(x, g_ref[...]).astype(BF16)
    o_ref[...] = x

    def up_project(c):
        slot = c % 2
        for part in range(2):
            cols = slice(part * D_FF + c * fc, part * D_FF + (c + 1) * fc)
            ext_ref[slot, part, 0:CONV_HALO, :] = tail_ref[part, :, c * fc:(c + 1) * fc]
            ext_ref[slot, part, CONV_HALO:CONV_HALO + tm, :] = jnp.dot(
                xn_ref[...], wup_ref[:, cols], preferred_element_type=F32)

    def conv_gate(c):
        slot = c % 2
        y = []
        for part in range(2):
            cols = slice(part * D_FF + c * fc, part * D_FF + (c + 1) * fc)
            w = cw_ref[:, cols]
            taps = [w[k:k + 1, :] * ext_ref[slot, part, CONV_HALO - 2 + k:CONV_HALO - 2 + k + tm, :]
                    for k in range(3)]
            y.append(taps[2] + taps[1] + taps[0])
            tail_ref[part, :, c * fc:(c + 1) * fc] = ext_ref[slot, part, tm:tm + CONV_HALO, :]
        half_g = 0.5 * y[0]
        act_ref[:, c * fc:(c + 1) * fc] = (
            (half_g + half_g * jnp.tanh(half_g)) * y[1]).astype(BF16)

    def down_project(rows):
        o_ref[...] += jnp.dot(act_ref[:, rows], wd_ref[rows, :], preferred_element_type=F32)

    last = n_chunks - 1
    up_project(0)
    for c in range(last):
        up_project(c + 1)
        conv_gate(c)
    down_project(slice(0, last * fc))
    conv_gate(last)
    down_project(slice(last * fc, D_FF))


def _mixer_out_ffn(h, mix, w_mix, g, w_up, conv_w, w_down):
    b, s, d = h.shape
    tm = TM_FFN
    fc = FF_CHUNK
    half = d // 2
    (a0, blk0), (a1, blk1) = mix
    w0, w1 = w_mix[:half], w_mix[half:]

    def half_spec(blk):
        return pl.BlockSpec((None, tm, half), lambda bi, i: (bi, i, blk))

    return pl.pallas_call(
        functools.partial(_ffn_kernel, tm=tm, fc=fc),
        grid=(b, s // tm),
        in_specs=[
            pl.BlockSpec((None, tm, d), lambda bi, i: (bi, i, 0)),
            half_spec(blk0), half_spec(blk1), _resident(w0.shape), _resident(w1.shape),
            _resident(g.shape), _resident(w_up.shape), _resident(conv_w.shape),
            _resident(w_down.shape),
        ],
        out_specs=pl.BlockSpec((None, tm, d), lambda bi, i: (bi, i, 0)),
        out_shape=jax.ShapeDtypeStruct(h.shape, F32),
        scratch_shapes=[pltpu.VMEM((tm, d), BF16),
                        pltpu.VMEM((2, 2, tm + CONV_HALO, fc), F32),
                        pltpu.VMEM((2, CONV_HALO, D_FF), F32),
                        pltpu.VMEM((tm, D_FF), BF16)],
        compiler_params=_params(("arbitrary", "arbitrary")),
        name="conv_ffn",
    )(h, a0, a1, w0, w1, g, w_up, conv_w, w_down)


def kernel(x, attn_norm, attn_w_in, attn_f_bias, fox_q_gain, fox_k_gain, sb_q_gain, sb_k_gain,
           attn_w_out, conv_norm, conv_w_in, conv_kernel, conv_w_out, ffn_norm, ffn_w_up,
           ffn_conv, ffn_w_down):
    depth = ffn_norm.shape[0]
    h = x
    for layer in range(depth):
        i = layer // 2
        if layer % 2 == 0:
            w_in = attn_w_in[i]
            w_qk = w_in[:, :2 * MIX_WIDTH].astype(BF16)
            wvt = w_in[:, 2 * MIX_WIDTH:3 * MIX_WIDTH].T.astype(BF16)
            wf = jnp.zeros((D_MODEL, LANES), BF16).at[:, :H_FOX].set(
                w_in[:, 3 * MIX_WIDTH:].astype(BF16))
            fb = jnp.zeros((1, LANES), F32).at[0, :H_FOX].set(attn_f_bias[i])
            qk_gain = jnp.concatenate(
                [jnp.tile(fox_q_gain[i], H_FOX), jnp.tile(sb_q_gain[i], H_SB),
                 jnp.tile(fox_k_gain[i], H_FOX), jnp.tile(sb_k_gain[i], H_SB)])[None, :]
            q, k, vt, cf = _attn_inproj(h, attn_norm[i][None, :], w_qk, wvt, wf, fb, qk_gain)
            o_fox, o_sb = _attention(q, k, vt, cf)
            mix = ((o_fox, 0), (o_sb, 0))
            w_mix = attn_w_out[i]
        else:
            y = _conv_mixer(h, conv_norm[i][None, :], conv_w_in[i].astype(BF16), conv_kernel[i])
            mix = ((y, 0), (y, 1))
            w_mix = conv_w_out[i]
        h = _mixer_out_ffn(h, mix, w_mix.astype(BF16), ffn_norm[layer][None, :],
                           ffn_w_up[layer].astype(BF16), ffn_conv[layer],
                           ffn_w_down[layer].astype(BF16))
    return h
```

```python
import functools

import jax
import jax.numpy as jnp
from jax import lax
from jax.experimental import pallas as pl
from jax.experimental.pallas import tpu as pltpu

F32 = jnp.float32
BF16 = jnp.bfloat16

D_MODEL = 1024
HEAD_DIM = 64
H_FOX = 8
H_SB = 8
MIX_WIDTH = (H_FOX + H_SB) * HEAD_DIM
D_FF = 2816
EPS = 1e-6
QK_SCALE = HEAD_DIM ** -0.5
LOG2E = 1.4426950408889634

LANES = 128
SUBLANES = 8
HEADS_PER_BLOCK = LANES // HEAD_DIM
BF16_ROWS = 2 * SUBLANES
MXU_DIM = 256
EXP2_CLAMP = 126.0
VMEM_LIMIT = 56 * 1024 * 1024

TM_PROJ = 256
TM_FFN = 512
FF_CHUNK = 256
DOWN_TAIL = 2
T_FOX = 512
T_SB = 256
SB_HEADS_PER_STEP = 8
SB_EXIT = 160.0
FOX_EXIT = 160.0
NORM_SLACK = 1.01
CONV_HALO = SUBLANES


def _params(sem):
    return pltpu.CompilerParams(dimension_semantics=sem, vmem_limit_bytes=VMEM_LIMIT)


def _resident(shape):
    zeros = (0,) * len(shape)
    return pl.BlockSpec(shape, lambda *_: zeros, pipeline_mode=pl.Buffered(1))


def _rms_norm(x, g):
    ms = jnp.mean(x * x, axis=-1, keepdims=True)
    return x * lax.rsqrt(ms + EPS) * g


def _split3(x):
    hi = x.astype(BF16)
    r = x - hi.astype(F32)
    mid = r.astype(BF16)
    lo = (r - mid.astype(F32)).astype(BF16)
    return hi, mid, lo


def _dot_nt(a, b):
    return lax.dot_general(a, b, (((1,), (1,)), ((), ())), preferred_element_type=F32)


def _causal_conv3(ext_ref, tail, cur, w, tm):
    ext_ref[0:CONV_HALO, :] = tail
    ext_ref[CONV_HALO:CONV_HALO + tm, :] = cur
    return (w[2:3, :] * cur
            + w[1:2, :] * ext_ref[CONV_HALO - 1:CONV_HALO - 1 + tm, :]
            + w[0:1, :] * ext_ref[CONV_HALO - 2:CONV_HALO - 2 + tm, :])


def _attn_inproj_kernel(h_ref, g_ref, w_ref, wvt_ref, wf_ref, fb_ref, qkg_ref, gmat_ref,
                        tri_ref, q_ref, k_ref, vt_ref, cf_ref, carry_ref, *, tm):
    i = pl.program_id(1)

    @pl.when(i == 0)
    def _():
        carry_ref[...] = jnp.zeros_like(carry_ref)

    xn = _rms_norm(h_ref[...], g_ref[...]).astype(BF16)
    gmat = gmat_ref[...]

    n_blk = MIX_WIDTH // MXU_DIM
    proj = [jnp.dot(xn, w_ref[:, c * MXU_DIM:(c + 1) * MXU_DIM], preferred_element_type=F32)
            for c in range(2 * n_blk)]
    vt_ref[...] = _dot_nt(wvt_ref[...], xn).astype(BF16)
    fl = jnp.dot(xn, wf_ref[...], preferred_element_type=F32) + fb_ref[...]

    for c in range(2 * n_blk):
        out_ref, scale = (q_ref, QK_SCALE * LOG2E) if c < n_blk else (k_ref, 1.0)
        t = proj[c]
        ssq = jnp.dot((t * t).astype(BF16), gmat, preferred_element_type=F32)
        gain = qkg_ref[:, c * MXU_DIM:(c + 1) * MXU_DIM]
        tn = t * lax.rsqrt(ssq * (1.0 / HEAD_DIM) + EPS) * gain
        if scale != 1.0:
            tn = tn * scale
        lo = (c % n_blk) * MXU_DIM
        out_ref[:, lo:lo + MXU_DIM] = tn.astype(BF16)

    log_f = jnp.minimum(fl, 0.0) - jnp.log1p(jnp.exp(-jnp.abs(fl)))
    tri = tri_ref[...]
    hi, mid, lo = _split3(log_f)
    cs = (jnp.dot(tri, hi, preferred_element_type=F32)
          + jnp.dot(tri, mid, preferred_element_type=F32)
          + jnp.dot(tri, lo, preferred_element_type=F32))
    cf = cs + carry_ref[0:1, :]
    carry_ref[...] = jnp.broadcast_to(cf[tm - 1:tm, :], carry_ref.shape)
    cf_ref[...] = cf


def _attn_inproj(h, g, w_qk, wvt, wf, fb, qk_gain):
    b, s, d = h.shape
    tm = TM_PROJ
    r = lax.broadcasted_iota(jnp.int32, (MXU_DIM, MXU_DIM), 0) // HEAD_DIM
    c = lax.broadcasted_iota(jnp.int32, (MXU_DIM, MXU_DIM), 1) // HEAD_DIM
    gmat = (r == c).astype(BF16)
    rr = lax.broadcasted_iota(jnp.int32, (tm, tm), 0)
    cc = lax.broadcasted_iota(jnp.int32, (tm, tm), 1)
    tri = (cc <= rr).astype(BF16)
    row_spec = pl.BlockSpec((None, tm, MIX_WIDTH), lambda bi, i: (bi, i, 0))
    return pl.pallas_call(
        functools.partial(_attn_inproj_kernel, tm=tm),
        grid=(b, s // tm),
        in_specs=[
            pl.BlockSpec((None, tm, d), lambda bi, i: (bi, i, 0)),
            _resident(g.shape), _resident(w_qk.shape), _resident(wvt.shape),
            _resident(wf.shape), _resident(fb.shape), _resident(qk_gain.shape),
            _resident(gmat.shape), _resident(tri.shape),
        ],
        out_specs=[row_spec, row_spec,
                   pl.BlockSpec((None, MIX_WIDTH, tm), lambda bi, i: (bi, 0, i)),
                   pl.BlockSpec((None, tm, LANES), lambda bi, i: (bi, i, 0))],
        out_shape=[jax.ShapeDtypeStruct((b, s, MIX_WIDTH), BF16),
                   jax.ShapeDtypeStruct((b, s, MIX_WIDTH), BF16),
                   jax.ShapeDtypeStruct((b, MIX_WIDTH, s), BF16),
                   jax.ShapeDtypeStruct((b, s, LANES), F32)],
        scratch_shapes=[pltpu.VMEM((SUBLANES, LANES), F32)],
        compiler_params=_params(("arbitrary", "arbitrary")),
        name="attn_inproj",
    )(h, g, w_qk, wvt, wf, fb, qk_gain, gmat, tri)


def _head_masks(t, n_heads=HEADS_PER_BLOCK):
    lane = lax.broadcasted_iota(jnp.int32, (t, n_heads * HEAD_DIM), 1)
    return [(lane >= HEAD_DIM * hh) & (lane < HEAD_DIM * (hh + 1)) for hh in range(n_heads)]


def _lane_pieces(t):
    return [slice(c * LANES, (c + 1) * LANES) for c in range(t // LANES)]


def _sweep_key_tiles(i, scores, step, more_needed, test_every_tile):
    scores(i, 0)
    step(i, 0, True, (jnp.maximum(i - 1, 0), 1))

    def unfinished(carry):
        r, go = carry
        return jnp.logical_and(r < i // 2, go > 0)

    def pair(carry):
        r, _ = carry
        j = i - 1 - 2 * r
        step(j, 1, False, (jnp.maximum(j - 1, 0), 0))
        if not test_every_tile:
            step(j - 1, 0, False, (jnp.maximum(j - 2, 0), 1))
            return r + 1, more_needed(jnp.maximum(j - 2, 0))
        go_mid = more_needed(j - 1)

        @pl.when(go_mid > 0)
        def _():
            step(j - 1, 0, False, (jnp.maximum(j - 2, 0), 1))

        return r + 1, go_mid & more_needed(jnp.maximum(j - 2, 0))

    _, go = lax.while_loop(unfinished, pair,
                           (jnp.int32(0), more_needed(jnp.maximum(i - 1, 0))))

    @pl.when(jnp.logical_and(i % 2 == 1, go > 0))
    def _():
        step(0, 1, False, None)


def _write_heads(o_ref, outs_t):
    o_ref[...] = jnp.concatenate(outs_t, axis=0).T.astype(o_ref.dtype)


def _fox_parts(q_ref, k_ref, vt_ref, cfb_ref, st_ref, m_ref, acc_ref, t):
    q = q_ref[...]
    qh = [jnp.where(mk, q, jnp.zeros_like(q)) for mk in _head_masks(t)]
    m_ref[...] = jnp.full_like(m_ref, -jnp.inf)
    acc_ref[...] = jnp.zeros_like(acc_ref)
    key = lax.broadcasted_iota(jnp.int32, (t, LANES), 0)
    qry = lax.broadcasted_iota(jnp.int32, (t, LANES), 1)
    pieces = _lane_pieces(t)
    ones_rows = jnp.ones((BF16_ROWS, t), BF16)

    def scores(j, slot, hh):
        off = pl.multiple_of(j * t, t)
        st = _dot_nt(k_ref[pl.ds(off, t), :], qh[hh])
        fcol = cfb_ref[hh, pl.ds(off, t), :]
        for cols in pieces:
            st_ref[slot, hh, :, cols] = st[:, cols] - fcol

    def consume(j, slot, masked, hh):
        off = pl.multiple_of(j * t, t)
        m_old = m_ref[hh]
        m_new, alpha, prob = [], [], []
        for c, cols in enumerate(pieces):
            x = st_ref[slot, hh, :, cols]
            if masked:
                x = jnp.where(key <= qry + c * LANES, x, -jnp.inf)
            m_o = m_old[:, cols]
            m_n = jnp.maximum(m_o, jnp.max(x, axis=0, keepdims=True))
            m_new.append(m_n)
            alpha.append(jnp.exp2(m_o - m_n))
            prob.append(jnp.exp2(x - m_n).astype(BF16))
        vt_h = jnp.concatenate(
            [vt_ref[hh * HEAD_DIM:(hh + 1) * HEAD_DIM, pl.ds(off, t)], ones_rows], axis=0)
        pv = jnp.dot(vt_h, jnp.concatenate(prob, axis=1), preferred_element_type=F32)
        acc_ref[hh] = jnp.concatenate(alpha, axis=1) * acc_ref[hh] + pv
        m_ref[hh] = jnp.concatenate(m_new, axis=1)

    def result():
        return [acc_ref[hh, 0:HEAD_DIM, :] / acc_ref[hh, HEAD_DIM:HEAD_DIM + 1, :]
                for hh in range(HEADS_PER_BLOCK)]

    return scores, consume, result


def _sb_parts(q_ref, k_ref, vt_ref, tri_ref, z_ref, r_ref, acc_ref, t, n_heads):
    q = q_ref[...]
    qh = [jnp.where(mk, q, jnp.zeros_like(q)) for mk in _head_masks(t, n_heads)]
    r_ref[...] = jnp.zeros_like(r_ref)
    acc_ref[...] = jnp.zeros_like(acc_ref)
    key = lax.broadcasted_iota(jnp.int32, (t, t), 0)
    qry = lax.broadcasted_iota(jnp.int32, (t, t), 1)
    strict = key < qry
    n_sub = t // MXU_DIM

    sub_rows = [slice(c * MXU_DIM, (c + 1) * MXU_DIM) for c in range(n_sub)]

    def scores(j, slot, hh):
        off = pl.multiple_of(j * t, t)
        z_ref[slot, hh] = _dot_nt(k_ref[pl.ds(off, t), :], qh[hh])

    def gate_sums(slot, masked, hh):
        z = z_ref[slot, hh]
        a = jnp.maximum(LOG2E * jnp.log(1.0 + jnp.exp2(jnp.minimum(z, EXP2_CLAMP))), z)
        base = z - a
        if masked:
            a = jnp.where(strict, a, 0.0)
            base = jnp.where(strict, base, -jnp.inf)
        tri = tri_ref[...]
        sums = [jnp.dot(tri, a[rows, :].astype(BF16), preferred_element_type=F32)
                for rows in sub_rows]
        return base, sums

    def accumulate(j, staged, hh):
        off = pl.multiple_of(j * t, t)
        base, sums = staged
        r_run = r_ref[hh]
        w = [None] * n_sub
        for c in reversed(range(n_sub)):
            later = sums[c][0:MXU_DIM, :]
            w[c] = jnp.exp2(base[sub_rows[c], :] - later - r_run).astype(BF16)
            r_run = r_run + sums[c][MXU_DIM:MXU_DIM + 1, :]
        vt_h = vt_ref[hh * HEAD_DIM:(hh + 1) * HEAD_DIM, pl.ds(off, t)]
        acc_ref[hh] += jnp.dot(vt_h, jnp.concatenate(w, axis=0), preferred_element_type=F32)
        r_ref[hh] = r_run

    def result():
        return [acc_ref[hh] for hh in range(n_heads)]

    return scores, gate_sums, accumulate, result


def _fox_kernel(q_ref, k_ref, vt_ref, cf_ref, o_ref, cfb_ref, kmax_ref, st_ref, m_ref, acc_ref,
                *, t):
    p = pl.program_id(1)
    i = pl.program_id(2)
    heads = range(HEADS_PER_BLOCK)
    lane_row = lax.broadcasted_iota(jnp.int32, (LANES, LANES), 0)
    head_rows = [((lane_row >= HEAD_DIM * hh) & (lane_row < HEAD_DIM * (hh + 1))).astype(BF16)
                 for hh in heads]

    @pl.when(i == 0)
    def _():
        for hh in heads:
            sel = (lane_row == p * HEADS_PER_BLOCK + hh).astype(BF16)

            def fill(r, kmax2):
                off = pl.multiple_of(r * t, t)
                hi, mid, lo = _split3(cf_ref[pl.ds(off, t), :])
                cfb_ref[hh, pl.ds(off, t), :] = LOG2E * (
                    jnp.dot(hi, sel, preferred_element_type=F32)
                    + jnp.dot(mid, sel, preferred_element_type=F32)
                    + jnp.dot(lo, sel, preferred_element_type=F32))
                kt = k_ref[pl.ds(off, t), :].astype(F32)
                ksq = jnp.dot((kt * kt).astype(BF16), head_rows[hh],
                              preferred_element_type=F32)
                return jnp.maximum(kmax2, ksq)

            kmax2 = lax.fori_loop(0, cf_ref.shape[0] // t, fill, jnp.zeros((t, LANES), F32))
            kmax_ref[hh] = jnp.max(kmax2, axis=0, keepdims=True)

    scores_h, consume_h, result = _fox_parts(q_ref, k_ref, vt_ref, cfb_ref, st_ref, m_ref,
                                             acc_ref, t)

    q32 = q_ref[...].astype(F32)
    ones_lhs = jnp.ones((BF16_ROWS, LANES), BF16)
    qk_bound = []
    for hh, mk in zip(heads, _head_masks(t)):
        qsq = jnp.where(mk, q32 * q32, 0.0).astype(BF16)
        qn2 = _dot_nt(ones_lhs, qsq)[0:1, :]
        qk_bound.append(NORM_SLACK * jnp.sqrt(qn2 * kmax_ref[hh][:, 0:1]))

    def more_needed(j):
        row = (j + 1) * t - 1
        need = jnp.int32(0)
        for hh in heads:
            f_hi = cfb_ref[hh, pl.ds(row, 1), :][:, 0:1]
            gap = jnp.max(qk_bound[hh] - f_hi - m_ref[hh])
            need = need | jnp.logical_not(gap <= -FOX_EXIT).astype(jnp.int32)
        return need

    def scores(j, slot):
        for hh in heads:
            scores_h(j, slot, hh)

    def step(j, slot, masked, nxt):
        for hh in heads:
            if nxt is not None:
                scores_h(*nxt, hh)
            consume_h(j, slot, masked, hh)

    _sweep_key_tiles(i, scores, step, more_needed, test_every_tile=False)
    _write_heads(o_ref, result())


def _sb_kernel(q_ref, k_ref, vt_ref, tri_ref, o_ref, z_ref, r_ref, acc_ref, *, t, n_heads):
    i = pl.program_id(2)
    scores_h, gate_sums_h, accumulate_h, result = _sb_parts(
        q_ref, k_ref, vt_ref, tri_ref, z_ref, r_ref, acc_ref, t, n_heads)
    heads = range(n_heads)

    def scores(j, slot):
        for hh in heads:
            scores_h(j, slot, hh)

    def step(j, slot, masked, nxt):
        staged = []
        for hh in heads:
            if nxt is not None:
                scores_h(*nxt, hh)
            staged.append(gate_sums_h(slot, masked, hh))
        for hh in heads:
            accumulate_h(j, staged[hh], hh)

    def more_needed(_):
        return (jnp.min(r_ref[...]) < SB_EXIT).astype(jnp.int32)

    _sweep_key_tiles(i, scores, step, more_needed, test_every_tile=True)
    _write_heads(o_ref, result())


def _attention(q, k, vt, cf):
    b, s, _ = q.shape
    rr = lax.broadcasted_iota(jnp.int32, (MXU_DIM + BF16_ROWS, MXU_DIM), 0)
    cc = lax.broadcasted_iota(jnp.int32, (MXU_DIM + BF16_ROWS, MXU_DIM), 1)
    tri = ((cc > rr) | (rr >= MXU_DIM)).astype(BF16)

    def specs(t, w, first):
        return [pl.BlockSpec((None, t, w), lambda bi, p, i: (bi, i, p + first)),
                pl.BlockSpec((None, s, w), lambda bi, p, i: (bi, 0, p + first)),
                pl.BlockSpec((None, w, s), lambda bi, p, i: (bi, p + first, 0))]

    def out_spec(t, w):
        return pl.BlockSpec((None, t, w), lambda bi, p, i: (bi, i, p))

    params = _params(("arbitrary", "arbitrary", "arbitrary"))
    t = T_FOX
    o_fox = pl.pallas_call(
        functools.partial(_fox_kernel, t=t),
        grid=(b, H_FOX // HEADS_PER_BLOCK, s // t),
        in_specs=specs(t, LANES, 0) + [
            pl.BlockSpec((None, s, LANES), lambda bi, p, i: (bi, 0, 0),
                         pipeline_mode=pl.Buffered(1))],
        out_specs=out_spec(t, LANES),
        out_shape=jax.ShapeDtypeStruct((b, s, H_FOX * HEAD_DIM), BF16),
        scratch_shapes=[pltpu.VMEM((HEADS_PER_BLOCK, s, LANES), F32),
                        pltpu.VMEM((HEADS_PER_BLOCK, 1, LANES), F32),
                        pltpu.VMEM((2, HEADS_PER_BLOCK, t, t), F32),
                        pltpu.VMEM((HEADS_PER_BLOCK, 1, t), F32),
                        pltpu.VMEM((HEADS_PER_BLOCK, HEAD_DIM + BF16_ROWS, t), F32)],
        compiler_params=params,
        name="fox_attention",
    )(q, k, vt, cf)
    t = T_SB
    nh = SB_HEADS_PER_STEP
    w = nh * HEAD_DIM
    o_sb = pl.pallas_call(
        functools.partial(_sb_kernel, t=t, n_heads=nh),
        grid=(b, H_SB // nh, s // t),
        in_specs=specs(t, w, H_FOX // nh) + [_resident(tri.shape)],
        out_specs=out_spec(t, w),
        out_shape=jax.ShapeDtypeStruct((b, s, H_SB * HEAD_DIM), BF16),
        scratch_shapes=[pltpu.VMEM((2, nh, t, t), F32),
                        pltpu.VMEM((nh, 1, t), F32),
                        pltpu.VMEM((nh, HEAD_DIM, t), F32)],
        compiler_params=params,
        name="sb_attention",
    )(q, k, vt, tri)
    return o_fox, o_sb


def _conv_mixer_kernel(h_ref, g_ref, win_ref, cw_ref, o_ref, ext_ref, tail_ref, *, tm):
    i = pl.program_id(1)

    @pl.when(i == 0)
    def _():
        tail_ref[...] = jnp.zeros_like(tail_ref)

    x = h_ref[...]
    xn = _rms_norm(x, g_ref[...]).astype(BF16)
    d = x.shape[-1]
    gate_c = jnp.dot(xn, win_ref[:, d:2 * d], preferred_element_type=F32)
    u = jnp.dot(xn, win_ref[:, 2 * d:3 * d], preferred_element_type=F32)
    cu = gate_c * u
    y = _causal_conv3(ext_ref, tail_ref[...], cu, cw_ref[...], tm)
    tail_ref[...] = cu[tm - CONV_HALO:tm, :]
    gate_b = jnp.dot(xn, win_ref[:, 0:d], preferred_element_type=F32)
    o_ref[...] = (gate_b * y).astype(o_ref.dtype)


def _conv_mixer(h, g, w_in, conv_w):
    b, s, d = h.shape
    tm = TM_PROJ
    return pl.pallas_call(
        functools.partial(_conv_mixer_kernel, tm=tm),
        grid=(b, s // tm),
        in_specs=[
            pl.BlockSpec((None, tm, d), lambda bi, i: (bi, i, 0)),
            _resident(g.shape), _resident(w_in.shape), _resident(conv_w.shape),
        ],
        out_specs=pl.BlockSpec((None, tm, d), lambda bi, i: (bi, i, 0)),
        out_shape=jax.ShapeDtypeStruct(h.shape, BF16),
        scratch_shapes=[pltpu.VMEM((tm + CONV_HALO, d), F32),
                        pltpu.VMEM((CONV_HALO, d), F32)],
        compiler_params=_params(("arbitrary", "arbitrary")),
        name="conv_mixer",
    )(h, g, w_in, conv_w)


def _ffn_kernel(h_ref, a0_ref, a1_ref, w0_ref, w1_ref, g_ref, wup_ref, cw_ref, wd_ref, o_ref,
                xn_ref, ext_ref, tail_ref, act_ref, *, tm, fc):
    i = pl.program_id(1)
    n_chunks = D_FF // fc

    @pl.when(i == 0)
    def _():
        tail_ref[...] = jnp.zeros_like(tail_ref)

    x = (h_ref[...]
         + jnp.dot(a0_ref[...], w0_ref[...], preferred_element_type=F32)
         + jnp.dot(a1_ref[...], w1_ref[...], preferred_element_type=F32))
    xn_ref[...] = _rms_norm(x, g_ref[...]).astype(BF16)
    o_ref[...] = x

    def up_project(c):
        slot = c % 2
        for part in range(2):
            cols = slice(part * D_FF + c * fc, part * D_FF + (c + 1) * fc)
            ext_ref[slot, part, 0:CONV_HALO, :] = tail_ref[part, :, c * fc:(c + 1) * fc]
            ext_ref[slot, part, CONV_HALO:CONV_HALO + tm, :] = jnp.dot(
                xn_ref[...], wup_ref[:, cols], preferred_element_type=F32)

    def conv_gate(c):
        slot = c % 2
        y = []
        for part in range(2):
            cols = slice(part * D_FF + c * fc, part * D_FF + (c + 1) * fc)
            w = cw_ref[:, cols]
            taps = [w[k:k + 1, :] * ext_ref[slot, part, CONV_HALO - 2 + k:CONV_HALO - 2 + k + tm, :]
                    for k in range(3)]
            y.append(taps[2] + taps[1] + taps[0])
            tail_ref[part, :, c * fc:(c + 1) * fc] = ext_ref[slot, part, tm:tm + CONV_HALO, :]
        half_g = 0.5 * y[0]
        act_ref[:, c * fc:(c + 1) * fc] = (
            (half_g + half_g * jnp.tanh(half_g)) * y[1]).astype(BF16)

    def down_project(rows):
        o_ref[...] += jnp.dot(act_ref[:, rows], wd_ref[rows, :], preferred_element_type=F32)

    head = n_chunks - DOWN_TAIL
    up_project(0)
    for c in range(n_chunks):
        if c + 1 < n_chunks:
            up_project(c + 1)
        if c == head:
            down_project(slice(0, head * fc))
        conv_gate(c)
    down_project(slice(head * fc, D_FF))


def _mixer_out_ffn(h, mix, w_mix, g, w_up, conv_w, w_down):
    b, s, d = h.shape
    tm = TM_FFN
    fc = FF_CHUNK
    half = d // 2
    (a0, blk0), (a1, blk1) = mix
    w0, w1 = w_mix[:half], w_mix[half:]

    def half_spec(blk):
        return pl.BlockSpec((None, tm, half), lambda bi, i: (bi, i, blk))

    return pl.pallas_call(
        functools.partial(_ffn_kernel, tm=tm, fc=fc),
        grid=(b, s // tm),
        in_specs=[
            pl.BlockSpec((None, tm, d), lambda bi, i: (bi, i, 0)),
            half_spec(blk0), half_spec(blk1), _resident(w0.shape), _resident(w1.shape),
            _resident(g.shape), _resident(w_up.shape), _resident(conv_w.shape),
            _resident(w_down.shape),
        ],
        out_specs=pl.BlockSpec((None, tm, d), lambda bi, i: (bi, i, 0)),
        out_shape=jax.ShapeDtypeStruct(h.shape, F32),
        scratch_shapes=[pltpu.VMEM((tm, d), BF16),
                        pltpu.VMEM((2, 2, tm + CONV_HALO, fc), F32),
                        pltpu.VMEM((2, CONV_HALO, D_FF), F32),
                        pltpu.VMEM((tm, D_FF), BF16)],
        compiler_params=_params(("arbitrary", "arbitrary")),
        name="conv_ffn",
    )(h, a0, a1, w0, w1, g, w_up, conv_w, w_down)


def kernel(x, attn_norm, attn_w_in, attn_f_bias, fox_q_gain, fox_k_gain, sb_q_gain, sb_k_gain,
           attn_w_out, conv_norm, conv_w_in, conv_kernel, conv_w_out, ffn_norm, ffn_w_up,
           ffn_conv, ffn_w_down):
    depth = ffn_norm.shape[0]
    h = x
    for layer in range(depth):
        i = layer // 2
        if layer % 2 == 0:
            w_in = attn_w_in[i]
            w_qk = w_in[:, :2 * MIX_WIDTH].astype(BF16)
            wvt = w_in[:, 2 * MIX_WIDTH:3 * MIX_WIDTH].T.astype(BF16)
            wf = jnp.zeros((D_MODEL, LANES), BF16).at[:, :H_FOX].set(
                w_in[:, 3 * MIX_WIDTH:].astype(BF16))
            fb = jnp.zeros((1, LANES), F32).at[0, :H_FOX].set(attn_f_bias[i])
            qk_gain = jnp.concatenate(
                [jnp.tile(fox_q_gain[i], H_FOX), jnp.tile(sb_q_gain[i], H_SB),
                 jnp.tile(fox_k_gain[i], H_FOX), jnp.tile(sb_k_gain[i], H_SB)])[None, :]
            q, k, vt, cf = _attn_inproj(h, attn_norm[i][None, :], w_qk, wvt, wf, fb, qk_gain)
            o_fox, o_sb = _attention(q, k, vt, cf)
            mix = ((o_fox, 0), (o_sb, 0))
            w_mix = attn_w_out[i]
        else:
            y = _conv_mixer(h, conv_norm[i][None, :], conv_w_in[i].astype(BF16), conv_kernel[i])
            mix = ((y, 0), (y, 1))
            w_mix = conv_w_out[i]
        h = _mixer_out_ffn(h, mix, w_mix.astype(BF16), ffn_norm[layer][None, :],
                           ffn_w_up[layer].astype(BF16), ffn_conv[layer],
                           ffn_w_down[layer].astype(BF16))
    return h
```

```python
import functools

import jax
import jax.numpy as jnp
from jax import lax
from jax.experimental import pallas as pl
from jax.experimental.pallas import tpu as pltpu

F32 = jnp.float32
BF16 = jnp.bfloat16

D_MODEL = 1024
HEAD_DIM = 64
H_FOX = 8
H_SB = 8
MIX_WIDTH = (H_FOX + H_SB) * HEAD_DIM
D_FF = 2816
EPS = 1e-6
QK_SCALE = HEAD_DIM ** -0.5
LOG2E = 1.4426950408889634

LANES = 128
SUBLANES = 8
HEADS_PER_BLOCK = LANES // HEAD_DIM
BF16_ROWS = 2 * SUBLANES
MXU_DIM = 256
EXP2_CLAMP = 126.0
VMEM_LIMIT = 56 * 1024 * 1024

TM_PROJ = 256
TM_FFN = 512
FF_CHUNK = 256
DOWN_TAIL = 2
T_FOX = 512
T_SB = 256
SB_HEADS_PER_STEP = 8
SB_EXIT = 160.0
FOX_EXIT = 160.0
NORM_SLACK = 1.01
CONV_HALO = SUBLANES


def _params(sem):
    return pltpu.CompilerParams(dimension_semantics=sem, vmem_limit_bytes=VMEM_LIMIT)


def _resident(shape):
    zeros = (0,) * len(shape)
    return pl.BlockSpec(shape, lambda *_: zeros, pipeline_mode=pl.Buffered(1))


def _rms_norm(x, g):
    ms = jnp.mean(x * x, axis=-1, keepdims=True)
    return x * lax.rsqrt(ms + EPS) * g


def _split3(x):
    hi = x.astype(BF16)
    r = x - hi.astype(F32)
    mid = r.astype(BF16)
    lo = (r - mid.astype(F32)).astype(BF16)
    return hi, mid, lo


def _dot_nt(a, b):
    return lax.dot_general(a, b, (((1,), (1,)), ((), ())), preferred_element_type=F32)


def _causal_conv3(ext_ref, tail, cur, w, tm):
    ext_ref[0:CONV_HALO, :] = tail
    ext_ref[CONV_HALO:CONV_HALO + tm, :] = cur
    return (w[2:3, :] * cur
            + w[1:2, :] * ext_ref[CONV_HALO - 1:CONV_HALO - 1 + tm, :]
            + w[0:1, :] * ext_ref[CONV_HALO - 2:CONV_HALO - 2 + tm, :])


def _attn_inproj_kernel(h_ref, g_ref, w_ref, wvt_ref, wf_ref, fb_ref, qkg_ref, gmat_ref,
                        tri_ref, q_ref, k_ref, vt_ref, cf_ref, carry_ref, *, tm):
    i = pl.program_id(1)

    @pl.when(i == 0)
    def _():
        carry_ref[...] = jnp.zeros_like(carry_ref)

    xn = _rms_norm(h_ref[...], g_ref[...]).astype(BF16)
    gmat = gmat_ref[...]

    n_blk = MIX_WIDTH // MXU_DIM
    proj = [jnp.dot(xn, w_ref[:, c * MXU_DIM:(c + 1) * MXU_DIM], preferred_element_type=F32)
            for c in range(2 * n_blk)]
    vt_ref[...] = _dot_nt(wvt_ref[...], xn).astype(BF16)
    fl = jnp.dot(xn, wf_ref[...], preferred_element_type=F32) + fb_ref[...]

    for c in range(2 * n_blk):
        out_ref, scale = (q_ref, QK_SCALE * LOG2E) if c < n_blk else (k_ref, 1.0)
        t = proj[c]
        ssq = jnp.dot((t * t).astype(BF16), gmat, preferred_element_type=F32)
        gain = qkg_ref[:, c * MXU_DIM:(c + 1) * MXU_DIM]
        tn = t * lax.rsqrt(ssq * (1.0 / HEAD_DIM) + EPS) * gain
        if scale != 1.0:
            tn = tn * scale
        lo = (c % n_blk) * MXU_DIM
        out_ref[:, lo:lo + MXU_DIM] = tn.astype(BF16)

    log_f = jnp.minimum(fl, 0.0) - jnp.log1p(jnp.exp(-jnp.abs(fl)))
    tri = tri_ref[...]
    hi, mid, lo = _split3(log_f)
    cs = (jnp.dot(tri, hi, preferred_element_type=F32)
          + jnp.dot(tri, mid, preferred_element_type=F32)
          + jnp.dot(tri, lo, preferred_element_type=F32))
    cf = cs + carry_ref[0:1, :]
    carry_ref[...] = jnp.broadcast_to(cf[tm - 1:tm, :], carry_ref.shape)
    cf_ref[...] = cf


def _attn_inproj(h, g, w_qk, wvt, wf, fb, qk_gain):
    b, s, d = h.shape
    tm = TM_PROJ
    r = lax.broadcasted_iota(jnp.int32, (MXU_DIM, MXU_DIM), 0) // HEAD_DIM
    c = lax.broadcasted_iota(jnp.int32, (MXU_DIM, MXU_DIM), 1) // HEAD_DIM
    gmat = (r == c).astype(BF16)
    rr = lax.broadcasted_iota(jnp.int32, (tm, tm), 0)
    cc = lax.broadcasted_iota(jnp.int32, (tm, tm), 1)
    tri = (cc <= rr).astype(BF16)
    row_spec = pl.BlockSpec((None, tm, MIX_WIDTH), lambda bi, i: (bi, i, 0))
    return pl.pallas_call(
        functools.partial(_attn_inproj_kernel, tm=tm),
        grid=(b, s // tm),
        in_specs=[
            pl.BlockSpec((None, tm, d), lambda bi, i: (bi, i, 0)),
            _resident(g.shape), _resident(w_qk.shape), _resident(wvt.shape),
            _resident(wf.shape), _resident(fb.shape), _resident(qk_gain.shape),
            _resident(gmat.shape), _resident(tri.shape),
        ],
        out_specs=[row_spec, row_spec,
                   pl.BlockSpec((None, MIX_WIDTH, tm), lambda bi, i: (bi, 0, i)),
                   pl.BlockSpec((None, tm, LANES), lambda bi, i: (bi, i, 0))],
        out_shape=[jax.ShapeDtypeStruct((b, s, MIX_WIDTH), BF16),
                   jax.ShapeDtypeStruct((b, s, MIX_WIDTH), BF16),
                   jax.ShapeDtypeStruct((b, MIX_WIDTH, s), BF16),
                   jax.ShapeDtypeStruct((b, s, LANES), F32)],
        scratch_shapes=[pltpu.VMEM((SUBLANES, LANES), F32)],
        compiler_params=_params(("arbitrary", "arbitrary")),
        name="attn_inproj",
    )(h, g, w_qk, wvt, wf, fb, qk_gain, gmat, tri)


def _head_masks(t, n_heads=HEADS_PER_BLOCK):
    lane = lax.broadcasted_iota(jnp.int32, (t, n_heads * HEAD_DIM), 1)
    return [(lane >= HEAD_DIM * hh) & (lane < HEAD_DIM * (hh + 1)) for hh in range(n_heads)]


def _lane_pieces(t):
    return [slice(c * LANES, (c + 1) * LANES) for c in range(t // LANES)]


def _sweep_key_tiles(i, scores, step, more_needed, test_every_tile):
    def clamp(j):
        return jnp.maximum(j, 0)

    @pl.when(i == 0)
    def _():
        scores(0, 0)
        step(0, 0, True, None)

    @pl.when(i > 0)
    def _():
        scores(i, 0)
        step(i, 0, True, (i - 1, 1))
        step(i - 1, 1, False, (clamp(i - 2), 0))
        rest = i - 1

        def unfinished(carry):
            r, go = carry
            return jnp.logical_and(r < rest // 2, go > 0)

        def pair(carry):
            r, _ = carry
            j = i - 2 - 2 * r
            step(j, 0, False, (clamp(j - 1), 1))
            if not test_every_tile:
                step(j - 1, 1, False, (clamp(j - 2), 0))
                return r + 1, more_needed(clamp(j - 2))
            go_mid = more_needed(j - 1)

            @pl.when(go_mid > 0)
            def _():
                step(j - 1, 1, False, (clamp(j - 2), 0))

            return r + 1, go_mid & more_needed(clamp(j - 2))

        _, go = lax.while_loop(unfinished, pair, (jnp.int32(0), more_needed(clamp(i - 2))))

        @pl.when(jnp.logical_and(rest % 2 == 1, go > 0))
        def _():
            step(0, 0, False, None)


def _write_heads(o_ref, outs_t):
    o_ref[...] = jnp.concatenate(outs_t, axis=0).T.astype(o_ref.dtype)


def _fox_parts(q_ref, k_ref, vt_ref, cfb_ref, st_ref, m_ref, acc_ref, t):
    q = q_ref[...]
    qh = [jnp.where(mk, q, jnp.zeros_like(q)) for mk in _head_masks(t)]
    m_ref[...] = jnp.full_like(m_ref, -jnp.inf)
    acc_ref[...] = jnp.zeros_like(acc_ref)
    key = lax.broadcasted_iota(jnp.int32, (t, LANES), 0)
    qry = lax.broadcasted_iota(jnp.int32, (t, LANES), 1)
    pieces = _lane_pieces(t)
    ones_rows = jnp.ones((BF16_ROWS, t), BF16)

    def scores(j, slot, hh):
        off = pl.multiple_of(j * t, t)
        st = _dot_nt(k_ref[pl.ds(off, t), :], qh[hh])
        fcol = cfb_ref[hh, pl.ds(off, t), :]
        for cols in pieces:
            st_ref[slot, hh, :, cols] = st[:, cols] - fcol

    def consume(j, slot, masked, hh):
        off = pl.multiple_of(j * t, t)
        m_old = m_ref[hh]
        m_new, alpha, prob = [], [], []
        for c, cols in enumerate(pieces):
            x = st_ref[slot, hh, :, cols]
            if masked:
                x = jnp.where(key <= qry + c * LANES, x, -jnp.inf)
            m_o = m_old[:, cols]
            m_n = jnp.maximum(m_o, jnp.max(x, axis=0, keepdims=True))
            m_new.append(m_n)
            alpha.append(jnp.exp2(m_o - m_n))
            prob.append(jnp.exp2(x - m_n).astype(BF16))
        vt_h = jnp.concatenate(
            [vt_ref[hh * HEAD_DIM:(hh + 1) * HEAD_DIM, pl.ds(off, t)], ones_rows], axis=0)
        pv = jnp.dot(vt_h, jnp.concatenate(prob, axis=1), preferred_element_type=F32)
        acc_ref[hh] = jnp.concatenate(alpha, axis=1) * acc_ref[hh] + pv
        m_ref[hh] = jnp.concatenate(m_new, axis=1)

    def result():
        return [acc_ref[hh, 0:HEAD_DIM, :] / acc_ref[hh, HEAD_DIM:HEAD_DIM + 1, :]
                for hh in range(HEADS_PER_BLOCK)]

    return scores, consume, result


def _sb_parts(q_ref, k_ref, vt_ref, tri_ref, z_ref, r_ref, acc_ref, t, n_heads):
    q = q_ref[...]
    qh = [jnp.where(mk, q, jnp.zeros_like(q)) for mk in _head_masks(t, n_heads)]
    r_ref[...] = jnp.zeros_like(r_ref)
    acc_ref[...] = jnp.zeros_like(acc_ref)
    key = lax.broadcasted_iota(jnp.int32, (t, t), 0)
    qry = lax.broadcasted_iota(jnp.int32, (t, t), 1)
    strict = key < qry
    n_sub = t // MXU_DIM

    sub_rows = [slice(c * MXU_DIM, (c + 1) * MXU_DIM) for c in range(n_sub)]

    def scores(j, slot, hh):
        off = pl.multiple_of(j * t, t)
        z_ref[slot, hh] = _dot_nt(k_ref[pl.ds(off, t), :], qh[hh])

    def gate_sums(slot, masked, hh):
        z = z_ref[slot, hh]
        a = jnp.maximum(LOG2E * jnp.log(1.0 + jnp.exp2(jnp.minimum(z, EXP2_CLAMP))), z)
        base = z - a
        if masked:
            a = jnp.where(strict, a, 0.0)
            base = jnp.where(strict, base, -jnp.inf)
        tri = tri_ref[...]
        sums = [jnp.dot(tri, a[rows, :].astype(BF16), preferred_element_type=F32)
                for rows in sub_rows]
        return base, sums

    def accumulate(j, staged, hh):
        off = pl.multiple_of(j * t, t)
        base, sums = staged
        r_run = r_ref[hh]
        w = [None] * n_sub
        for c in reversed(range(n_sub)):
            later = sums[c][0:MXU_DIM, :]
            w[c] = jnp.exp2(base[sub_rows[c], :] - later - r_run).astype(BF16)
            r_run = r_run + sums[c][MXU_DIM:MXU_DIM + 1, :]
        vt_h = vt_ref[hh * HEAD_DIM:(hh + 1) * HEAD_DIM, pl.ds(off, t)]
        acc_ref[hh] += jnp.dot(vt_h, jnp.concatenate(w, axis=0), preferred_element_type=F32)
        r_ref[hh] = r_run

    def result():
        return [acc_ref[hh] for hh in range(n_heads)]

    return scores, gate_sums, accumulate, result


def _fox_kernel(q_ref, k_ref, vt_ref, cf_ref, o_ref, cfb_ref, kmax_ref, st_ref, m_ref, acc_ref,
                *, t):
    p = pl.program_id(1)
    i = pl.program_id(2)
    heads = range(HEADS_PER_BLOCK)
    lane_row = lax.broadcasted_iota(jnp.int32, (LANES, LANES), 0)
    head_rows = [((lane_row >= HEAD_DIM * hh) & (lane_row < HEAD_DIM * (hh + 1))).astype(BF16)
                 for hh in heads]

    @pl.when(i == 0)
    def _():
        for hh in heads:
            sel = (lane_row == p * HEADS_PER_BLOCK + hh).astype(BF16)

            def fill(r, kmax2):
                off = pl.multiple_of(r * t, t)
                hi, mid, lo = _split3(cf_ref[pl.ds(off, t), :])
                cfb_ref[hh, pl.ds(off, t), :] = LOG2E * (
                    jnp.dot(hi, sel, preferred_element_type=F32)
                    + jnp.dot(mid, sel, preferred_element_type=F32)
                    + jnp.dot(lo, sel, preferred_element_type=F32))
                kt = k_ref[pl.ds(off, t), :].astype(F32)
                ksq = jnp.dot((kt * kt).astype(BF16), head_rows[hh],
                              preferred_element_type=F32)
                return jnp.maximum(kmax2, ksq)

            kmax2 = lax.fori_loop(0, cf_ref.shape[0] // t, fill, jnp.zeros((t, LANES), F32))
            kmax_ref[hh] = jnp.max(kmax2, axis=0, keepdims=True)

    scores_h, consume_h, result = _fox_parts(q_ref, k_ref, vt_ref, cfb_ref, st_ref, m_ref,
                                             acc_ref, t)

    q32 = q_ref[...].astype(F32)
    ones_lhs = jnp.ones((BF16_ROWS, LANES), BF16)
    qk_bound = []
    for hh, mk in zip(heads, _head_masks(t)):
        qsq = jnp.where(mk, q32 * q32, 0.0).astype(BF16)
        qn2 = _dot_nt(ones_lhs, qsq)[0:1, :]
        qk_bound.append(NORM_SLACK * jnp.sqrt(qn2 * kmax_ref[hh][:, 0:1]))

    def more_needed(j):
        row = (j + 1) * t - 1
        need = jnp.int32(0)
        for hh in heads:
            f_hi = cfb_ref[hh, pl.ds(row, 1), :][:, 0:1]
            gap = jnp.max(qk_bound[hh] - f_hi - m_ref[hh])
            need = need | jnp.logical_not(gap <= -FOX_EXIT).astype(jnp.int32)
        return need

    def scores(j, slot):
        for hh in heads:
            scores_h(j, slot, hh)

    def step(j, slot, masked, nxt):
        for hh in heads:
            if nxt is not None:
                scores_h(*nxt, hh)
            consume_h(j, slot, masked, hh)

    _sweep_key_tiles(i, scores, step, more_needed, test_every_tile=False)
    _write_heads(o_ref, result())


def _sb_kernel(q_ref, k_ref, vt_ref, tri_ref, o_ref, z_ref, r_ref, acc_ref, *, t, n_heads):
    i = pl.program_id(2)
    scores_h, gate_sums_h, accumulate_h, result = _sb_parts(
        q_ref, k_ref, vt_ref, tri_ref, z_ref, r_ref, acc_ref, t, n_heads)
    heads = range(n_heads)

    def scores(j, slot):
        for hh in heads:
            scores_h(j, slot, hh)

    def step(j, slot, masked, nxt):
        staged = []
        for hh in heads:
            if nxt is not None:
                scores_h(*nxt, hh)
            staged.append(gate_sums_h(slot, masked, hh))
        for hh in heads:
            accumulate_h(j, staged[hh], hh)

    def more_needed(_):
        return (jnp.min(r_ref[...]) < SB_EXIT).astype(jnp.int32)

    _sweep_key_tiles(i, scores, step, more_needed, test_every_tile=True)
    _write_heads(o_ref, result())


def _attention(q, k, vt, cf):
    b, s, _ = q.shape
    rr = lax.broadcasted_iota(jnp.int32, (MXU_DIM + BF16_ROWS, MXU_DIM), 0)
    cc = lax.broadcasted_iota(jnp.int32, (MXU_DIM + BF16_ROWS, MXU_DIM), 1)
    tri = ((cc > rr) | (rr >= MXU_DIM)).astype(BF16)

    def specs(t, w, first):
        return [pl.BlockSpec((None, t, w), lambda bi, p, i: (bi, i, p + first)),
                pl.BlockSpec((None, s, w), lambda bi, p, i: (bi, 0, p + first)),
                pl.BlockSpec((None, w, s), lambda bi, p, i: (bi, p + first, 0))]

    def out_spec(t, w):
        return pl.BlockSpec((None, t, w), lambda bi, p, i: (bi, i, p))

    params = _params(("arbitrary", "arbitrary", "arbitrary"))
    t = T_FOX
    o_fox = pl.pallas_call(
        functools.partial(_fox_kernel, t=t),
        grid=(b, H_FOX // HEADS_PER_BLOCK, s // t),
        in_specs=specs(t, LANES, 0) + [
            pl.BlockSpec((None, s, LANES), lambda bi, p, i: (bi, 0, 0),
                         pipeline_mode=pl.Buffered(1))],
        out_specs=out_spec(t, LANES),
        out_shape=jax.ShapeDtypeStruct((b, s, H_FOX * HEAD_DIM), BF16),
        scratch_shapes=[pltpu.VMEM((HEADS_PER_BLOCK, s, LANES), F32),
                        pltpu.VMEM((HEADS_PER_BLOCK, 1, LANES), F32),
                        pltpu.VMEM((2, HEADS_PER_BLOCK, t, t), F32),
                        pltpu.VMEM((HEADS_PER_BLOCK, 1, t), F32),
                        pltpu.VMEM((HEADS_PER_BLOCK, HEAD_DIM + BF16_ROWS, t), F32)],
        compiler_params=params,
        name="fox_attention",
    )(q, k, vt, cf)
    t = T_SB
    nh = SB_HEADS_PER_STEP
    w = nh * HEAD_DIM
    o_sb = pl.pallas_call(
        functools.partial(_sb_kernel, t=t, n_heads=nh),
        grid=(b, H_SB // nh, s // t),
        in_specs=specs(t, w, H_FOX // nh) + [_resident(tri.shape)],
        out_specs=out_spec(t, w),
        out_shape=jax.ShapeDtypeStruct((b, s, H_SB * HEAD_DIM), BF16),
        scratch_shapes=[pltpu.VMEM((2, nh, t, t), F32),
                        pltpu.VMEM((nh, 1, t), F32),
                        pltpu.VMEM((nh, HEAD_DIM, t), F32)],
        compiler_params=params,
        name="sb_attention",
    )(q, k, vt, tri)
    return o_fox, o_sb


def _conv_mixer_kernel(h_ref, g_ref, win_ref, cw_ref, o_ref, ext_ref, tail_ref, *, tm):
    i = pl.program_id(1)

    @pl.when(i == 0)
    def _():
        tail_ref[...] = jnp.zeros_like(tail_ref)

    x = h_ref[...]
    xn = _rms_norm(x, g_ref[...]).astype(BF16)
    d = x.shape[-1]
    gate_c = jnp.dot(xn, win_ref[:, d:2 * d], preferred_element_type=F32)
    u = jnp.dot(xn, win_ref[:, 2 * d:3 * d], preferred_element_type=F32)
    cu = gate_c * u
    y = _causal_conv3(ext_ref, tail_ref[...], cu, cw_ref[...], tm)
    tail_ref[...] = cu[tm - CONV_HALO:tm, :]
    gate_b = jnp.dot(xn, win_ref[:, 0:d], preferred_element_type=F32)
    o_ref[...] = (gate_b * y).astype(o_ref.dtype)


def _conv_mixer(h, g, w_in, conv_w):
    b, s, d = h.shape
    tm = TM_PROJ
    return pl.pallas_call(
        functools.partial(_conv_mixer_kernel, tm=tm),
        grid=(b, s // tm),
        in_specs=[
            pl.BlockSpec((None, tm, d), lambda bi, i: (bi, i, 0)),
            _resident(g.shape), _resident(w_in.shape), _resident(conv_w.shape),
        ],
        out_specs=pl.BlockSpec((None, tm, d), lambda bi, i: (bi, i, 0)),
        out_shape=jax.ShapeDtypeStruct(h.shape, BF16),
        scratch_shapes=[pltpu.VMEM((tm + CONV_HALO, d), F32),
                        pltpu.VMEM((CONV_HALO, d), F32)],
        compiler_params=_params(("arbitrary", "arbitrary")),
        name="conv_mixer",
    )(h, g, w_in, conv_w)


def _ffn_kernel(h_ref, a0_ref, a1_ref, w0_ref, w1_ref, g_ref, wup_ref, cw_ref, wd_ref, o_ref,
                xn_ref, ext_ref, tail_ref, act_ref, *, tm, fc):
    i = pl.program_id(1)
    n_chunks = D_FF // fc

    @pl.when(i == 0)
    def _():
        tail_ref[...] = jnp.zeros_like(tail_ref)

    x = (h_ref[...]
         + jnp.dot(a0_ref[...], w0_ref[...], preferred_element_type=F32)
         + jnp.dot(a1_ref[...], w1_ref[...], preferred_element_type=F32))
    xn_ref[...] = _rms_norm(x, g_ref[...]).astype(BF16)
    o_ref[...] = x

    def up_project(c):
        slot = c % 2
        for part in range(2):
            cols = slice(part * D_FF + c * fc, part * D_FF + (c + 1) * fc)
            ext_ref[slot, part, 0:CONV_HALO, :] = tail_ref[part, :, c * fc:(c + 1) * fc]
            ext_ref[slot, part, CONV_HALO:CONV_HALO + tm, :] = jnp.dot(
                xn_ref[...], wup_ref[:, cols], preferred_element_type=F32)

    def conv_gate(c):
        slot = c % 2
        y = []
        for part in range(2):
            cols = slice(part * D_FF + c * fc, part * D_FF + (c + 1) * fc)
            w = cw_ref[:, cols]
            taps = [w[k:k + 1, :] * ext_ref[slot, part, CONV_HALO - 2 + k:CONV_HALO - 2 + k + tm, :]
                    for k in range(3)]
            y.append(taps[2] + taps[1] + taps[0])
            tail_ref[part, :, c * fc:(c + 1) * fc] = ext_ref[slot, part, tm:tm + CONV_HALO, :]
        half_g = 0.5 * y[0]
        act_ref[:, c * fc:(c + 1) * fc] = (
            (half_g + half_g * jnp.tanh(half_g)) * y[1]).astype(BF16)

    def down_project(rows):
        o_ref[...] += jnp.dot(act_ref[:, rows], wd_ref[rows, :], preferred_element_type=F32)

    head = n_chunks - DOWN_TAIL
    up_project(0)
    for c in range(n_chunks):
        if c + 1 < n_chunks:
            up_project(c + 1)
        if c == head:
            down_project(slice(0, head * fc))
        conv_gate(c)
    down_project(slice(head * fc, D_FF))


def _mixer_out_ffn(h, mix, w_mix, g, w_up, conv_w, w_down):
    b, s, d = h.shape
    tm = TM_FFN
    fc = FF_CHUNK
    half = d // 2
    (a0, blk0), (a1, blk1) = mix
    w0, w1 = w_mix[:half], w_mix[half:]

    def half_spec(blk):
        return pl.BlockSpec((None, tm, half), lambda bi, i: (bi, i, blk))

    return pl.pallas_call(
        functools.partial(_ffn_kernel, tm=tm, fc=fc),
        grid=(b, s // tm),
        in_specs=[
            pl.BlockSpec((None, tm, d), lambda bi, i: (bi, i, 0)),
            half_spec(blk0), half_spec(blk1), _resident(w0.shape), _resident(w1.shape),
            _resident(g.shape), _resident(w_up.shape), _resident(conv_w.shape),
            _resident(w_down.shape),
        ],
        out_specs=pl.BlockSpec((None, tm, d), lambda bi, i: (bi, i, 0)),
        out_shape=jax.ShapeDtypeStruct(h.shape, F32),
        scratch_shapes=[pltpu.VMEM((tm, d), BF16),
                        pltpu.VMEM((2, 2, tm + CONV_HALO, fc), F32),
                        pltpu.VMEM((2, CONV_HALO, D_FF), F32),
                        pltpu.VMEM((tm, D_FF), BF16)],
        compiler_params=_params(("arbitrary", "arbitrary")),
        name="conv_ffn",
    )(h, a0, a1, w0, w1, g, w_up, conv_w, w_down)


def kernel(x, attn_norm, attn_w_in, attn_f_bias, fox_q_gain, fox_k_gain, sb_q_gain, sb_k_gain,
           attn_w_out, conv_norm, conv_w_in, conv_kernel, conv_w_out, ffn_norm, ffn_w_up,
           ffn_conv, ffn_w_down):
    depth = ffn_norm.shape[0]
    h = x
    for layer in range(depth):
        i = layer // 2
        if layer % 2 == 0:
            w_in = attn_w_in[i]
            w_qk = w_in[:, :2 * MIX_WIDTH].astype(BF16)
            wvt = w_in[:, 2 * MIX_WIDTH:3 * MIX_WIDTH].T.astype(BF16)
            wf = jnp.zeros((D_MODEL, LANES), BF16).at[:, :H_FOX].set(
                w_in[:, 3 * MIX_WIDTH:].astype(BF16))
            fb = jnp.zeros((1, LANES), F32).at[0, :H_FOX].set(attn_f_bias[i])
            qk_gain = jnp.concatenate(
                [jnp.tile(fox_q_gain[i], H_FOX), jnp.tile(sb_q_gain[i], H_SB),
                 jnp.tile(fox_k_gain[i], H_FOX), jnp.tile(sb_k_gain[i], H_SB)])[None, :]
            q, k, vt, cf = _attn_inproj(h, attn_norm[i][None, :], w_qk, wvt, wf, fb, qk_gain)
            o_fox, o_sb = _attention(q, k, vt, cf)
            mix = ((o_fox, 0), (o_sb, 0))
            w_mix = attn_w_out[i]
        else:
            y = _conv_mixer(h, conv_norm[i][None, :], conv_w_in[i].astype(BF16), conv_kernel[i])
            mix = ((y, 0), (y, 1))
            w_mix = conv_w_out[i]
        h = _mixer_out_ffn(h, mix, w_mix.astype(BF16), ffn_norm[layer][None, :],
                           ffn_w_up[layer].astype(BF16), ffn_conv[layer],
                           ffn_w_down[layer].astype(BF16))
    return h
```

```python
import functools

import jax
import jax.numpy as jnp
from jax import lax
from jax.experimental import pallas as pl
from jax.experimental.pallas import tpu as pltpu

F32 = jnp.float32
BF16 = jnp.bfloat16

D_MODEL = 1024
HEAD_DIM = 64
H_FOX = 8
H_SB = 8
MIX_WIDTH = (H_FOX + H_SB) * HEAD_DIM
D_FF = 2816
EPS = 1e-6
QK_SCALE = HEAD_DIM ** -0.5
LOG2E = 1.4426950408889634

LANES = 128
SUBLANES = 8
HEADS_PER_BLOCK = LANES // HEAD_DIM
BF16_ROWS = 2 * SUBLANES
MXU_DIM = 256
EXP2_CLAMP = 126.0
VMEM_LIMIT = 56 * 1024 * 1024

TM_PROJ = 256
TM_FFN = 512
FF_CHUNK = 256
DOWN_TAIL = 2
T_FOX = 512
T_SB = 256
SB_HEADS_PER_STEP = 8
SB_EXIT = 160.0
FOX_EXIT = 160.0
NORM_SLACK = 1.01
CONV_HALO = SUBLANES


def _params(sem):
    return pltpu.CompilerParams(dimension_semantics=sem, vmem_limit_bytes=VMEM_LIMIT)


def _resident(shape):
    zeros = (0,) * len(shape)
    return pl.BlockSpec(shape, lambda *_: zeros, pipeline_mode=pl.Buffered(1))


def _rms_norm(x, g):
    ms = jnp.mean(x * x, axis=-1, keepdims=True)
    return x * lax.rsqrt(ms + EPS) * g


def _split3(x):
    hi = x.astype(BF16)
    r = x - hi.astype(F32)
    mid = r.astype(BF16)
    lo = (r - mid.astype(F32)).astype(BF16)
    return hi, mid, lo


def _dot_nt(a, b):
    return lax.dot_general(a, b, (((1,), (1,)), ((), ())), preferred_element_type=F32)


def _causal_conv3(ext_ref, tail, cur, w, tm):
    ext_ref[0:CONV_HALO, :] = tail
    ext_ref[CONV_HALO:CONV_HALO + tm, :] = cur
    return (w[2:3, :] * cur
            + w[1:2, :] * ext_ref[CONV_HALO - 1:CONV_HALO - 1 + tm, :]
            + w[0:1, :] * ext_ref[CONV_HALO - 2:CONV_HALO - 2 + tm, :])


def _attn_inproj_kernel(h_ref, g_ref, w_ref, wvt_ref, wf_ref, fb_ref, qkg_ref, gmat_ref,
                        tri_ref, q_ref, k_ref, vt_ref, cf_ref, carry_ref, *, tm):
    i = pl.program_id(1)

    @pl.when(i == 0)
    def _():
        carry_ref[...] = jnp.zeros_like(carry_ref)

    xn = _rms_norm(h_ref[...], g_ref[...]).astype(BF16)
    gmat = gmat_ref[...]

    n_blk = MIX_WIDTH // MXU_DIM
    proj = [jnp.dot(xn, w_ref[:, c * MXU_DIM:(c + 1) * MXU_DIM], preferred_element_type=F32)
            for c in range(2 * n_blk)]
    vt_ref[...] = _dot_nt(wvt_ref[...], xn).astype(BF16)
    fl = jnp.dot(xn, wf_ref[...], preferred_element_type=F32) + fb_ref[...]

    for c in range(2 * n_blk):
        out_ref, scale = (q_ref, QK_SCALE * LOG2E) if c < n_blk else (k_ref, 1.0)
        t = proj[c]
        ssq = jnp.dot((t * t).astype(BF16), gmat, preferred_element_type=F32)
        gain = qkg_ref[:, c * MXU_DIM:(c + 1) * MXU_DIM]
        tn = t * lax.rsqrt(ssq * (1.0 / HEAD_DIM) + EPS) * gain
        if scale != 1.0:
            tn = tn * scale
        lo = (c % n_blk) * MXU_DIM
        out_ref[:, lo:lo + MXU_DIM] = tn.astype(BF16)

    log_f = jnp.minimum(fl, 0.0) - jnp.log1p(jnp.exp(-jnp.abs(fl)))
    tri = tri_ref[...]
    hi, mid, lo = _split3(log_f)
    cs = (jnp.dot(tri, hi, preferred_element_type=F32)
          + jnp.dot(tri, mid, preferred_element_type=F32)
          + jnp.dot(tri, lo, preferred_element_type=F32))
    cf = cs + carry_ref[0:1, :]
    carry_ref[...] = jnp.broadcast_to(cf[tm - 1:tm, :], carry_ref.shape)
    cf_ref[...] = cf


def _attn_inproj(h, g, w_qk, wvt, wf, fb, qk_gain):
    b, s, d = h.shape
    tm = TM_PROJ
    r = lax.broadcasted_iota(jnp.int32, (MXU_DIM, MXU_DIM), 0) // HEAD_DIM
    c = lax.broadcasted_iota(jnp.int32, (MXU_DIM, MXU_DIM), 1) // HEAD_DIM
    gmat = (r == c).astype(BF16)
    rr = lax.broadcasted_iota(jnp.int32, (tm, tm), 0)
    cc = lax.broadcasted_iota(jnp.int32, (tm, tm), 1)
    tri = (cc <= rr).astype(BF16)
    row_spec = pl.BlockSpec((None, tm, MIX_WIDTH), lambda bi, i: (bi, i, 0))
    return pl.pallas_call(
        functools.partial(_attn_inproj_kernel, tm=tm),
        grid=(b, s // tm),
        in_specs=[
            pl.BlockSpec((None, tm, d), lambda bi, i: (bi, i, 0)),
            _resident(g.shape), _resident(w_qk.shape), _resident(wvt.shape),
            _resident(wf.shape), _resident(fb.shape), _resident(qk_gain.shape),
            _resident(gmat.shape), _resident(tri.shape),
        ],
        out_specs=[row_spec, row_spec,
                   pl.BlockSpec((None, MIX_WIDTH, tm), lambda bi, i: (bi, 0, i)),
                   pl.BlockSpec((None, tm, LANES), lambda bi, i: (bi, i, 0))],
        out_shape=[jax.ShapeDtypeStruct((b, s, MIX_WIDTH), BF16),
                   jax.ShapeDtypeStruct((b, s, MIX_WIDTH), BF16),
                   jax.ShapeDtypeStruct((b, MIX_WIDTH, s), BF16),
                   jax.ShapeDtypeStruct((b, s, LANES), F32)],
        scratch_shapes=[pltpu.VMEM((SUBLANES, LANES), F32)],
        compiler_params=_params(("arbitrary", "arbitrary")),
        name="attn_inproj",
    )(h, g, w_qk, wvt, wf, fb, qk_gain, gmat, tri)


def _head_masks(t, n_heads=HEADS_PER_BLOCK):
    lane = lax.broadcasted_iota(jnp.int32, (t, n_heads * HEAD_DIM), 1)
    return [(lane >= HEAD_DIM * hh) & (lane < HEAD_DIM * (hh + 1)) for hh in range(n_heads)]


def _lane_pieces(t):
    return [slice(c * LANES, (c + 1) * LANES) for c in range(t // LANES)]


def _sweep_key_tiles(i, scores, step, more_needed, test_every_tile, start_paired):
    def clamp(j):
        return jnp.maximum(j, 0)

    def walk(first, slot_a):
        slot_b = 1 - slot_a
        count = first + 1

        def unfinished(carry):
            r, go = carry
            return jnp.logical_and(r < count // 2, go > 0)

        def pair(carry):
            r, _ = carry
            j = first - 2 * r
            step(j, slot_a, False, (clamp(j - 1), slot_b))
            if not test_every_tile:
                step(j - 1, slot_b, False, (clamp(j - 2), slot_a))
                return r + 1, more_needed(clamp(j - 2))
            go_mid = more_needed(j - 1)

            @pl.when(go_mid > 0)
            def _():
                step(j - 1, slot_b, False, (clamp(j - 2), slot_a))

            return r + 1, go_mid & more_needed(clamp(j - 2))

        _, go = lax.while_loop(unfinished, pair, (jnp.int32(0), more_needed(clamp(first))))

        @pl.when(jnp.logical_and(count % 2 == 1, go > 0))
        def _():
            step(0, slot_a, False, None)

    if not start_paired:
        scores(i, 0)
        step(i, 0, True, (clamp(i - 1), 1))
        walk(i - 1, 1)
        return

    @pl.when(i == 0)
    def _():
        scores(0, 0)
        step(0, 0, True, None)

    @pl.when(i > 0)
    def _():
        scores(i, 0)
        step(i, 0, True, (i - 1, 1))
        step(i - 1, 1, False, (clamp(i - 2), 0))
        walk(i - 2, 0)


def _write_heads(o_ref, outs_t):
    o_ref[...] = jnp.concatenate(outs_t, axis=0).T.astype(o_ref.dtype)


def _fox_parts(q_ref, k_ref, vt_ref, cfb_ref, st_ref, m_ref, acc_ref, t):
    q = q_ref[...]
    qh = [jnp.where(mk, q, jnp.zeros_like(q)) for mk in _head_masks(t)]
    m_ref[...] = jnp.full_like(m_ref, -jnp.inf)
    acc_ref[...] = jnp.zeros_like(acc_ref)
    key = lax.broadcasted_iota(jnp.int32, (t, LANES), 0)
    qry = lax.broadcasted_iota(jnp.int32, (t, LANES), 1)
    pieces = _lane_pieces(t)
    ones_rows = jnp.ones((BF16_ROWS, t), BF16)

    def scores(j, slot, hh):
        off = pl.multiple_of(j * t, t)
        st = _dot_nt(k_ref[pl.ds(off, t), :], qh[hh])
        fcol = cfb_ref[hh, pl.ds(off, t), :]
        for cols in pieces:
            st_ref[slot, hh, :, cols] = st[:, cols] - fcol

    def consume(j, slot, masked, hh):
        off = pl.multiple_of(j * t, t)
        m_old = m_ref[hh]
        m_new, alpha, prob = [], [], []
        for c, cols in enumerate(pieces):
            x = st_ref[slot, hh, :, cols]
            if masked:
                x = jnp.where(key <= qry + c * LANES, x, -jnp.inf)
            m_o = m_old[:, cols]
            m_n = jnp.maximum(m_o, jnp.max(x, axis=0, keepdims=True))
            m_new.append(m_n)
            alpha.append(jnp.exp2(m_o - m_n))
            prob.append(jnp.exp2(x - m_n).astype(BF16))
        vt_h = jnp.concatenate(
            [vt_ref[hh * HEAD_DIM:(hh + 1) * HEAD_DIM, pl.ds(off, t)], ones_rows], axis=0)
        pv = jnp.dot(vt_h, jnp.concatenate(prob, axis=1), preferred_element_type=F32)
        acc_ref[hh] = jnp.concatenate(alpha, axis=1) * acc_ref[hh] + pv
        m_ref[hh] = jnp.concatenate(m_new, axis=1)

    def result():
        return [acc_ref[hh, 0:HEAD_DIM, :] / acc_ref[hh, HEAD_DIM:HEAD_DIM + 1, :]
                for hh in range(HEADS_PER_BLOCK)]

    return scores, consume, result


def _sb_parts(q_ref, k_ref, vt_ref, tri_ref, z_ref, r_ref, acc_ref, t, n_heads):
    q = q_ref[...]
    qh = [jnp.where(mk, q, jnp.zeros_like(q)) for mk in _head_masks(t, n_heads)]
    r_ref[...] = jnp.zeros_like(r_ref)
    acc_ref[...] = jnp.zeros_like(acc_ref)
    key = lax.broadcasted_iota(jnp.int32, (t, t), 0)
    qry = lax.broadcasted_iota(jnp.int32, (t, t), 1)
    strict = key < qry
    n_sub = t // MXU_DIM

    sub_rows = [slice(c * MXU_DIM, (c + 1) * MXU_DIM) for c in range(n_sub)]

    def scores(j, slot, hh):
        off = pl.multiple_of(j * t, t)
        z_ref[slot, hh] = _dot_nt(k_ref[pl.ds(off, t), :], qh[hh])

    def gate_sums(slot, masked, hh):
        z = z_ref[slot, hh]
        a = jnp.maximum(LOG2E * jnp.log(1.0 + jnp.exp2(jnp.minimum(z, EXP2_CLAMP))), z)
        base = z - a
        if masked:
            a = jnp.where(strict, a, 0.0)
            base = jnp.where(strict, base, -jnp.inf)
        tri = tri_ref[...]
        sums = [jnp.dot(tri, a[rows, :].astype(BF16), preferred_element_type=F32)
                for rows in sub_rows]
        return base, sums

    def accumulate(j, staged, hh):
        off = pl.multiple_of(j * t, t)
        base, sums = staged
        r_run = r_ref[hh]
        w = [None] * n_sub
        for c in reversed(range(n_sub)):
            later = sums[c][0:MXU_DIM, :]
            w[c] = jnp.exp2(base[sub_rows[c], :] - later - r_run).astype(BF16)
            r_run = r_run + sums[c][MXU_DIM:MXU_DIM + 1, :]
        vt_h = vt_ref[hh * HEAD_DIM:(hh + 1) * HEAD_DIM, pl.ds(off, t)]
        acc_ref[hh] += jnp.dot(vt_h, jnp.concatenate(w, axis=0), preferred_element_type=F32)
        r_ref[hh] = r_run

    def result():
        return [acc_ref[hh] for hh in range(n_heads)]

    return scores, gate_sums, accumulate, result


def _fox_kernel(q_ref, k_ref, vt_ref, cf_ref, o_ref, cfb_ref, kmax_ref, st_ref, m_ref, acc_ref,
                *, t):
    p = pl.program_id(1)
    i = pl.program_id(2)
    heads = range(HEADS_PER_BLOCK)
    lane_row = lax.broadcasted_iota(jnp.int32, (LANES, LANES), 0)
    head_rows = [((lane_row >= HEAD_DIM * hh) & (lane_row < HEAD_DIM * (hh + 1))).astype(BF16)
                 for hh in heads]

    @pl.when(i == 0)
    def _():
        for hh in heads:
            sel = (lane_row == p * HEADS_PER_BLOCK + hh).astype(BF16)

            def fill(r, kmax2):
                off = pl.multiple_of(r * t, t)
                hi, mid, lo = _split3(cf_ref[pl.ds(off, t), :])
                cfb_ref[hh, pl.ds(off, t), :] = LOG2E * (
                    jnp.dot(hi, sel, preferred_element_type=F32)
                    + jnp.dot(mid, sel, preferred_element_type=F32)
                    + jnp.dot(lo, sel, preferred_element_type=F32))
                kt = k_ref[pl.ds(off, t), :].astype(F32)
                ksq = jnp.dot((kt * kt).astype(BF16), head_rows[hh],
                              preferred_element_type=F32)
                return jnp.maximum(kmax2, ksq)

            kmax2 = lax.fori_loop(0, cf_ref.shape[0] // t, fill, jnp.zeros((t, LANES), F32))
            kmax_ref[hh] = jnp.max(kmax2, axis=0, keepdims=True)

    scores_h, consume_h, result = _fox_parts(q_ref, k_ref, vt_ref, cfb_ref, st_ref, m_ref,
                                             acc_ref, t)

    q32 = q_ref[...].astype(F32)
    ones_lhs = jnp.ones((BF16_ROWS, LANES), BF16)
    qk_bound = []
    for hh, mk in zip(heads, _head_masks(t)):
        qsq = jnp.where(mk, q32 * q32, 0.0).astype(BF16)
        qn2 = _dot_nt(ones_lhs, qsq)[0:1, :]
        qk_bound.append(NORM_SLACK * jnp.sqrt(qn2 * kmax_ref[hh][:, 0:1]))

    def more_needed(j):
        row = (j + 1) * t - 1
        need = jnp.int32(0)
        for hh in heads:
            f_hi = cfb_ref[hh, pl.ds(row, 1), :][:, 0:1]
            gap = jnp.max(qk_bound[hh] - f_hi - m_ref[hh])
            need = need | jnp.logical_not(gap <= -FOX_EXIT).astype(jnp.int32)
        return need

    def scores(j, slot):
        for hh in heads:
            scores_h(j, slot, hh)

    def step(j, slot, masked, nxt):
        for hh in heads:
            if nxt is not None:
                scores_h(*nxt, hh)
            consume_h(j, slot, masked, hh)

    _sweep_key_tiles(i, scores, step, more_needed, test_every_tile=False, start_paired=False)
    _write_heads(o_ref, result())


def _sb_kernel(q_ref, k_ref, vt_ref, tri_ref, o_ref, z_ref, r_ref, acc_ref, *, t, n_heads):
    i = pl.program_id(2)
    scores_h, gate_sums_h, accumulate_h, result = _sb_parts(
        q_ref, k_ref, vt_ref, tri_ref, z_ref, r_ref, acc_ref, t, n_heads)
    heads = range(n_heads)

    def scores(j, slot):
        for hh in heads:
            scores_h(j, slot, hh)

    def step(j, slot, masked, nxt):
        staged = []
        for hh in heads:
            if nxt is not None:
                scores_h(*nxt, hh)
            staged.append(gate_sums_h(slot, masked, hh))
        for hh in heads:
            accumulate_h(j, staged[hh], hh)

    def more_needed(_):
        return (jnp.min(r_ref[...]) < SB_EXIT).astype(jnp.int32)

    _sweep_key_tiles(i, scores, step, more_needed, test_every_tile=True, start_paired=True)
    _write_heads(o_ref, result())


def _attention(q, k, vt, cf):
    b, s, _ = q.shape
    rr = lax.broadcasted_iota(jnp.int32, (MXU_DIM + BF16_ROWS, MXU_DIM), 0)
    cc = lax.broadcasted_iota(jnp.int32, (MXU_DIM + BF16_ROWS, MXU_DIM), 1)
    tri = ((cc > rr) | (rr >= MXU_DIM)).astype(BF16)

    def specs(t, w, first):
        return [pl.BlockSpec((None, t, w), lambda bi, p, i: (bi, i, p + first)),
                pl.BlockSpec((None, s, w), lambda bi, p, i: (bi, 0, p + first)),
                pl.BlockSpec((None, w, s), lambda bi, p, i: (bi, p + first, 0))]

    def out_spec(t, w):
        return pl.BlockSpec((None, t, w), lambda bi, p, i: (bi, i, p))

    params = _params(("arbitrary", "arbitrary", "arbitrary"))
    t = T_FOX
    o_fox = pl.pallas_call(
        functools.partial(_fox_kernel, t=t),
        grid=(b, H_FOX // HEADS_PER_BLOCK, s // t),
        in_specs=specs(t, LANES, 0) + [
            pl.BlockSpec((None, s, LANES), lambda bi, p, i: (bi, 0, 0),
                         pipeline_mode=pl.Buffered(1))],
        out_specs=out_spec(t, LANES),
        out_shape=jax.ShapeDtypeStruct((b, s, H_FOX * HEAD_DIM), BF16),
        scratch_shapes=[pltpu.VMEM((HEADS_PER_BLOCK, s, LANES), F32),
                        pltpu.VMEM((HEADS_PER_BLOCK, 1, LANES), F32),
                        pltpu.VMEM((2, HEADS_PER_BLOCK, t, t), F32),
                        pltpu.VMEM((HEADS_PER_BLOCK, 1, t), F32),
                        pltpu.VMEM((HEADS_PER_BLOCK, HEAD_DIM + BF16_ROWS, t), F32)],
        compiler_params=params,
        name="fox_attention",
    )(q, k, vt, cf)
    t = T_SB
    nh = SB_HEADS_PER_STEP
    w = nh * HEAD_DIM
    o_sb = pl.pallas_call(
        functools.partial(_sb_kernel, t=t, n_heads=nh),
        grid=(b, H_SB // nh, s // t),
        in_specs=specs(t, w, H_FOX // nh) + [_resident(tri.shape)],
        out_specs=out_spec(t, w),
        out_shape=jax.ShapeDtypeStruct((b, s, H_SB * HEAD_DIM), BF16),
        scratch_shapes=[pltpu.VMEM((2, nh, t, t), F32),
                        pltpu.VMEM((nh, 1, t), F32),
                        pltpu.VMEM((nh, HEAD_DIM, t), F32)],
        compiler_params=params,
        name="sb_attention",
    )(q, k, vt, tri)
    return o_fox, o_sb


def _conv_mixer_kernel(h_ref, g_ref, win_ref, cw_ref, o_ref, ext_ref, tail_ref, *, tm):
    i = pl.program_id(1)

    @pl.when(i == 0)
    def _():
        tail_ref[...] = jnp.zeros_like(tail_ref)

    x = h_ref[...]
    xn = _rms_norm(x, g_ref[...]).astype(BF16)
    d = x.shape[-1]
    gate_c = jnp.dot(xn, win_ref[:, d:2 * d], preferred_element_type=F32)
    u = jnp.dot(xn, win_ref[:, 2 * d:3 * d], preferred_element_type=F32)
    cu = gate_c * u
    y = _causal_conv3(ext_ref, tail_ref[...], cu, cw_ref[...], tm)
    tail_ref[...] = cu[tm - CONV_HALO:tm, :]
    gate_b = jnp.dot(xn, win_ref[:, 0:d], preferred_element_type=F32)
    o_ref[...] = (gate_b * y).astype(o_ref.dtype)


def _conv_mixer(h, g, w_in, conv_w):
    b, s, d = h.shape
    tm = TM_PROJ
    return pl.pallas_call(
        functools.partial(_conv_mixer_kernel, tm=tm),
        grid=(b, s // tm),
        in_specs=[
            pl.BlockSpec((None, tm, d), lambda bi, i: (bi, i, 0)),
            _resident(g.shape), _resident(w_in.shape), _resident(conv_w.shape),
        ],
        out_specs=pl.BlockSpec((None, tm, d), lambda bi, i: (bi, i, 0)),
        out_shape=jax.ShapeDtypeStruct(h.shape, BF16),
        scratch_shapes=[pltpu.VMEM((tm + CONV_HALO, d), F32),
                        pltpu.VMEM((CONV_HALO, d), F32)],
        compiler_params=_params(("arbitrary", "arbitrary")),
        name="conv_mixer",
    )(h, g, w_in, conv_w)


def _ffn_kernel(h_ref, a0_ref, a1_ref, w0_ref, w1_ref, g_ref, wup_ref, cw_ref, wd_ref, o_ref,
                xn_ref, ext_ref, tail_ref, act_ref, *, tm, fc):
    i = pl.program_id(1)
    n_chunks = D_FF // fc

    @pl.when(i == 0)
    def _():
        tail_ref[...] = jnp.zeros_like(tail_ref)

    x = (h_ref[...]
         + jnp.dot(a0_ref[...], w0_ref[...], preferred_element_type=F32)
         + jnp.dot(a1_ref[...], w1_ref[...], preferred_element_type=F32))
    xn_ref[...] = _rms_norm(x, g_ref[...]).astype(BF16)
    o_ref[...] = x

    def up_project(c):
        slot = c % 2
        for part in range(2):
            cols = slice(part * D_FF + c * fc, part * D_FF + (c + 1) * fc)
            ext_ref[slot, part, 0:CONV_HALO, :] = tail_ref[part, :, c * fc:(c + 1) * fc]
            ext_ref[slot, part, CONV_HALO:CONV_HALO + tm, :] = jnp.dot(
                xn_ref[...], wup_ref[:, cols], preferred_element_type=F32)

    def conv_gate(c):
        slot = c % 2
        y = []
        for part in range(2):
            cols = slice(part * D_FF + c * fc, part * D_FF + (c + 1) * fc)
            w = cw_ref[:, cols]
            taps = [w[k:k + 1, :] * ext_ref[slot, part, CONV_HALO - 2 + k:CONV_HALO - 2 + k + tm, :]
                    for k in range(3)]
            y.append(taps[2] + taps[1] + taps[0])
            tail_ref[part, :, c * fc:(c + 1) * fc] = ext_ref[slot, part, tm:tm + CONV_HALO, :]
        half_g = 0.5 * y[0]
        act_ref[:, c * fc:(c + 1) * fc] = (
            (half_g + half_g * jnp.tanh(half_g)) * y[1]).astype(BF16)

    def down_project(rows):
        o_ref[...] += jnp.dot(act_ref[:, rows], wd_ref[rows, :], preferred_element_type=F32)

    head = n_chunks - DOWN_TAIL
    up_project(0)
    for c in range(n_chunks):
        if c + 1 < n_chunks:
            up_project(c + 1)
        if c == head:
            down_project(slice(0, head * fc))
        conv_gate(c)
    down_project(slice(head * fc, D_FF))


def _mixer_out_ffn(h, mix, w_mix, g, w_up, conv_w, w_down):
    b, s, d = h.shape
    tm = TM_FFN
    fc = FF_CHUNK
    half = d // 2
    (a0, blk0), (a1, blk1) = mix
    w0, w1 = w_mix[:half], w_mix[half:]

    def half_spec(blk):
        return pl.BlockSpec((None, tm, half), lambda bi, i: (bi, i, blk))

    return pl.pallas_call(
        functools.partial(_ffn_kernel, tm=tm, fc=fc),
        grid=(b, s // tm),
        in_specs=[
            pl.BlockSpec((None, tm, d), lambda bi, i: (bi, i, 0)),
            half_spec(blk0), half_spec(blk1), _resident(w0.shape), _resident(w1.shape),
            _resident(g.shape), _resident(w_up.shape), _resident(conv_w.shape),
            _resident(w_down.shape),
        ],
        out_specs=pl.BlockSpec((None, tm, d), lambda bi, i: (bi, i, 0)),
        out_shape=jax.ShapeDtypeStruct(h.shape, F32),
        scratch_shapes=[pltpu.VMEM((tm, d), BF16),
                        pltpu.VMEM((2, 2, tm + CONV_HALO, fc), F32),
                        pltpu.VMEM((2, CONV_HALO, D_FF), F32),
                        pltpu.VMEM((tm, D_FF), BF16)],
        compiler_params=_params(("arbitrary", "arbitrary")),
        name="conv_ffn",
    )(h, a0, a1, w0, w1, g, w_up, conv_w, w_down)


def kernel(x, attn_norm, attn_w_in, attn_f_bias, fox_q_gain, fox_k_gain, sb_q_gain, sb_k_gain,
           attn_w_out, conv_norm, conv_w_in, conv_kernel, conv_w_out, ffn_norm, ffn_w_up,
           ffn_conv, ffn_w_down):
    depth = ffn_norm.shape[0]
    h = x
    for layer in range(depth):
        i = layer // 2
        if layer % 2 == 0:
            w_in = attn_w_in[i]
            w_qk = w_in[:, :2 * MIX_WIDTH].astype(BF16)
            wvt = w_in[:, 2 * MIX_WIDTH:3 * MIX_WIDTH].T.astype(BF16)
            wf = jnp.zeros((D_MODEL, LANES), BF16).at[:, :H_FOX].set(
                w_in[:, 3 * MIX_WIDTH:].astype(BF16))
            fb = jnp.zeros((1, LANES), F32).at[0, :H_FOX].set(attn_f_bias[i])
            qk_gain = jnp.concatenate(
                [jnp.tile(fox_q_gain[i], H_FOX), jnp.tile(sb_q_gain[i], H_SB),
                 jnp.tile(fox_k_gain[i], H_FOX), jnp.tile(sb_k_gain[i], H_SB)])[None, :]
            q, k, vt, cf = _attn_inproj(h, attn_norm[i][None, :], w_qk, wvt, wf, fb, qk_gain)
            o_fox, o_sb = _attention(q, k, vt, cf)
            mix = ((o_fox, 0), (o_sb, 0))
            w_mix = attn_w_out[i]
        else:
            y = _conv_mixer(h, conv_norm[i][None, :], conv_w_in[i].astype(BF16), conv_kernel[i])
            mix = ((y, 0), (y, 1))
            w_mix = conv_w_out[i]
        h = _mixer_out_ffn(h, mix, w_mix.astype(BF16), ffn_norm[layer][None, :],
                           ffn_w_up[layer].astype(BF16), ffn_conv[layer],
                           ffn_w_down[layer].astype(BF16))
    return h
```

```python
import functools

import jax
import jax.numpy as jnp
from jax import lax
from jax.experimental import pallas as pl
from jax.experimental.pallas import tpu as pltpu

F32 = jnp.float32
BF16 = jnp.bfloat16

D_MODEL = 1024
HEAD_DIM = 64
H_FOX = 8
H_SB = 8
MIX_WIDTH = (H_FOX + H_SB) * HEAD_DIM
D_FF = 2816
EPS = 1e-6
QK_SCALE = HEAD_DIM ** -0.5
LOG2E = 1.4426950408889634

LANES = 128
SUBLANES = 8
HEADS_PER_BLOCK = LANES // HEAD_DIM
BF16_ROWS = 2 * SUBLANES
MXU_DIM = 256
EXP2_CLAMP = 126.0
VMEM_LIMIT = 56 * 1024 * 1024

TM_PROJ = 256
TM_FFN = 512
FF_CHUNK = 256
DOWN_TAIL = 2
T_FOX = 512
T_SB = 256
SB_HEADS_PER_STEP = 8
SB_EXIT = 160.0
FOX_EXIT = 160.0
NORM_SLACK = 1.01
CONV_HALO = SUBLANES


def _params(sem):
    return pltpu.CompilerParams(dimension_semantics=sem, vmem_limit_bytes=VMEM_LIMIT)


def _resident(shape):
    zeros = (0,) * len(shape)
    return pl.BlockSpec(shape, lambda *_: zeros, pipeline_mode=pl.Buffered(1))


def _rms_norm(x, g):
    ms = jnp.mean(x * x, axis=-1, keepdims=True)
    return x * lax.rsqrt(ms + EPS) * g


def _split3(x):
    hi = x.astype(BF16)
    r = x - hi.astype(F32)
    mid = r.astype(BF16)
    lo = (r - mid.astype(F32)).astype(BF16)
    return hi, mid, lo


def _dot_nt(a, b):
    return lax.dot_general(a, b, (((1,), (1,)), ((), ())), preferred_element_type=F32)


def _causal_conv3(ext_ref, tail, cur, w, tm):
    ext_ref[0:CONV_HALO, :] = tail
    ext_ref[CONV_HALO:CONV_HALO + tm, :] = cur
    return (w[2:3, :] * cur
            + w[1:2, :] * ext_ref[CONV_HALO - 1:CONV_HALO - 1 + tm, :]
            + w[0:1, :] * ext_ref[CONV_HALO - 2:CONV_HALO - 2 + tm, :])


def _attn_inproj_kernel(h_ref, g_ref, w_ref, wvt_ref, wf_ref, fb_ref, qkg_ref, gmat_ref,
                        tri_ref, q_ref, k_ref, vt_ref, cf_ref, carry_ref, *, tm):
    i = pl.program_id(1)

    @pl.when(i == 0)
    def _():
        carry_ref[...] = jnp.zeros_like(carry_ref)

    xn = _rms_norm(h_ref[...], g_ref[...]).astype(BF16)
    gmat = gmat_ref[...]

    n_blk = MIX_WIDTH // MXU_DIM
    proj = [jnp.dot(xn, w_ref[:, c * MXU_DIM:(c + 1) * MXU_DIM], preferred_element_type=F32)
            for c in range(2 * n_blk)]
    vt_ref[...] = _dot_nt(wvt_ref[...], xn).astype(BF16)
    fl = jnp.dot(xn, wf_ref[...], preferred_element_type=F32) + fb_ref[...]

    for c in range(2 * n_blk):
        out_ref, scale = (q_ref, QK_SCALE * LOG2E) if c < n_blk else (k_ref, 1.0)
        t = proj[c]
        ssq = jnp.dot((t * t).astype(BF16), gmat, preferred_element_type=F32)
        gain = qkg_ref[:, c * MXU_DIM:(c + 1) * MXU_DIM]
        tn = t * lax.rsqrt(ssq * (1.0 / HEAD_DIM) + EPS) * gain
        if scale != 1.0:
            tn = tn * scale
        lo = (c % n_blk) * MXU_DIM
        out_ref[:, lo:lo + MXU_DIM] = tn.astype(BF16)

    log_f = jnp.minimum(fl, 0.0) - jnp.log1p(jnp.exp(-jnp.abs(fl)))
    tri = tri_ref[...]
    hi, mid, lo = _split3(log_f)
    cs = (jnp.dot(tri, hi, preferred_element_type=F32)
          + jnp.dot(tri, mid, preferred_element_type=F32)
          + jnp.dot(tri, lo, preferred_element_type=F32))
    cf = cs + carry_ref[0:1, :]
    carry_ref[...] = jnp.broadcast_to(cf[tm - 1:tm, :], carry_ref.shape)
    cf_ref[...] = cf


def _attn_inproj(h, g, w_qk, wvt, wf, fb, qk_gain):
    b, s, d = h.shape
    tm = TM_PROJ
    r = lax.broadcasted_iota(jnp.int32, (MXU_DIM, MXU_DIM), 0) // HEAD_DIM
    c = lax.broadcasted_iota(jnp.int32, (MXU_DIM, MXU_DIM), 1) // HEAD_DIM
    gmat = (r == c).astype(BF16)
    rr = lax.broadcasted_iota(jnp.int32, (tm, tm), 0)
    cc = lax.broadcasted_iota(jnp.int32, (tm, tm), 1)
    tri = (cc <= rr).astype(BF16)
    row_spec = pl.BlockSpec((None, tm, MIX_WIDTH), lambda bi, i: (bi, i, 0))
    return pl.pallas_call(
        functools.partial(_attn_inproj_kernel, tm=tm),
        grid=(b, s // tm),
        in_specs=[
            pl.BlockSpec((None, tm, d), lambda bi, i: (bi, i, 0)),
            _resident(g.shape), _resident(w_qk.shape), _resident(wvt.shape),
            _resident(wf.shape), _resident(fb.shape), _resident(qk_gain.shape),
            _resident(gmat.shape), _resident(tri.shape),
        ],
        out_specs=[row_spec, row_spec,
                   pl.BlockSpec((None, MIX_WIDTH, tm), lambda bi, i: (bi, 0, i)),
                   pl.BlockSpec((None, tm, LANES), lambda bi, i: (bi, i, 0))],
        out_shape=[jax.ShapeDtypeStruct((b, s, MIX_WIDTH), BF16),
                   jax.ShapeDtypeStruct((b, s, MIX_WIDTH), BF16),
                   jax.ShapeDtypeStruct((b, MIX_WIDTH, s), BF16),
                   jax.ShapeDtypeStruct((b, s, LANES), F32)],
        scratch_shapes=[pltpu.VMEM((SUBLANES, LANES), F32)],
        compiler_params=_params(("arbitrary", "arbitrary")),
        name="attn_inproj",
    )(h, g, w_qk, wvt, wf, fb, qk_gain, gmat, tri)


def _head_masks(t, n_heads=HEADS_PER_BLOCK):
    lane = lax.broadcasted_iota(jnp.int32, (t, n_heads * HEAD_DIM), 1)
    return [(lane >= HEAD_DIM * hh) & (lane < HEAD_DIM * (hh + 1)) for hh in range(n_heads)]


def _lane_pieces(t):
    return [slice(c * LANES, (c + 1) * LANES) for c in range(t // LANES)]


def _sweep_key_tiles(i, scores, step, more_needed, test_every_tile, start_paired):
    def clamp(j):
        return jnp.maximum(j, 0)

    def walk(first, slot_a):
        slot_b = 1 - slot_a
        count = first + 1

        def unfinished(carry):
            r, go = carry
            return jnp.logical_and(r < count // 2, go > 0)

        def pair(carry):
            r, _ = carry
            j = first - 2 * r
            step(j, slot_a, False, (clamp(j - 1), slot_b))
            if not test_every_tile:
                step(j - 1, slot_b, False, (clamp(j - 2), slot_a))
                return r + 1, more_needed(clamp(j - 2))
            go_mid = more_needed(j - 1)

            @pl.when(go_mid > 0)
            def _():
                step(j - 1, slot_b, False, (clamp(j - 2), slot_a))

            return r + 1, go_mid & more_needed(clamp(j - 2))

        _, go = lax.while_loop(unfinished, pair, (jnp.int32(0), more_needed(clamp(first))))

        @pl.when(jnp.logical_and(count % 2 == 1, go > 0))
        def _():
            step(0, slot_a, False, None)

    if not start_paired:
        scores(i, 0)
        step(i, 0, True, (clamp(i - 1), 1))
        walk(i - 1, 1)
        return

    @pl.when(i == 0)
    def _():
        scores(0, 0)
        step(0, 0, True, None)

    @pl.when(i > 0)
    def _():
        scores(i, 0)
        step(i, 0, True, (i - 1, 1))
        step(i - 1, 1, False, (clamp(i - 2), 0))
        walk(i - 2, 0)


def _write_heads(o_ref, outs_t):
    o_ref[...] = jnp.concatenate(outs_t, axis=0).T.astype(o_ref.dtype)


def _fox_parts(q_ref, k_ref, vt_ref, cfb_ref, st_ref, m_ref, acc_ref, t):
    q = q_ref[...]
    qh = [jnp.where(mk, q, jnp.zeros_like(q)) for mk in _head_masks(t)]
    m_ref[...] = jnp.full_like(m_ref, -jnp.inf)
    acc_ref[...] = jnp.zeros_like(acc_ref)
    key = lax.broadcasted_iota(jnp.int32, (t, LANES), 0)
    qry = lax.broadcasted_iota(jnp.int32, (t, LANES), 1)
    pieces = _lane_pieces(t)
    ones_rows = jnp.ones((BF16_ROWS, t), BF16)

    def scores(j, slot, hh):
        off = pl.multiple_of(j * t, t)
        st = _dot_nt(k_ref[pl.ds(off, t), :], qh[hh])
        fcol = cfb_ref[hh, pl.ds(off, t), :]
        for cols in pieces:
            st_ref[slot, hh, :, cols] = st[:, cols] - fcol

    def consume(j, slot, masked, hh):
        off = pl.multiple_of(j * t, t)
        m_old = m_ref[hh]
        m_new, alpha, prob = [], [], []
        for c, cols in enumerate(pieces):
            x = st_ref[slot, hh, :, cols]
            if masked:
                x = jnp.where(key <= qry + c * LANES, x, -jnp.inf)
            m_o = m_old[:, cols]
            m_n = jnp.maximum(m_o, jnp.max(x, axis=0, keepdims=True))
            m_new.append(m_n)
            alpha.append(jnp.exp2(m_o - m_n))
            prob.append(jnp.exp2(x - m_n).astype(BF16))
        vt_h = jnp.concatenate(
            [vt_ref[hh * HEAD_DIM:(hh + 1) * HEAD_DIM, pl.ds(off, t)], ones_rows], axis=0)
        pv = jnp.dot(vt_h, jnp.concatenate(prob, axis=1), preferred_element_type=F32)
        acc_ref[hh] = jnp.concatenate(alpha, axis=1) * acc_ref[hh] + pv
        m_ref[hh] = jnp.concatenate(m_new, axis=1)

    def result():
        return [acc_ref[hh, 0:HEAD_DIM, :] / acc_ref[hh, HEAD_DIM:HEAD_DIM + 1, :]
                for hh in range(HEADS_PER_BLOCK)]

    return scores, consume, result


def _sb_parts(q_ref, k_ref, vt_ref, tri_ref, z_ref, r_ref, acc_ref, t, n_heads):
    group_w = min(n_heads * HEAD_DIM, MXU_DIM)
    per_group = group_w // HEAD_DIM
    masks = _head_masks(t, per_group)
    groups = [slice((hh // per_group) * group_w, (hh // per_group + 1) * group_w)
              for hh in range(n_heads)]
    qh = []
    for hh in range(n_heads):
        qg = q_ref[:, groups[hh]]
        qh.append(jnp.where(masks[hh % per_group], qg, jnp.zeros_like(qg)))
    r_ref[...] = jnp.zeros_like(r_ref)
    acc_ref[...] = jnp.zeros_like(acc_ref)
    key = lax.broadcasted_iota(jnp.int32, (t, t), 0)
    qry = lax.broadcasted_iota(jnp.int32, (t, t), 1)
    strict = key < qry
    n_sub = t // MXU_DIM

    sub_rows = [slice(c * MXU_DIM, (c + 1) * MXU_DIM) for c in range(n_sub)]

    def scores(j, slot, hh):
        off = pl.multiple_of(j * t, t)
        z_ref[slot, hh] = _dot_nt(k_ref[pl.ds(off, t), groups[hh]], qh[hh])

    def gate_sums(slot, masked, hh):
        z = z_ref[slot, hh]
        a = jnp.maximum(LOG2E * jnp.log(1.0 + jnp.exp2(jnp.minimum(z, EXP2_CLAMP))), z)
        base = z - a
        if masked:
            a = jnp.where(strict, a, 0.0)
            base = jnp.where(strict, base, -jnp.inf)
        tri = tri_ref[...]
        sums = [jnp.dot(tri, a[rows, :].astype(BF16), preferred_element_type=F32)
                for rows in sub_rows]
        return base, sums

    def accumulate(j, staged, hh):
        off = pl.multiple_of(j * t, t)
        base, sums = staged
        r_run = r_ref[hh]
        w = [None] * n_sub
        for c in reversed(range(n_sub)):
            later = sums[c][0:MXU_DIM, :]
            w[c] = jnp.exp2(base[sub_rows[c], :] - later - r_run).astype(BF16)
            r_run = r_run + sums[c][MXU_DIM:MXU_DIM + 1, :]
        vt_h = vt_ref[hh * HEAD_DIM:(hh + 1) * HEAD_DIM, pl.ds(off, t)]
        acc_ref[hh] += jnp.dot(vt_h, jnp.concatenate(w, axis=0), preferred_element_type=F32)
        r_ref[hh] = r_run

    def result():
        return [acc_ref[hh] for hh in range(n_heads)]

    return scores, gate_sums, accumulate, result


def _fox_kernel(q_ref, k_ref, vt_ref, cf_ref, o_ref, cfb_ref, kmax_ref, st_ref, m_ref, acc_ref,
                *, t):
    p = pl.program_id(1)
    i = pl.program_id(2)
    heads = range(HEADS_PER_BLOCK)
    lane_row = lax.broadcasted_iota(jnp.int32, (LANES, LANES), 0)
    head_rows = [((lane_row >= HEAD_DIM * hh) & (lane_row < HEAD_DIM * (hh + 1))).astype(BF16)
                 for hh in heads]

    @pl.when(i == 0)
    def _():
        for hh in heads:
            sel = (lane_row == p * HEADS_PER_BLOCK + hh).astype(BF16)

            def fill(r, kmax2):
                off = pl.multiple_of(r * t, t)
                hi, mid, lo = _split3(cf_ref[pl.ds(off, t), :])
                cfb_ref[hh, pl.ds(off, t), :] = LOG2E * (
                    jnp.dot(hi, sel, preferred_element_type=F32)
                    + jnp.dot(mid, sel, preferred_element_type=F32)
                    + jnp.dot(lo, sel, preferred_element_type=F32))
                kt = k_ref[pl.ds(off, t), :].astype(F32)
                ksq = jnp.dot((kt * kt).astype(BF16), head_rows[hh],
                              preferred_element_type=F32)
                return jnp.maximum(kmax2, ksq)

            kmax2 = lax.fori_loop(0, cf_ref.shape[0] // t, fill, jnp.zeros((t, LANES), F32))
            kmax_ref[hh] = jnp.max(kmax2, axis=0, keepdims=True)

    scores_h, consume_h, result = _fox_parts(q_ref, k_ref, vt_ref, cfb_ref, st_ref, m_ref,
                                             acc_ref, t)

    q32 = q_ref[...].astype(F32)
    ones_lhs = jnp.ones((BF16_ROWS, LANES), BF16)
    qk_bound = []
    for hh, mk in zip(heads, _head_masks(t)):
        qsq = jnp.where(mk, q32 * q32, 0.0).astype(BF16)
        qn2 = _dot_nt(ones_lhs, qsq)[0:1, :]
        qk_bound.append(NORM_SLACK * jnp.sqrt(qn2 * kmax_ref[hh][:, 0:1]))

    def more_needed(j):
        row = (j + 1) * t - 1
        need = jnp.int32(0)
        for hh in heads:
            f_hi = cfb_ref[hh, pl.ds(row, 1), :][:, 0:1]
            gap = jnp.max(qk_bound[hh] - f_hi - m_ref[hh])
            need = need | jnp.logical_not(gap <= -FOX_EXIT).astype(jnp.int32)
        return need

    def scores(j, slot):
        for hh in heads:
            scores_h(j, slot, hh)

    def step(j, slot, masked, nxt):
        for hh in heads:
            if nxt is not None:
                scores_h(*nxt, hh)
            consume_h(j, slot, masked, hh)

    _sweep_key_tiles(i, scores, step, more_needed, test_every_tile=False, start_paired=False)
    _write_heads(o_ref, result())


def _sb_kernel(q_ref, k_ref, vt_ref, tri_ref, o_ref, z_ref, r_ref, acc_ref, *, t, n_heads):
    i = pl.program_id(2)
    scores_h, gate_sums_h, accumulate_h, result = _sb_parts(
        q_ref, k_ref, vt_ref, tri_ref, z_ref, r_ref, acc_ref, t, n_heads)
    heads = range(n_heads)

    def scores(j, slot):
        for hh in heads:
            scores_h(j, slot, hh)

    def step(j, slot, masked, nxt):
        staged = []
        for hh in heads:
            if nxt is not None:
                scores_h(*nxt, hh)
            staged.append(gate_sums_h(slot, masked, hh))
        for hh in heads:
            accumulate_h(j, staged[hh], hh)

    def more_needed(_):
        return (jnp.min(r_ref[...]) < SB_EXIT).astype(jnp.int32)

    _sweep_key_tiles(i, scores, step, more_needed, test_every_tile=True, start_paired=True)
    _write_heads(o_ref, result())


def _attention(q, k, vt, cf):
    b, s, _ = q.shape
    rr = lax.broadcasted_iota(jnp.int32, (MXU_DIM + BF16_ROWS, MXU_DIM), 0)
    cc = lax.broadcasted_iota(jnp.int32, (MXU_DIM + BF16_ROWS, MXU_DIM), 1)
    tri = ((cc > rr) | (rr >= MXU_DIM)).astype(BF16)

    def specs(t, w, first):
        return [pl.BlockSpec((None, t, w), lambda bi, p, i: (bi, i, p + first)),
                pl.BlockSpec((None, s, w), lambda bi, p, i: (bi, 0, p + first)),
                pl.BlockSpec((None, w, s), lambda bi, p, i: (bi, p + first, 0))]

    def out_spec(t, w):
        return pl.BlockSpec((None, t, w), lambda bi, p, i: (bi, i, p))

    params = _params(("arbitrary", "arbitrary", "arbitrary"))
    t = T_FOX
    o_fox = pl.pallas_call(
        functools.partial(_fox_kernel, t=t),
        grid=(b, H_FOX // HEADS_PER_BLOCK, s // t),
        in_specs=specs(t, LANES, 0) + [
            pl.BlockSpec((None, s, LANES), lambda bi, p, i: (bi, 0, 0),
                         pipeline_mode=pl.Buffered(1))],
        out_specs=out_spec(t, LANES),
        out_shape=jax.ShapeDtypeStruct((b, s, H_FOX * HEAD_DIM), BF16),
        scratch_shapes=[pltpu.VMEM((HEADS_PER_BLOCK, s, LANES), F32),
                        pltpu.VMEM((HEADS_PER_BLOCK, 1, LANES), F32),
                        pltpu.VMEM((2, HEADS_PER_BLOCK, t, t), F32),
                        pltpu.VMEM((HEADS_PER_BLOCK, 1, t), F32),
                        pltpu.VMEM((HEADS_PER_BLOCK, HEAD_DIM + BF16_ROWS, t), F32)],
        compiler_params=params,
        name="fox_attention",
    )(q, k, vt, cf)
    t = T_SB
    nh = SB_HEADS_PER_STEP
    w = nh * HEAD_DIM
    o_sb = pl.pallas_call(
        functools.partial(_sb_kernel, t=t, n_heads=nh),
        grid=(b, H_SB // nh, s // t),
        in_specs=specs(t, w, H_FOX // nh) + [_resident(tri.shape)],
        out_specs=out_spec(t, w),
        out_shape=jax.ShapeDtypeStruct((b, s, H_SB * HEAD_DIM), BF16),
        scratch_shapes=[pltpu.VMEM((2, nh, t, t), F32),
                        pltpu.VMEM((nh, 1, t), F32),
                        pltpu.VMEM((nh, HEAD_DIM, t), F32)],
        compiler_params=params,
        name="sb_attention",
    )(q, k, vt, tri)
    return o_fox, o_sb


def _conv_mixer_kernel(h_ref, g_ref, win_ref, cw_ref, o_ref, ext_ref, tail_ref, *, tm):
    i = pl.program_id(1)

    @pl.when(i == 0)
    def _():
        tail_ref[...] = jnp.zeros_like(tail_ref)

    x = h_ref[...]
    xn = _rms_norm(x, g_ref[...]).astype(BF16)
    d = x.shape[-1]
    gate_c = jnp.dot(xn, win_ref[:, d:2 * d], preferred_element_type=F32)
    u = jnp.dot(xn, win_ref[:, 2 * d:3 * d], preferred_element_type=F32)
    cu = gate_c * u
    y = _causal_conv3(ext_ref, tail_ref[...], cu, cw_ref[...], tm)
    tail_ref[...] = cu[tm - CONV_HALO:tm, :]
    gate_b = jnp.dot(xn, win_ref[:, 0:d], preferred_element_type=F32)
    o_ref[...] = (gate_b * y).astype(o_ref.dtype)


def _conv_mixer(h, g, w_in, conv_w):
    b, s, d = h.shape
    tm = TM_PROJ
    return pl.pallas_call(
        functools.partial(_conv_mixer_kernel, tm=tm),
        grid=(b, s // tm),
        in_specs=[
            pl.BlockSpec((None, tm, d), lambda bi, i: (bi, i, 0)),
            _resident(g.shape), _resident(w_in.shape), _resident(conv_w.shape),
        ],
        out_specs=pl.BlockSpec((None, tm, d), lambda bi, i: (bi, i, 0)),
        out_shape=jax.ShapeDtypeStruct(h.shape, BF16),
        scratch_shapes=[pltpu.VMEM((tm + CONV_HALO, d), F32),
                        pltpu.VMEM((CONV_HALO, d), F32)],
        compiler_params=_params(("arbitrary", "arbitrary")),
        name="conv_mixer",
    )(h, g, w_in, conv_w)


def _ffn_kernel(h_ref, a0_ref, a1_ref, w0_ref, w1_ref, g_ref, wup_ref, cw_ref, wd_ref, o_ref,
                xn_ref, ext_ref, tail_ref, act_ref, *, tm, fc):
    i = pl.program_id(1)
    n_chunks = D_FF // fc

    @pl.when(i == 0)
    def _():
        tail_ref[...] = jnp.zeros_like(tail_ref)

    x = (h_ref[...]
         + jnp.dot(a0_ref[...], w0_ref[...], preferred_element_type=F32)
         + jnp.dot(a1_ref[...], w1_ref[...], preferred_element_type=F32))
    xn_ref[...] = _rms_norm(x, g_ref[...]).astype(BF16)
    o_ref[...] = x

    def up_project(c):
        slot = c % 2
        for part in range(2):
            cols = slice(part * D_FF + c * fc, part * D_FF + (c + 1) * fc)
            ext_ref[slot, part, 0:CONV_HALO, :] = tail_ref[part, :, c * fc:(c + 1) * fc]
            ext_ref[slot, part, CONV_HALO:CONV_HALO + tm, :] = jnp.dot(
                xn_ref[...], wup_ref[:, cols], preferred_element_type=F32)

    def conv_gate(c):
        slot = c % 2
        y = []
        for part in range(2):
            cols = slice(part * D_FF + c * fc, part * D_FF + (c + 1) * fc)
            w = cw_ref[:, cols]
            taps = [w[k:k + 1, :] * ext_ref[slot, part, CONV_HALO - 2 + k:CONV_HALO - 2 + k + tm, :]
                    for k in range(3)]
            y.append(taps[2] + taps[1] + taps[0])
            tail_ref[part, :, c * fc:(c + 1) * fc] = ext_ref[slot, part, tm:tm + CONV_HALO, :]
        half_g = 0.5 * y[0]
        act_ref[:, c * fc:(c + 1) * fc] = (
            (half_g + half_g * jnp.tanh(half_g)) * y[1]).astype(BF16)

    def down_project(rows):
        o_ref[...] += jnp.dot(act_ref[:, rows], wd_ref[rows, :], preferred_element_type=F32)

    head = n_chunks - DOWN_TAIL
    up_project(0)
    for c in range(n_chunks):
        if c + 1 < n_chunks:
            up_project(c + 1)
        if c == head:
            down_project(slice(0, head * fc))
        conv_gate(c)
    down_project(slice(head * fc, D_FF))


def _mixer_out_ffn(h, mix, w_mix, g, w_up, conv_w, w_down):
    b, s, d = h.shape
    tm = TM_FFN
    fc = FF_CHUNK
    half = d // 2
    (a0, blk0), (a1, blk1) = mix
    w0, w1 = w_mix[:half], w_mix[half:]

    def half_spec(blk):
        return pl.BlockSpec((None, tm, half), lambda bi, i: (bi, i, blk))

    return pl.pallas_call(
        functools.partial(_ffn_kernel, tm=tm, fc=fc),
        grid=(b, s // tm),
        in_specs=[
            pl.BlockSpec((None, tm, d), lambda bi, i: (bi, i, 0)),
            half_spec(blk0), half_spec(blk1), _resident(w0.shape), _resident(w1.shape),
            _resident(g.shape), _resident(w_up.shape), _resident(conv_w.shape),
            _resident(w_down.shape),
        ],
        out_specs=pl.BlockSpec((None, tm, d), lambda bi, i: (bi, i, 0)),
        out_shape=jax.ShapeDtypeStruct(h.shape, F32),
        scratch_shapes=[pltpu.VMEM((tm, d), BF16),
                        pltpu.VMEM((2, 2, tm + CONV_HALO, fc), F32),
                        pltpu.VMEM((2, CONV_HALO, D_FF), F32),
                        pltpu.VMEM((tm, D_FF), BF16)],
        compiler_params=_params(("arbitrary", "arbitrary")),
        name="conv_ffn",
    )(h, a0, a1, w0, w1, g, w_up, conv_w, w_down)


def kernel(x, attn_norm, attn_w_in, attn_f_bias, fox_q_gain, fox_k_gain, sb_q_gain, sb_k_gain,
           attn_w_out, conv_norm, conv_w_in, conv_kernel, conv_w_out, ffn_norm, ffn_w_up,
           ffn_conv, ffn_w_down):
    depth = ffn_norm.shape[0]
    h = x
    for layer in range(depth):
        i = layer // 2
        if layer % 2 == 0:
            w_in = attn_w_in[i]
            w_qk = w_in[:, :2 * MIX_WIDTH].astype(BF16)
            wvt = w_in[:, 2 * MIX_WIDTH:3 * MIX_WIDTH].T.astype(BF16)
            wf = jnp.zeros((D_MODEL, LANES), BF16).at[:, :H_FOX].set(
                w_in[:, 3 * MIX_WIDTH:].astype(BF16))
            fb = jnp.zeros((1, LANES), F32).at[0, :H_FOX].set(attn_f_bias[i])
            qk_gain = jnp.concatenate(
                [jnp.tile(fox_q_gain[i], H_FOX), jnp.tile(sb_q_gain[i], H_SB),
                 jnp.tile(fox_k_gain[i], H_FOX), jnp.tile(sb_k_gain[i], H_SB)])[None, :]
            q, k, vt, cf = _attn_inproj(h, attn_norm[i][None, :], w_qk, wvt, wf, fb, qk_gain)
            o_fox, o_sb = _attention(q, k, vt, cf)
            mix = ((o_fox, 0), (o_sb, 0))
            w_mix = attn_w_out[i]
        else:
            y = _conv_mixer(h, conv_norm[i][None, :], conv_w_in[i].astype(BF16), conv_kernel[i])
            mix = ((y, 0), (y, 1))
            w_mix = conv_w_out[i]
        h = _mixer_out_ffn(h, mix, w_mix.astype(BF16), ffn_norm[layer][None, :],
                           ffn_w_up[layer].astype(BF16), ffn_conv[layer],
                           ffn_w_down[layer].astype(BF16))
    return h
```

```python
import functools

import jax
import jax.numpy as jnp
from jax import lax
from jax.experimental import pallas as pl
from jax.experimental.pallas import tpu as pltpu

F32 = jnp.float32
BF16 = jnp.bfloat16

D_MODEL = 1024
HEAD_DIM = 64
H_FOX = 8
H_SB = 8
MIX_WIDTH = (H_FOX + H_SB) * HEAD_DIM
D_FF = 2816
EPS = 1e-6
QK_SCALE = HEAD_DIM ** -0.5
LOG2E = 1.4426950408889634

LANES = 128
SUBLANES = 8
HEADS_PER_BLOCK = LANES // HEAD_DIM
BF16_ROWS = 2 * SUBLANES
MXU_DIM = 256
EXP2_CLAMP = 126.0
VMEM_LIMIT = 56 * 1024 * 1024

TM_PROJ = 256
TM_MIXER = 512
TM_FFN = 512
FF_CHUNK = 256
DOWN_TAIL = 2
T_FOX = 512
T_SB = 256
SB_HEADS_PER_STEP = 8
SB_EXIT = 160.0
FOX_EXIT = 160.0
NORM_SLACK = 1.01
CONV_HALO = SUBLANES


def _params(sem):
    return pltpu.CompilerParams(dimension_semantics=sem, vmem_limit_bytes=VMEM_LIMIT)


def _resident(shape):
    zeros = (0,) * len(shape)
    return pl.BlockSpec(shape, lambda *_: zeros, pipeline_mode=pl.Buffered(1))


def _rms_norm(x, g):
    ms = jnp.mean(x * x, axis=-1, keepdims=True)
    return x * lax.rsqrt(ms + EPS) * g


def _split3(x):
    hi = x.astype(BF16)
    r = x - hi.astype(F32)
    mid = r.astype(BF16)
    lo = (r - mid.astype(F32)).astype(BF16)
    return hi, mid, lo


def _dot_nt(a, b):
    return lax.dot_general(a, b, (((1,), (1,)), ((), ())), preferred_element_type=F32)


def _causal_conv3(ext_ref, tail, cur, w, tm):
    ext_ref[0:CONV_HALO, :] = tail
    ext_ref[CONV_HALO:CONV_HALO + tm, :] = cur
    return (w[2:3, :] * cur
            + w[1:2, :] * ext_ref[CONV_HALO - 1:CONV_HALO - 1 + tm, :]
            + w[0:1, :] * ext_ref[CONV_HALO - 2:CONV_HALO - 2 + tm, :])


def _attn_inproj_kernel(h_ref, g_ref, w_ref, wvt_ref, wf_ref, fb_ref, qkg_ref, gmat_ref,
                        tri_ref, q_ref, k_ref, vt_ref, cf_ref, carry_ref, *, tm):
    i = pl.program_id(1)

    @pl.when(i == 0)
    def _():
        carry_ref[...] = jnp.zeros_like(carry_ref)

    xn = _rms_norm(h_ref[...], g_ref[...]).astype(BF16)
    gmat = gmat_ref[...]

    n_blk = MIX_WIDTH // MXU_DIM
    proj = [jnp.dot(xn, w_ref[:, c * MXU_DIM:(c + 1) * MXU_DIM], preferred_element_type=F32)
            for c in range(2 * n_blk)]
    vt_ref[...] = _dot_nt(wvt_ref[...], xn).astype(BF16)
    fl = jnp.dot(xn, wf_ref[...], preferred_element_type=F32) + fb_ref[...]

    for c in range(2 * n_blk):
        out_ref, scale = (q_ref, QK_SCALE * LOG2E) if c < n_blk else (k_ref, 1.0)
        t = proj[c]
        ssq = jnp.dot((t * t).astype(BF16), gmat, preferred_element_type=F32)
        gain = qkg_ref[:, c * MXU_DIM:(c + 1) * MXU_DIM]
        tn = t * lax.rsqrt(ssq * (1.0 / HEAD_DIM) + EPS) * gain
        if scale != 1.0:
            tn = tn * scale
        lo = (c % n_blk) * MXU_DIM
        out_ref[:, lo:lo + MXU_DIM] = tn.astype(BF16)

    log_f = jnp.minimum(fl, 0.0) - jnp.log1p(jnp.exp(-jnp.abs(fl)))
    tri = tri_ref[...]
    hi, mid, lo = _split3(log_f)
    cs = (jnp.dot(tri, hi, preferred_element_type=F32)
          + jnp.dot(tri, mid, preferred_element_type=F32)
          + jnp.dot(tri, lo, preferred_element_type=F32))
    cf = cs + carry_ref[0:1, :]
    carry_ref[...] = jnp.broadcast_to(cf[tm - 1:tm, :], carry_ref.shape)
    cf_ref[...] = cf


def _attn_inproj(h, g, w_qk, wvt, wf, fb, qk_gain):
    b, s, d = h.shape
    tm = TM_PROJ
    r = lax.broadcasted_iota(jnp.int32, (MXU_DIM, MXU_DIM), 0) // HEAD_DIM
    c = lax.broadcasted_iota(jnp.int32, (MXU_DIM, MXU_DIM), 1) // HEAD_DIM
    gmat = (r == c).astype(BF16)
    rr = lax.broadcasted_iota(jnp.int32, (tm, tm), 0)
    cc = lax.broadcasted_iota(jnp.int32, (tm, tm), 1)
    tri = (cc <= rr).astype(BF16)
    row_spec = pl.BlockSpec((None, tm, MIX_WIDTH), lambda bi, i: (bi, i, 0))
    return pl.pallas_call(
        functools.partial(_attn_inproj_kernel, tm=tm),
        grid=(b, s // tm),
        in_specs=[
            pl.BlockSpec((None, tm, d), lambda bi, i: (bi, i, 0)),
            _resident(g.shape), _resident(w_qk.shape), _resident(wvt.shape),
            _resident(wf.shape), _resident(fb.shape), _resident(qk_gain.shape),
            _resident(gmat.shape), _resident(tri.shape),
        ],
        out_specs=[row_spec, row_spec,
                   pl.BlockSpec((None, MIX_WIDTH, tm), lambda bi, i: (bi, 0, i)),
                   pl.BlockSpec((None, tm, LANES), lambda bi, i: (bi, i, 0))],
        out_shape=[jax.ShapeDtypeStruct((b, s, MIX_WIDTH), BF16),
                   jax.ShapeDtypeStruct((b, s, MIX_WIDTH), BF16),
                   jax.ShapeDtypeStruct((b, MIX_WIDTH, s), BF16),
                   jax.ShapeDtypeStruct((b, s, LANES), F32)],
        scratch_shapes=[pltpu.VMEM((SUBLANES, LANES), F32)],
        compiler_params=_params(("arbitrary", "arbitrary")),
        name="attn_inproj",
    )(h, g, w_qk, wvt, wf, fb, qk_gain, gmat, tri)


def _head_masks(t, n_heads=HEADS_PER_BLOCK):
    lane = lax.broadcasted_iota(jnp.int32, (t, n_heads * HEAD_DIM), 1)
    return [(lane >= HEAD_DIM * hh) & (lane < HEAD_DIM * (hh + 1)) for hh in range(n_heads)]


def _lane_pieces(t):
    return [slice(c * LANES, (c + 1) * LANES) for c in range(t // LANES)]


def _sweep_key_tiles(i, scores, step, more_needed, test_every_tile, start_paired):
    def clamp(j):
        return jnp.maximum(j, 0)

    def walk(first, slot_a):
        slot_b = 1 - slot_a
        count = first + 1

        def unfinished(carry):
            r, go = carry
            return jnp.logical_and(r < count // 2, go > 0)

        def pair(carry):
            r, _ = carry
            j = first - 2 * r
            step(j, slot_a, False, (clamp(j - 1), slot_b))
            if not test_every_tile:
                step(j - 1, slot_b, False, (clamp(j - 2), slot_a))
                return r + 1, more_needed(clamp(j - 2))
            go_mid = more_needed(j - 1)

            @pl.when(go_mid > 0)
            def _():
                step(j - 1, slot_b, False, (clamp(j - 2), slot_a))

            return r + 1, go_mid & more_needed(clamp(j - 2))

        _, go = lax.while_loop(unfinished, pair, (jnp.int32(0), more_needed(clamp(first))))

        @pl.when(jnp.logical_and(count % 2 == 1, go > 0))
        def _():
            step(0, slot_a, False, None)

    if not start_paired:
        scores(i, 0)
        step(i, 0, True, (clamp(i - 1), 1))
        walk(i - 1, 1)
        return

    @pl.when(i == 0)
    def _():
        scores(0, 0)
        step(0, 0, True, None)

    @pl.when(i > 0)
    def _():
        scores(i, 0)
        step(i, 0, True, (i - 1, 1))
        step(i - 1, 1, False, (clamp(i - 2), 0))
        walk(i - 2, 0)


def _write_heads(o_ref, outs_t):
    o_ref[...] = jnp.concatenate(outs_t, axis=0).T.astype(o_ref.dtype)


def _fox_parts(q_ref, k_ref, vt_ref, cfb_ref, st_ref, m_ref, acc_ref, t):
    q = q_ref[...]
    qh = [jnp.where(mk, q, jnp.zeros_like(q)) for mk in _head_masks(t)]
    m_ref[...] = jnp.full_like(m_ref, -jnp.inf)
    acc_ref[...] = jnp.zeros_like(acc_ref)
    key = lax.broadcasted_iota(jnp.int32, (t, LANES), 0)
    qry = lax.broadcasted_iota(jnp.int32, (t, LANES), 1)
    pieces = _lane_pieces(t)
    ones_rows = jnp.ones((BF16_ROWS, t), BF16)

    def scores(j, slot, hh):
        off = pl.multiple_of(j * t, t)
        st = _dot_nt(k_ref[pl.ds(off, t), :], qh[hh])
        fcol = cfb_ref[hh, pl.ds(off, t), :]
        for cols in pieces:
            st_ref[slot, hh, :, cols] = st[:, cols] - fcol

    def consume(j, slot, masked, hh):
        off = pl.multiple_of(j * t, t)
        m_old = m_ref[hh]
        m_new, alpha, prob = [], [], []
        for c, cols in enumerate(pieces):
            x = st_ref[slot, hh, :, cols]
            if masked:
                x = jnp.where(key <= qry + c * LANES, x, -jnp.inf)
            m_o = m_old[:, cols]
            m_n = jnp.maximum(m_o, jnp.max(x, axis=0, keepdims=True))
            m_new.append(m_n)
            alpha.append(jnp.exp2(m_o - m_n))
            prob.append(jnp.exp2(x - m_n).astype(BF16))
        vt_h = jnp.concatenate(
            [vt_ref[hh * HEAD_DIM:(hh + 1) * HEAD_DIM, pl.ds(off, t)], ones_rows], axis=0)
        pv = jnp.dot(vt_h, jnp.concatenate(prob, axis=1), preferred_element_type=F32)
        acc_ref[hh] = jnp.concatenate(alpha, axis=1) * acc_ref[hh] + pv
        m_ref[hh] = jnp.concatenate(m_new, axis=1)

    def result():
        return [acc_ref[hh, 0:HEAD_DIM, :] / acc_ref[hh, HEAD_DIM:HEAD_DIM + 1, :]
                for hh in range(HEADS_PER_BLOCK)]

    return scores, consume, result


def _sb_parts(q_ref, k_ref, vt_ref, tri_ref, z_ref, r_ref, acc_ref, t, n_heads):
    group_w = min(n_heads * HEAD_DIM, MXU_DIM)
    per_group = group_w // HEAD_DIM
    masks = _head_masks(t, per_group)
    groups = [slice((hh // per_group) * group_w, (hh // per_group + 1) * group_w)
              for hh in range(n_heads)]
    qh = []
    for hh in range(n_heads):
        qg = q_ref[:, groups[hh]]
        qh.append(jnp.where(masks[hh % per_group], qg, jnp.zeros_like(qg)))
    r_ref[...] = jnp.zeros_like(r_ref)
    acc_ref[...] = jnp.zeros_like(acc_ref)
    key = lax.broadcasted_iota(jnp.int32, (t, t), 0)
    qry = lax.broadcasted_iota(jnp.int32, (t, t), 1)
    strict = key < qry
    n_sub = t // MXU_DIM

    sub_rows = [slice(c * MXU_DIM, (c + 1) * MXU_DIM) for c in range(n_sub)]

    def scores(j, slot, hh):
        off = pl.multiple_of(j * t, t)
        z_ref[slot, hh] = _dot_nt(k_ref[pl.ds(off, t), groups[hh]], qh[hh])

    def gate_sums(slot, masked, hh):
        z = z_ref[slot, hh]
        a = jnp.maximum(LOG2E * jnp.log(1.0 + jnp.exp2(jnp.minimum(z, EXP2_CLAMP))), z)
        base = z - a
        if masked:
            a = jnp.where(strict, a, 0.0)
            base = jnp.where(strict, base, -jnp.inf)
        tri = tri_ref[...]
        sums = [jnp.dot(tri, a[rows, :].astype(BF16), preferred_element_type=F32)
                for rows in sub_rows]
        return base, sums

    def accumulate(j, staged, hh):
        off = pl.multiple_of(j * t, t)
        base, sums = staged
        r_run = r_ref[hh]
        w = [None] * n_sub
        for c in reversed(range(n_sub)):
            later = sums[c][0:MXU_DIM, :]
            w[c] = jnp.exp2(base[sub_rows[c], :] - later - r_run).astype(BF16)
            r_run = r_run + sums[c][MXU_DIM:MXU_DIM + 1, :]
        vt_h = vt_ref[hh * HEAD_DIM:(hh + 1) * HEAD_DIM, pl.ds(off, t)]
        acc_ref[hh] += jnp.dot(vt_h, jnp.concatenate(w, axis=0), preferred_element_type=F32)
        r_ref[hh] = r_run

    def result():
        return [acc_ref[hh] for hh in range(n_heads)]

    return scores, gate_sums, accumulate, result


def _fox_kernel(q_ref, k_ref, vt_ref, cf_ref, o_ref, cfb_ref, kmax_ref, st_ref, m_ref, acc_ref,
                *, t):
    p = pl.program_id(1)
    i = pl.program_id(2)
    heads = range(HEADS_PER_BLOCK)
    lane_row = lax.broadcasted_iota(jnp.int32, (LANES, LANES), 0)
    head_rows = [((lane_row >= HEAD_DIM * hh) & (lane_row < HEAD_DIM * (hh + 1))).astype(BF16)
                 for hh in heads]

    @pl.when(i == 0)
    def _():
        for hh in heads:
            sel = (lane_row == p * HEADS_PER_BLOCK + hh).astype(BF16)

            def fill(r, kmax2):
                off = pl.multiple_of(r * t, t)
                hi, mid, lo = _split3(cf_ref[pl.ds(off, t), :])
                cfb_ref[hh, pl.ds(off, t), :] = LOG2E * (
                    jnp.dot(hi, sel, preferred_element_type=F32)
                    + jnp.dot(mid, sel, preferred_element_type=F32)
                    + jnp.dot(lo, sel, preferred_element_type=F32))
                kt = k_ref[pl.ds(off, t), :].astype(F32)
                ksq = jnp.dot((kt * kt).astype(BF16), head_rows[hh],
                              preferred_element_type=F32)
                return jnp.maximum(kmax2, ksq)

            kmax2 = lax.fori_loop(0, cf_ref.shape[0] // t, fill, jnp.zeros((t, LANES), F32))
            kmax_ref[hh] = jnp.max(kmax2, axis=0, keepdims=True)

    scores_h, consume_h, result = _fox_parts(q_ref, k_ref, vt_ref, cfb_ref, st_ref, m_ref,
                                             acc_ref, t)

    q32 = q_ref[...].astype(F32)
    ones_lhs = jnp.ones((BF16_ROWS, LANES), BF16)
    qk_bound = []
    for hh, mk in zip(heads, _head_masks(t)):
        qsq = jnp.where(mk, q32 * q32, 0.0).astype(BF16)
        qn2 = _dot_nt(ones_lhs, qsq)[0:1, :]
        qk_bound.append(NORM_SLACK * jnp.sqrt(qn2 * kmax_ref[hh][:, 0:1]))

    def more_needed(j):
        row = (j + 1) * t - 1
        need = jnp.int32(0)
        for hh in heads:
            f_hi = cfb_ref[hh, pl.ds(row, 1), :][:, 0:1]
            gap = jnp.max(qk_bound[hh] - f_hi - m_ref[hh])
            need = need | jnp.logical_not(gap <= -FOX_EXIT).astype(jnp.int32)
        return need

    def scores(j, slot):
        for hh in heads:
            scores_h(j, slot, hh)

    def step(j, slot, masked, nxt):
        for hh in heads:
            if nxt is not None:
                scores_h(*nxt, hh)
            consume_h(j, slot, masked, hh)

    _sweep_key_tiles(i, scores, step, more_needed, test_every_tile=False, start_paired=False)
    _write_heads(o_ref, result())


def _sb_kernel(q_ref, k_ref, vt_ref, tri_ref, o_ref, z_ref, r_ref, acc_ref, *, t, n_heads):
    i = pl.program_id(2)
    scores_h, gate_sums_h, accumulate_h, result = _sb_parts(
        q_ref, k_ref, vt_ref, tri_ref, z_ref, r_ref, acc_ref, t, n_heads)
    heads = range(n_heads)

    def scores(j, slot):
        for hh in heads:
            scores_h(j, slot, hh)

    def step(j, slot, masked, nxt):
        staged = []
        for hh in heads:
            if nxt is not None:
                scores_h(*nxt, hh)
            staged.append(gate_sums_h(slot, masked, hh))
        for hh in heads:
            accumulate_h(j, staged[hh], hh)

    def more_needed(_):
        return (jnp.min(r_ref[...]) < SB_EXIT).astype(jnp.int32)

    _sweep_key_tiles(i, scores, step, more_needed, test_every_tile=True, start_paired=True)
    _write_heads(o_ref, result())


def _attention(q, k, vt, cf):
    b, s, _ = q.shape
    rr = lax.broadcasted_iota(jnp.int32, (MXU_DIM + BF16_ROWS, MXU_DIM), 0)
    cc = lax.broadcasted_iota(jnp.int32, (MXU_DIM + BF16_ROWS, MXU_DIM), 1)
    tri = ((cc > rr) | (rr >= MXU_DIM)).astype(BF16)

    def specs(t, w, first):
        return [pl.BlockSpec((None, t, w), lambda bi, p, i: (bi, i, p + first)),
                pl.BlockSpec((None, s, w), lambda bi, p, i: (bi, 0, p + first)),
                pl.BlockSpec((None, w, s), lambda bi, p, i: (bi, p + first, 0))]

    def out_spec(t, w):
        return pl.BlockSpec((None, t, w), lambda bi, p, i: (bi, i, p))

    params = _params(("arbitrary", "arbitrary", "arbitrary"))
    t = T_FOX
    o_fox = pl.pallas_call(
        functools.partial(_fox_kernel, t=t),
        grid=(b, H_FOX // HEADS_PER_BLOCK, s // t),
        in_specs=specs(t, LANES, 0) + [
            pl.BlockSpec((None, s, LANES), lambda bi, p, i: (bi, 0, 0),
                         pipeline_mode=pl.Buffered(1))],
        out_specs=out_spec(t, LANES),
        out_shape=jax.ShapeDtypeStruct((b, s, H_FOX * HEAD_DIM), BF16),
        scratch_shapes=[pltpu.VMEM((HEADS_PER_BLOCK, s, LANES), F32),
                        pltpu.VMEM((HEADS_PER_BLOCK, 1, LANES), F32),
                        pltpu.VMEM((2, HEADS_PER_BLOCK, t, t), F32),
                        pltpu.VMEM((HEADS_PER_BLOCK, 1, t), F32),
                        pltpu.VMEM((HEADS_PER_BLOCK, HEAD_DIM + BF16_ROWS, t), F32)],
        compiler_params=params,
        name="fox_attention",
    )(q, k, vt, cf)
    t = T_SB
    nh = SB_HEADS_PER_STEP
    w = nh * HEAD_DIM
    o_sb = pl.pallas_call(
        functools.partial(_sb_kernel, t=t, n_heads=nh),
        grid=(b, H_SB // nh, s // t),
        in_specs=specs(t, w, H_FOX // nh) + [_resident(tri.shape)],
        out_specs=out_spec(t, w),
        out_shape=jax.ShapeDtypeStruct((b, s, H_SB * HEAD_DIM), BF16),
        scratch_shapes=[pltpu.VMEM((2, nh, t, t), F32),
                        pltpu.VMEM((nh, 1, t), F32),
                        pltpu.VMEM((nh, HEAD_DIM, t), F32)],
        compiler_params=params,
        name="sb_attention",
    )(q, k, vt, tri)
    return o_fox, o_sb


def _conv_mixer_kernel(h_ref, g_ref, win_ref, cw_ref, o_ref, ext_ref, tail_ref, *, tm):
    i = pl.program_id(1)

    @pl.when(i == 0)
    def _():
        tail_ref[...] = jnp.zeros_like(tail_ref)

    x = h_ref[...]
    xn = _rms_norm(x, g_ref[...]).astype(BF16)
    d = x.shape[-1]
    gate_c = jnp.dot(xn, win_ref[:, d:2 * d], preferred_element_type=F32)
    u = jnp.dot(xn, win_ref[:, 2 * d:3 * d], preferred_element_type=F32)
    cu = gate_c * u
    y = _causal_conv3(ext_ref, tail_ref[...], cu, cw_ref[...], tm)
    tail_ref[...] = cu[tm - CONV_HALO:tm, :]
    gate_b = jnp.dot(xn, win_ref[:, 0:d], preferred_element_type=F32)
    o_ref[...] = (gate_b * y).astype(o_ref.dtype)


def _conv_mixer(h, g, w_in, conv_w):
    b, s, d = h.shape
    tm = TM_MIXER
    return pl.pallas_call(
        functools.partial(_conv_mixer_kernel, tm=tm),
        grid=(b, s // tm),
        in_specs=[
            pl.BlockSpec((None, tm, d), lambda bi, i: (bi, i, 0)),
            _resident(g.shape), _resident(w_in.shape), _resident(conv_w.shape),
        ],
        out_specs=pl.BlockSpec((None, tm, d), lambda bi, i: (bi, i, 0)),
        out_shape=jax.ShapeDtypeStruct(h.shape, BF16),
        scratch_shapes=[pltpu.VMEM((tm + CONV_HALO, d), F32),
                        pltpu.VMEM((CONV_HALO, d), F32)],
        compiler_params=_params(("arbitrary", "arbitrary")),
        name="conv_mixer",
    )(h, g, w_in, conv_w)


def _ffn_kernel(h_ref, a0_ref, a1_ref, w0_ref, w1_ref, g_ref, wup_ref, cw_ref, wd_ref, o_ref,
                xn_ref, ext_ref, tail_ref, act_ref, *, tm, fc):
    i = pl.program_id(1)
    n_chunks = D_FF // fc

    @pl.when(i == 0)
    def _():
        tail_ref[...] = jnp.zeros_like(tail_ref)

    x = (h_ref[...]
         + jnp.dot(a0_ref[...], w0_ref[...], preferred_element_type=F32)
         + jnp.dot(a1_ref[...], w1_ref[...], preferred_element_type=F32))
    xn_ref[...] = _rms_norm(x, g_ref[...]).astype(BF16)
    o_ref[...] = x

    def up_project(c):
        slot = c % 2
        for part in range(2):
            cols = slice(part * D_FF + c * fc, part * D_FF + (c + 1) * fc)
            ext_ref[slot, part, 0:CONV_HALO, :] = tail_ref[part, :, c * fc:(c + 1) * fc]
            ext_ref[slot, part, CONV_HALO:CONV_HALO + tm, :] = jnp.dot(
                xn_ref[...], wup_ref[:, cols], preferred_element_type=F32)

    def conv_gate(c):
        slot = c % 2
        y = []
        for part in range(2):
            cols = slice(part * D_FF + c * fc, part * D_FF + (c + 1) * fc)
            w = cw_ref[:, cols]
            taps = [w[k:k + 1, :] * ext_ref[slot, part, CONV_HALO - 2 + k:CONV_HALO - 2 + k + tm, :]
                    for k in range(3)]
            y.append(taps[2] + taps[1] + taps[0])
            tail_ref[part, :, c * fc:(c + 1) * fc] = ext_ref[slot, part, tm:tm + CONV_HALO, :]
        half_g = 0.5 * y[0]
        act_ref[:, c * fc:(c + 1) * fc] = (
            (half_g + half_g * jnp.tanh(half_g)) * y[1]).astype(BF16)

    def down_project(rows):
        o_ref[...] += jnp.dot(act_ref[:, rows], wd_ref[rows, :], preferred_element_type=F32)

    head = n_chunks - DOWN_TAIL
    up_project(0)
    for c in range(n_chunks):
        if c + 1 < n_chunks:
            up_project(c + 1)
        if c == head:
            down_project(slice(0, head * fc))
        conv_gate(c)
    down_project(slice(head * fc, D_FF))


def _mixer_out_ffn(h, mix, w_mix, g, w_up, conv_w, w_down):
    b, s, d = h.shape
    tm = TM_FFN
    fc = FF_CHUNK
    half = d // 2
    (a0, blk0), (a1, blk1) = mix
    w0, w1 = w_mix[:half], w_mix[half:]

    def half_spec(blk):
        return pl.BlockSpec((None, tm, half), lambda bi, i: (bi, i, blk))

    return pl.pallas_call(
        functools.partial(_ffn_kernel, tm=tm, fc=fc),
        grid=(b, s // tm),
        in_specs=[
            pl.BlockSpec((None, tm, d), lambda bi, i: (bi, i, 0)),
            half_spec(blk0), half_spec(blk1), _resident(w0.shape), _resident(w1.shape),
            _resident(g.shape), _resident(w_up.shape), _resident(conv_w.shape),
            _resident(w_down.shape),
        ],
        out_specs=pl.BlockSpec((None, tm, d), lambda bi, i: (bi, i, 0)),
        out_shape=jax.ShapeDtypeStruct(h.shape, F32),
        scratch_shapes=[pltpu.VMEM((tm, d), BF16),
                        pltpu.VMEM((2, 2, tm + CONV_HALO, fc), F32),
                        pltpu.VMEM((2, CONV_HALO, D_FF), F32),
                        pltpu.VMEM((tm, D_FF), BF16)],
        compiler_params=_params(("arbitrary", "arbitrary")),
        name="conv_ffn",
    )(h, a0, a1, w0, w1, g, w_up, conv_w, w_down)


def kernel(x, attn_norm, attn_w_in, attn_f_bias, fox_q_gain, fox_k_gain, sb_q_gain, sb_k_gain,
           attn_w_out, conv_norm, conv_w_in, conv_kernel, conv_w_out, ffn_norm, ffn_w_up,
           ffn_conv, ffn_w_down):
    depth = ffn_norm.shape[0]
    h = x
    for layer in range(depth):
        i = layer // 2
        if layer % 2 == 0:
            w_in = attn_w_in[i]
            w_qk = w_in[:, :2 * MIX_WIDTH].astype(BF16)
            wvt = w_in[:, 2 * MIX_WIDTH:3 * MIX_WIDTH].T.astype(BF16)
            wf = jnp.zeros((D_MODEL, LANES), BF16).at[:, :H_FOX].set(
                w_in[:, 3 * MIX_WIDTH:].astype(BF16))
            fb = jnp.zeros((1, LANES), F32).at[0, :H_FOX].set(attn_f_bias[i])
            qk_gain = jnp.concatenate(
                [jnp.tile(fox_q_gain[i], H_FOX), jnp.tile(sb_q_gain[i], H_SB),
                 jnp.tile(fox_k_gain[i], H_FOX), jnp.tile(sb_k_gain[i], H_SB)])[None, :]
            q, k, vt, cf = _attn_inproj(h, attn_norm[i][None, :], w_qk, wvt, wf, fb, qk_gain)
            o_fox, o_sb = _attention(q, k, vt, cf)
            mix = ((o_fox, 0), (o_sb, 0))
            w_mix = attn_w_out[i]
        else:
            y = _conv_mixer(h, conv_norm[i][None, :], conv_w_in[i].astype(BF16), conv_kernel[i])
            mix = ((y, 0), (y, 1))
            w_mix = conv_w_out[i]
        h = _mixer_out_ffn(h, mix, w_mix.astype(BF16), ffn_norm[layer][None, :],
                           ffn_w_up[layer].astype(BF16), ffn_conv[layer],
                           ffn_w_down[layer].astype(BF16))
    return h
```

```python
import functools

import jax
import jax.numpy as jnp
from jax import lax
from jax.experimental import pallas as pl
from jax.experimental.pallas import tpu as pltpu

F32 = jnp.float32
BF16 = jnp.bfloat16

D_MODEL = 1024
HEAD_DIM = 64
H_FOX = 8
H_SB = 8
MIX_WIDTH = (H_FOX + H_SB) * HEAD_DIM
D_FF = 2816
EPS = 1e-6
QK_SCALE = HEAD_DIM ** -0.5
LOG2E = 1.4426950408889634

LANES = 128
SUBLANES = 8
HEADS_PER_BLOCK = LANES // HEAD_DIM
BF16_ROWS = 2 * SUBLANES
MXU_DIM = 256
EXP2_CLAMP = 126.0
VMEM_LIMIT = 56 * 1024 * 1024

TM_PROJ = 256
TM_MIXER = 512
TM_FFN = 512
FF_CHUNK = 256
DOWN_TAIL = 2
T_FOX = 512
T_SB = 256
SB_HEADS_PER_STEP = 8
SB_EXIT = 160.0
FOX_EXIT = 160.0
NORM_SLACK = 1.01
CONV_HALO = SUBLANES


def _params(sem, n_inputs=None, fused=()):
    fusion = None if n_inputs is None else [k in fused for k in range(n_inputs)]
    return pltpu.CompilerParams(dimension_semantics=sem, vmem_limit_bytes=VMEM_LIMIT,
                                allow_input_fusion=fusion)


def _resident(shape):
    zeros = (0,) * len(shape)
    return pl.BlockSpec(shape, lambda *_: zeros, pipeline_mode=pl.Buffered(1))


def _rms_norm(x, g):
    ms = jnp.mean(x * x, axis=-1, keepdims=True)
    return x * lax.rsqrt(ms + EPS) * g


def _split3(x):
    hi = x.astype(BF16)
    r = x - hi.astype(F32)
    mid = r.astype(BF16)
    lo = (r - mid.astype(F32)).astype(BF16)
    return hi, mid, lo


def _dot_nt(a, b):
    return lax.dot_general(a, b, (((1,), (1,)), ((), ())), preferred_element_type=F32)


def _causal_conv3(ext_ref, tail, cur, w, tm):
    ext_ref[0:CONV_HALO, :] = tail
    ext_ref[CONV_HALO:CONV_HALO + tm, :] = cur
    return (w[2:3, :] * cur
            + w[1:2, :] * ext_ref[CONV_HALO - 1:CONV_HALO - 1 + tm, :]
            + w[0:1, :] * ext_ref[CONV_HALO - 2:CONV_HALO - 2 + tm, :])


def _attn_inproj_kernel(h_ref, g_ref, w_ref, wvt_ref, wf_ref, fb_ref, qkg_ref, gmat_ref,
                        tri_ref, q_ref, k_ref, vt_ref, cf_ref, carry_ref, *, tm):
    i = pl.program_id(1)

    @pl.when(i == 0)
    def _():
        carry_ref[...] = jnp.zeros_like(carry_ref)

    xn = _rms_norm(h_ref[...], g_ref[...]).astype(BF16)
    gmat = gmat_ref[...]

    n_blk = MIX_WIDTH // MXU_DIM
    proj = [jnp.dot(xn, w_ref[:, c * MXU_DIM:(c + 1) * MXU_DIM], preferred_element_type=F32)
            for c in range(2 * n_blk)]
    vt_ref[...] = _dot_nt(wvt_ref[...], xn).astype(BF16)
    fl = jnp.dot(xn, wf_ref[...], preferred_element_type=F32) + fb_ref[...]

    for c in range(2 * n_blk):
        out_ref, scale = (q_ref, QK_SCALE * LOG2E) if c < n_blk else (k_ref, 1.0)
        t = proj[c]
        ssq = jnp.dot((t * t).astype(BF16), gmat, preferred_element_type=F32)
        gain = qkg_ref[:, c * MXU_DIM:(c + 1) * MXU_DIM]
        tn = t * lax.rsqrt(ssq * (1.0 / HEAD_DIM) + EPS) * gain
        if scale != 1.0:
            tn = tn * scale
        lo = (c % n_blk) * MXU_DIM
        out_ref[:, lo:lo + MXU_DIM] = tn.astype(BF16)

    log_f = jnp.minimum(fl, 0.0) - jnp.log1p(jnp.exp(-jnp.abs(fl)))
    tri = tri_ref[...]
    hi, mid, lo = _split3(log_f)
    cs = (jnp.dot(tri, hi, preferred_element_type=F32)
          + jnp.dot(tri, mid, preferred_element_type=F32)
          + jnp.dot(tri, lo, preferred_element_type=F32))
    cf = cs + carry_ref[0:1, :]
    carry_ref[...] = jnp.broadcast_to(cf[tm - 1:tm, :], carry_ref.shape)
    cf_ref[...] = cf


def _attn_inproj(h, g, w_qk, wvt, wf, fb, qk_gain):
    b, s, d = h.shape
    tm = TM_PROJ
    r = lax.broadcasted_iota(jnp.int32, (MXU_DIM, MXU_DIM), 0) // HEAD_DIM
    c = lax.broadcasted_iota(jnp.int32, (MXU_DIM, MXU_DIM), 1) // HEAD_DIM
    gmat = (r == c).astype(BF16)
    rr = lax.broadcasted_iota(jnp.int32, (tm, tm), 0)
    cc = lax.broadcasted_iota(jnp.int32, (tm, tm), 1)
    tri = (cc <= rr).astype(BF16)
    row_spec = pl.BlockSpec((None, tm, MIX_WIDTH), lambda bi, i: (bi, i, 0))
    return pl.pallas_call(
        functools.partial(_attn_inproj_kernel, tm=tm),
        grid=(b, s // tm),
        in_specs=[
            pl.BlockSpec((None, tm, d), lambda bi, i: (bi, i, 0)),
            _resident(g.shape), _resident(w_qk.shape), _resident(wvt.shape),
            _resident(wf.shape), _resident(fb.shape), _resident(qk_gain.shape),
            _resident(gmat.shape), _resident(tri.shape),
        ],
        out_specs=[row_spec, row_spec,
                   pl.BlockSpec((None, MIX_WIDTH, tm), lambda bi, i: (bi, 0, i)),
                   pl.BlockSpec((None, tm, LANES), lambda bi, i: (bi, i, 0))],
        out_shape=[jax.ShapeDtypeStruct((b, s, MIX_WIDTH), BF16),
                   jax.ShapeDtypeStruct((b, s, MIX_WIDTH), BF16),
                   jax.ShapeDtypeStruct((b, MIX_WIDTH, s), BF16),
                   jax.ShapeDtypeStruct((b, s, LANES), F32)],
        scratch_shapes=[pltpu.VMEM((SUBLANES, LANES), F32)],
        compiler_params=_params(("arbitrary", "arbitrary"), 9, fused=(2, 3, 4)),
        name="attn_inproj",
    )(h, g, w_qk, wvt, wf, fb, qk_gain, gmat, tri)


def _head_masks(t, n_heads=HEADS_PER_BLOCK):
    lane = lax.broadcasted_iota(jnp.int32, (t, n_heads * HEAD_DIM), 1)
    return [(lane >= HEAD_DIM * hh) & (lane < HEAD_DIM * (hh + 1)) for hh in range(n_heads)]


def _lane_pieces(t):
    return [slice(c * LANES, (c + 1) * LANES) for c in range(t // LANES)]


def _sweep_key_tiles(i, scores, step, more_needed, test_every_tile, start_paired):
    def clamp(j):
        return jnp.maximum(j, 0)

    def walk(first, slot_a):
        slot_b = 1 - slot_a
        count = first + 1

        def unfinished(carry):
            r, go = carry
            return jnp.logical_and(r < count // 2, go > 0)

        def pair(carry):
            r, _ = carry
            j = first - 2 * r
            step(j, slot_a, False, (clamp(j - 1), slot_b))
            if not test_every_tile:
                step(j - 1, slot_b, False, (clamp(j - 2), slot_a))
                return r + 1, more_needed(clamp(j - 2))
            go_mid = more_needed(j - 1)

            @pl.when(go_mid > 0)
            def _():
                step(j - 1, slot_b, False, (clamp(j - 2), slot_a))

            return r + 1, go_mid & more_needed(clamp(j - 2))

        _, go = lax.while_loop(unfinished, pair, (jnp.int32(0), more_needed(clamp(first))))

        @pl.when(jnp.logical_and(count % 2 == 1, go > 0))
        def _():
            step(0, slot_a, False, None)

    if not start_paired:
        scores(i, 0)
        step(i, 0, True, (clamp(i - 1), 1))
        walk(i - 1, 1)
        return

    @pl.when(i == 0)
    def _():
        scores(0, 0)
        step(0, 0, True, None)

    @pl.when(i > 0)
    def _():
        scores(i, 0)
        step(i, 0, True, (i - 1, 1))
        step(i - 1, 1, False, (clamp(i - 2), 0))
        walk(i - 2, 0)


def _write_heads(o_ref, outs_t):
    o_ref[...] = jnp.concatenate(outs_t, axis=0).T.astype(o_ref.dtype)


def _fox_parts(q_ref, k_ref, vt_ref, cfb_ref, st_ref, m_ref, acc_ref, t):
    q = q_ref[...]
    qh = [jnp.where(mk, q, jnp.zeros_like(q)) for mk in _head_masks(t)]
    m_ref[...] = jnp.full_like(m_ref, -jnp.inf)
    acc_ref[...] = jnp.zeros_like(acc_ref)
    key = lax.broadcasted_iota(jnp.int32, (t, LANES), 0)
    qry = lax.broadcasted_iota(jnp.int32, (t, LANES), 1)
    pieces = _lane_pieces(t)
    ones_rows = jnp.ones((BF16_ROWS, t), BF16)

    def scores(j, slot, hh):
        off = pl.multiple_of(j * t, t)
        st = _dot_nt(k_ref[pl.ds(off, t), :], qh[hh])
        fcol = cfb_ref[hh, pl.ds(off, t), :]
        for cols in pieces:
            st_ref[slot, hh, :, cols] = st[:, cols] - fcol

    def consume(j, slot, masked, hh):
        off = pl.multiple_of(j * t, t)
        m_old = m_ref[hh]
        m_new, alpha, prob = [], [], []
        for c, cols in enumerate(pieces):
            x = st_ref[slot, hh, :, cols]
            if masked:
                x = jnp.where(key <= qry + c * LANES, x, -jnp.inf)
            m_o = m_old[:, cols]
            m_n = jnp.maximum(m_o, jnp.max(x, axis=0, keepdims=True))
            m_new.append(m_n)
            alpha.append(jnp.exp2(m_o - m_n))
            prob.append(jnp.exp2(x - m_n).astype(BF16))
        vt_h = jnp.concatenate(
            [vt_ref[hh * HEAD_DIM:(hh + 1) * HEAD_DIM, pl.ds(off, t)], ones_rows], axis=0)
        pv = jnp.dot(vt_h, jnp.concatenate(prob, axis=1), preferred_element_type=F32)
        acc_ref[hh] = jnp.concatenate(alpha, axis=1) * acc_ref[hh] + pv
        m_ref[hh] = jnp.concatenate(m_new, axis=1)

    def result():
        return [acc_ref[hh, 0:HEAD_DIM, :] / acc_ref[hh, HEAD_DIM:HEAD_DIM + 1, :]
                for hh in range(HEADS_PER_BLOCK)]

    return scores, consume, result


def _sb_parts(q_ref, k_ref, vt_ref, tri_ref, z_ref, r_ref, acc_ref, t, n_heads):
    group_w = min(n_heads * HEAD_DIM, MXU_DIM)
    per_group = group_w // HEAD_DIM
    masks = _head_masks(t, per_group)
    groups = [slice((hh // per_group) * group_w, (hh // per_group + 1) * group_w)
              for hh in range(n_heads)]
    qh = []
    for hh in range(n_heads):
        qg = q_ref[:, groups[hh]]
        qh.append(jnp.where(masks[hh % per_group], qg, jnp.zeros_like(qg)))
    r_ref[...] = jnp.zeros_like(r_ref)
    acc_ref[...] = jnp.zeros_like(acc_ref)
    key = lax.broadcasted_iota(jnp.int32, (t, t), 0)
    qry = lax.broadcasted_iota(jnp.int32, (t, t), 1)
    strict = key < qry
    n_sub = t // MXU_DIM

    sub_rows = [slice(c * MXU_DIM, (c + 1) * MXU_DIM) for c in range(n_sub)]

    def scores(j, slot, hh):
        off = pl.multiple_of(j * t, t)
        z_ref[slot, hh] = _dot_nt(k_ref[pl.ds(off, t), groups[hh]], qh[hh])

    def gate_sums(slot, masked, hh):
        z = z_ref[slot, hh]
        a = jnp.maximum(LOG2E * jnp.log(1.0 + jnp.exp2(jnp.minimum(z, EXP2_CLAMP))), z)
        base = z - a
        if masked:
            a = jnp.where(strict, a, 0.0)
            base = jnp.where(strict, base, -jnp.inf)
        tri = tri_ref[...]
        sums = [jnp.dot(tri, a[rows, :].astype(BF16), preferred_element_type=F32)
                for rows in sub_rows]
        return base, sums

    def accumulate(j, staged, hh):
        off = pl.multiple_of(j * t, t)
        base, sums = staged
        r_run = r_ref[hh]
        w = [None] * n_sub
        for c in reversed(range(n_sub)):
            later = sums[c][0:MXU_DIM, :]
            w[c] = jnp.exp2(base[sub_rows[c], :] - later - r_run).astype(BF16)
            r_run = r_run + sums[c][MXU_DIM:MXU_DIM + 1, :]
        vt_h = vt_ref[hh * HEAD_DIM:(hh + 1) * HEAD_DIM, pl.ds(off, t)]
        acc_ref[hh] += jnp.dot(vt_h, jnp.concatenate(w, axis=0), preferred_element_type=F32)
        r_ref[hh] = r_run

    def result():
        return [acc_ref[hh] for hh in range(n_heads)]

    return scores, gate_sums, accumulate, result


def _fox_kernel(q_ref, k_ref, vt_ref, cf_ref, o_ref, cfb_ref, kmax_ref, st_ref, m_ref, acc_ref,
                *, t):
    p = pl.program_id(1)
    i = pl.program_id(2)
    heads = range(HEADS_PER_BLOCK)
    lane_row = lax.broadcasted_iota(jnp.int32, (LANES, LANES), 0)
    head_rows = [((lane_row >= HEAD_DIM * hh) & (lane_row < HEAD_DIM * (hh + 1))).astype(BF16)
                 for hh in heads]

    @pl.when(i == 0)
    def _():
        for hh in heads:
            sel = (lane_row == p * HEADS_PER_BLOCK + hh).astype(BF16)

            def fill(r, kmax2):
                off = pl.multiple_of(r * t, t)
                hi, mid, lo = _split3(cf_ref[pl.ds(off, t), :])
                cfb_ref[hh, pl.ds(off, t), :] = LOG2E * (
                    jnp.dot(hi, sel, preferred_element_type=F32)
                    + jnp.dot(mid, sel, preferred_element_type=F32)
                    + jnp.dot(lo, sel, preferred_element_type=F32))
                kt = k_ref[pl.ds(off, t), :].astype(F32)
                ksq = jnp.dot((kt * kt).astype(BF16), head_rows[hh],
                              preferred_element_type=F32)
                return jnp.maximum(kmax2, ksq)

            kmax2 = lax.fori_loop(0, cf_ref.shape[0] // t, fill, jnp.zeros((t, LANES), F32))
            kmax_ref[hh] = jnp.max(kmax2, axis=0, keepdims=True)

    scores_h, consume_h, result = _fox_parts(q_ref, k_ref, vt_ref, cfb_ref, st_ref, m_ref,
                                             acc_ref, t)

    q32 = q_ref[...].astype(F32)
    ones_lhs = jnp.ones((BF16_ROWS, LANES), BF16)
    qk_bound = []
    for hh, mk in zip(heads, _head_masks(t)):
        qsq = jnp.where(mk, q32 * q32, 0.0).astype(BF16)
        qn2 = _dot_nt(ones_lhs, qsq)[0:1, :]
        qk_bound.append(NORM_SLACK * jnp.sqrt(qn2 * kmax_ref[hh][:, 0:1]))

    def more_needed(j):
        row = (j + 1) * t - 1
        need = jnp.int32(0)
        for hh in heads:
            f_hi = cfb_ref[hh, pl.ds(row, 1), :][:, 0:1]
            gap = jnp.max(qk_bound[hh] - f_hi - m_ref[hh])
            need = need | jnp.logical_not(gap <= -FOX_EXIT).astype(jnp.int32)
        return need

    def scores(j, slot):
        for hh in heads:
            scores_h(j, slot, hh)

    def step(j, slot, masked, nxt):
        for hh in heads:
            if nxt is not None:
                scores_h(*nxt, hh)
            consume_h(j, slot, masked, hh)

    _sweep_key_tiles(i, scores, step, more_needed, test_every_tile=False, start_paired=False)
    _write_heads(o_ref, result())


def _sb_kernel(q_ref, k_ref, vt_ref, tri_ref, o_ref, z_ref, r_ref, acc_ref, *, t, n_heads):
    i = pl.program_id(2)
    scores_h, gate_sums_h, accumulate_h, result = _sb_parts(
        q_ref, k_ref, vt_ref, tri_ref, z_ref, r_ref, acc_ref, t, n_heads)
    heads = range(n_heads)

    def scores(j, slot):
        for hh in heads:
            scores_h(j, slot, hh)

    def step(j, slot, masked, nxt):
        staged = []
        for hh in heads:
            if nxt is not None:
                scores_h(*nxt, hh)
            staged.append(gate_sums_h(slot, masked, hh))
        for hh in heads:
            accumulate_h(j, staged[hh], hh)

    def more_needed(_):
        return (jnp.min(r_ref[...]) < SB_EXIT).astype(jnp.int32)

    _sweep_key_tiles(i, scores, step, more_needed, test_every_tile=True, start_paired=True)
    _write_heads(o_ref, result())


def _attention(q, k, vt, cf):
    b, s, _ = q.shape
    rr = lax.broadcasted_iota(jnp.int32, (MXU_DIM + BF16_ROWS, MXU_DIM), 0)
    cc = lax.broadcasted_iota(jnp.int32, (MXU_DIM + BF16_ROWS, MXU_DIM), 1)
    tri = ((cc > rr) | (rr >= MXU_DIM)).astype(BF16)

    def specs(t, w, first):
        return [pl.BlockSpec((None, t, w), lambda bi, p, i: (bi, i, p + first)),
                pl.BlockSpec((None, s, w), lambda bi, p, i: (bi, 0, p + first)),
                pl.BlockSpec((None, w, s), lambda bi, p, i: (bi, p + first, 0))]

    def out_spec(t, w):
        return pl.BlockSpec((None, t, w), lambda bi, p, i: (bi, i, p))

    params = _params(("arbitrary", "arbitrary", "arbitrary"))
    t = T_FOX
    o_fox = pl.pallas_call(
        functools.partial(_fox_kernel, t=t),
        grid=(b, H_FOX // HEADS_PER_BLOCK, s // t),
        in_specs=specs(t, LANES, 0) + [
            pl.BlockSpec((None, s, LANES), lambda bi, p, i: (bi, 0, 0),
                         pipeline_mode=pl.Buffered(1))],
        out_specs=out_spec(t, LANES),
        out_shape=jax.ShapeDtypeStruct((b, s, H_FOX * HEAD_DIM), BF16),
        scratch_shapes=[pltpu.VMEM((HEADS_PER_BLOCK, s, LANES), F32),
                        pltpu.VMEM((HEADS_PER_BLOCK, 1, LANES), F32),
                        pltpu.VMEM((2, HEADS_PER_BLOCK, t, t), F32),
                        pltpu.VMEM((HEADS_PER_BLOCK, 1, t), F32),
                        pltpu.VMEM((HEADS_PER_BLOCK, HEAD_DIM + BF16_ROWS, t), F32)],
        compiler_params=params,
        name="fox_attention",
    )(q, k, vt, cf)
    t = T_SB
    nh = SB_HEADS_PER_STEP
    w = nh * HEAD_DIM
    o_sb = pl.pallas_call(
        functools.partial(_sb_kernel, t=t, n_heads=nh),
        grid=(b, H_SB // nh, s // t),
        in_specs=specs(t, w, H_FOX // nh) + [_resident(tri.shape)],
        out_specs=out_spec(t, w),
        out_shape=jax.ShapeDtypeStruct((b, s, H_SB * HEAD_DIM), BF16),
        scratch_shapes=[pltpu.VMEM((2, nh, t, t), F32),
                        pltpu.VMEM((nh, 1, t), F32),
                        pltpu.VMEM((nh, HEAD_DIM, t), F32)],
        compiler_params=params,
        name="sb_attention",
    )(q, k, vt, tri)
    return o_fox, o_sb


def _conv_mixer_kernel(h_ref, g_ref, win_ref, cw_ref, o_ref, ext_ref, tail_ref, *, tm):
    i = pl.program_id(1)

    @pl.when(i == 0)
    def _():
        tail_ref[...] = jnp.zeros_like(tail_ref)

    x = h_ref[...]
    xn = _rms_norm(x, g_ref[...]).astype(BF16)
    d = x.shape[-1]
    gate_c = jnp.dot(xn, win_ref[:, d:2 * d], preferred_element_type=F32)
    u = jnp.dot(xn, win_ref[:, 2 * d:3 * d], preferred_element_type=F32)
    cu = gate_c * u
    y = _causal_conv3(ext_ref, tail_ref[...], cu, cw_ref[...], tm)
    tail_ref[...] = cu[tm - CONV_HALO:tm, :]
    gate_b = jnp.dot(xn, win_ref[:, 0:d], preferred_element_type=F32)
    o_ref[...] = (gate_b * y).astype(o_ref.dtype)


def _conv_mixer(h, g, w_in, conv_w):
    b, s, d = h.shape
    tm = TM_MIXER
    return pl.pallas_call(
        functools.partial(_conv_mixer_kernel, tm=tm),
        grid=(b, s // tm),
        in_specs=[
            pl.BlockSpec((None, tm, d), lambda bi, i: (bi, i, 0)),
            _resident(g.shape), _resident(w_in.shape), _resident(conv_w.shape),
        ],
        out_specs=pl.BlockSpec((None, tm, d), lambda bi, i: (bi, i, 0)),
        out_shape=jax.ShapeDtypeStruct(h.shape, BF16),
        scratch_shapes=[pltpu.VMEM((tm + CONV_HALO, d), F32),
                        pltpu.VMEM((CONV_HALO, d), F32)],
        compiler_params=_params(("arbitrary", "arbitrary"), 4, fused=(2,)),
        name="conv_mixer",
    )(h, g, w_in, conv_w)


def _ffn_kernel(h_ref, a0_ref, a1_ref, w0_ref, w1_ref, g_ref, wup_ref, cw_ref, wd_ref, o_ref,
                xn_ref, ext_ref, tail_ref, act_ref, *, tm, fc):
    i = pl.program_id(1)
    n_chunks = D_FF // fc

    @pl.when(i == 0)
    def _():
        tail_ref[...] = jnp.zeros_like(tail_ref)

    x = (h_ref[...]
         + jnp.dot(a0_ref[...], w0_ref[...], preferred_element_type=F32)
         + jnp.dot(a1_ref[...], w1_ref[...], preferred_element_type=F32))
    xn_ref[...] = _rms_norm(x, g_ref[...]).astype(BF16)
    o_ref[...] = x

    def up_project(c):
        slot = c % 2
        for part in range(2):
            cols = slice(part * D_FF + c * fc, part * D_FF + (c + 1) * fc)
            ext_ref[slot, part, 0:CONV_HALO, :] = tail_ref[part, :, c * fc:(c + 1) * fc]
            ext_ref[slot, part, CONV_HALO:CONV_HALO + tm, :] = jnp.dot(
                xn_ref[...], wup_ref[:, cols], preferred_element_type=F32)

    def conv_gate(c):
        slot = c % 2
        y = []
        for part in range(2):
            cols = slice(part * D_FF + c * fc, part * D_FF + (c + 1) * fc)
            w = cw_ref[:, cols]
            taps = [w[k:k + 1, :] * ext_ref[slot, part, CONV_HALO - 2 + k:CONV_HALO - 2 + k + tm, :]
                    for k in range(3)]
            y.append(taps[2] + taps[1] + taps[0])
            tail_ref[part, :, c * fc:(c + 1) * fc] = ext_ref[slot, part, tm:tm + CONV_HALO, :]
        half_g = 0.5 * y[0]
        act_ref[:, c * fc:(c + 1) * fc] = (
            (half_g + half_g * jnp.tanh(half_g)) * y[1]).astype(BF16)

    def down_project(rows):
        o_ref[...] += jnp.dot(act_ref[:, rows], wd_ref[rows, :], preferred_element_type=F32)

    head = n_chunks - DOWN_TAIL
    up_project(0)
    for c in range(n_chunks):
        if c + 1 < n_chunks:
            up_project(c + 1)
        if c == head:
            down_project(slice(0, head * fc))
        conv_gate(c)
    down_project(slice(head * fc, D_FF))


def _mixer_out_ffn(h, mix, w_mix, g, w_up, conv_w, w_down):
    b, s, d = h.shape
    tm = TM_FFN
    fc = FF_CHUNK
    half = d // 2
    (a0, blk0), (a1, blk1) = mix
    w0, w1 = w_mix[:half], w_mix[half:]

    def half_spec(blk):
        return pl.BlockSpec((None, tm, half), lambda bi, i: (bi, i, blk))

    return pl.pallas_call(
        functools.partial(_ffn_kernel, tm=tm, fc=fc),
        grid=(b, s // tm),
        in_specs=[
            pl.BlockSpec((None, tm, d), lambda bi, i: (bi, i, 0)),
            half_spec(blk0), half_spec(blk1), _resident(w0.shape), _resident(w1.shape),
            _resident(g.shape), _resident(w_up.shape), _resident(conv_w.shape),
            _resident(w_down.shape),
        ],
        out_specs=pl.BlockSpec((None, tm, d), lambda bi, i: (bi, i, 0)),
        out_shape=jax.ShapeDtypeStruct(h.shape, F32),
        scratch_shapes=[pltpu.VMEM((tm, d), BF16),
                        pltpu.VMEM((2, 2, tm + CONV_HALO, fc), F32),
                        pltpu.VMEM((2, CONV_HALO, D_FF), F32),
                        pltpu.VMEM((tm, D_FF), BF16)],
        compiler_params=_params(("arbitrary", "arbitrary"), 9, fused=(3, 4, 6, 8)),
        name="conv_ffn",
    )(h, a0, a1, w0, w1, g, w_up, conv_w, w_down)


def kernel(x, attn_norm, attn_w_in, attn_f_bias, fox_q_gain, fox_k_gain, sb_q_gain, sb_k_gain,
           attn_w_out, conv_norm, conv_w_in, conv_kernel, conv_w_out, ffn_norm, ffn_w_up,
           ffn_conv, ffn_w_down):
    depth = ffn_norm.shape[0]
    h = x
    for layer in range(depth):
        i = layer // 2
        if layer % 2 == 0:
            w_in = attn_w_in[i]
            w_qk = w_in[:, :2 * MIX_WIDTH].astype(BF16)
            wvt = w_in[:, 2 * MIX_WIDTH:3 * MIX_WIDTH].T.astype(BF16)
            wf = jnp.zeros((D_MODEL, LANES), BF16).at[:, :H_FOX].set(
                w_in[:, 3 * MIX_WIDTH:].astype(BF16))
            fb = jnp.zeros((1, LANES), F32).at[0, :H_FOX].set(attn_f_bias[i])
            qk_gain = jnp.concatenate(
                [jnp.tile(fox_q_gain[i], H_FOX), jnp.tile(sb_q_gain[i], H_SB),
                 jnp.tile(fox_k_gain[i], H_FOX), jnp.tile(sb_k_gain[i], H_SB)])[None, :]
            q, k, vt, cf = _attn_inproj(h, attn_norm[i][None, :], w_qk, wvt, wf, fb, qk_gain)
            o_fox, o_sb = _attention(q, k, vt, cf)
            mix = ((o_fox, 0), (o_sb, 0))
            w_mix = attn_w_out[i]
        else:
            y = _conv_mixer(h, conv_norm[i][None, :], conv_w_in[i].astype(BF16), conv_kernel[i])
            mix = ((y, 0), (y, 1))
            w_mix = conv_w_out[i]
        h = _mixer_out_ffn(h, mix, w_mix.astype(BF16), ffn_norm[layer][None, :],
                           ffn_w_up[layer].astype(BF16), ffn_conv[layer],
                           ffn_w_down[layer].astype(BF16))
    return h
```
